```python
import math
import jax
import jax.numpy as jnp
from jax import lax
import numpy as np

D_MODEL = 2048
BATCH = 2
SEQ = 4096
DEPTH = 1
DEC_BATCH = 32
DEC_SEQ = 4
PAST_LEN = 8192
PAGE_SIZE = 128

DA_HEADS = 8
DA_KV = 4
DA_REP = DA_HEADS // DA_KV
DA_DH = 64
DA_QK = 2 * DA_DH
DA_DV = 2 * DA_DH
NSA_HEADS = 16
NSA_KV = 2
NSA_REP = NSA_HEADS // NSA_KV
NSA_DH = 64
CMP_BLOCK = 64
SLC_BLOCK = 64
N_SELECT = 16
WINDOW = 512
FORCE_BONUS = 1000.0
ROPE_THETA = 500000.0
ROPE_DIMS = DA_DH // 4
N_GROUPS = 8
EXP_PER_GROUP = 8
N_EXPERTS = N_GROUPS * EXP_PER_GROUP
TOP_K = 2
D_EXPERT = 512
MOE_BLOCK = 64
D_PLE = 256
Q_BLOCK = 128
EPS = 1e-6
NEG_INF = -1e30

C_DA_Q = DA_HEADS * DA_QK
C_DA_K = DA_KV * DA_QK
C_DA_V = DA_KV * DA_DV
C_NSA_Q = NSA_HEADS * NSA_DH
C_NSA_KV = 6 * NSA_KV * NSA_DH
C_NSA_G = 3 * NSA_HEADS
C_MERGE = 2 * D_MODEL
S1 = C_DA_Q
S2 = S1 + C_DA_K
S3 = S2 + C_DA_V
S4 = S3 + C_NSA_Q
S5 = S4 + C_NSA_KV
S6 = S5 + C_NSA_G
D_IN = S6 + C_MERGE
SPLITS = (S1, S2, S3, S4, S5, S6)

kernel_name = 'hybrid_diffattn_nsa_hmoe_step'


def rms_norm(x, g):
    xf = x.astype(jnp.float32)
    y = xf * lax.rsqrt(jnp.mean(xf * xf, axis=-1, keepdims=True) + EPS)
    return (y * g.astype(jnp.float32)).astype(x.dtype)


def rope(x, pos):
    half = ROPE_DIMS // 2
    inv = ROPE_THETA ** (-jnp.arange(half, dtype=jnp.float32) / half)
    ang = pos.astype(jnp.float32)[:, None] * inv[None, :]
    shp = (1, pos.shape[0]) + (1,) * (x.ndim - 3) + (half,)
    cos = jnp.cos(ang).reshape(shp)
    sin = jnp.sin(ang).reshape(shp)
    xf = x.astype(jnp.float32)
    x1 = xf[..., :half]
    x2 = xf[..., half:ROPE_DIMS]
    out = jnp.concatenate([x1 * cos - x2 * sin, x2 * cos + x1 * sin, xf[..., ROPE_DIMS:]], axis=-1)
    return out.astype(x.dtype)


def masked_softmax(s, mask):
    p = jax.nn.softmax(jnp.where(mask, s, NEG_INF), axis=-1)
    return jnp.where(mask, p, 0.0)


def over_query_blocks(fn, q, q_pos):
    B, S = q.shape[:2]
    qb = Q_BLOCK if S % Q_BLOCK == 0 else S
    nq = S // qb
    qs = jnp.swapaxes(q.reshape((B, nq, qb) + q.shape[2:]), 0, 1)
    ps = q_pos.reshape(nq, qb)
    out = lax.map(lambda a: fn(a[0], a[1]), (qs, ps))
    return jax.tree_util.tree_map(lambda o: jnp.swapaxes(o, 0, 1).reshape((B, S) + o.shape[3:]), out)


def mixer_inputs(h, pos, w_in):
    B, S, _ = h.shape
    z = h @ w_in
    q_da, k_da, v_da, q_ns, kv_ns, g_ns, g_mg = jnp.split(z, SPLITS, axis=-1)
    q_da = rope(q_da.reshape(B, S, DA_HEADS, 2, DA_DH), pos)
    k_da = rope(k_da.reshape(B, S, DA_KV, 2, DA_DH), pos).reshape(B, S, DA_KV, DA_QK)
    da_kv = jnp.stack([k_da, v_da.reshape(B, S, DA_KV, DA_DV)], axis=2)
    q_ns = rope(q_ns.reshape(B, S, NSA_HEADS, NSA_DH), pos)
    kv_ns = kv_ns.reshape(B, S, 3, 2, NSA_KV, NSA_DH)
    kv_ns = jnp.stack([rope(kv_ns[:, :, :, 0], pos), kv_ns[:, :, :, 1]], axis=3)
    kv_ns = kv_ns.reshape(B, S, 6, NSA_KV, NSA_DH)
    g_ns = jax.nn.sigmoid(g_ns.reshape(B, S, NSA_HEADS, 3))
    g_mg = jax.nn.sigmoid(g_mg.reshape(B, S, 2, D_MODEL))
    return q_da, da_kv, q_ns, kv_ns[:, :, :4], kv_ns[:, :, 4:], g_ns, g_mg


def diff_attention(q, kv, q_pos, lam, subln, lam_init):
    B, T = kv.shape[:2]
    S = q.shape[1]
    k = kv[:, :, 0].reshape(B, T, DA_KV, 2, DA_DH)
    v = kv[:, :, 1]
    k_pos = jnp.arange(T)
    scale = DA_DH ** -0.5

    def block(qc, pc):
        qb = qc.shape[1]
        qg = qc.reshape(B, qb, DA_KV, DA_REP, 2, DA_DH)
        s = jnp.einsum('bqgrcd,bkgcd->bgrcqk', qg, k, preferred_element_type=jnp.float32) * scale
        p = masked_softmax(s, k_pos[None, :] <= pc[:, None])
        a = p[:, :, :, 0] - lam * p[:, :, :, 1]
        o = jnp.einsum('bgrqk,bkgd->bqgrd', a.astype(v.dtype), v)
        return o.reshape(B, qb, DA_HEADS, DA_DV)

    o = over_query_blocks(block, q, q_pos)
    o = rms_norm(o, subln) * (1.0 - lam_init)
    return o.reshape(B, S, DA_HEADS * DA_DV)


def nsa_global(q, kv, q_pos, w_cpos, w_cmp):
    B, T = kv.shape[:2]
    nc = T // CMP_BLOCK
    blk = kv[:, :nc * CMP_BLOCK, :2].reshape(B, nc, CMP_BLOCK, 2, NSA_KV, NSA_DH)
    cmp = jnp.einsum('bnlcgd,cld->bncgd', blk, w_cpos)
    cmp = jnp.einsum('bncgd,cde->bncge', cmp, w_cmp)
    k_cmp, v_cmp = cmp[:, :, 0], cmp[:, :, 1]
    cmp_end = (jnp.arange(nc) + 1) * CMP_BLOCK - 1
    ns = max(-(-T // SLC_BLOCK), N_SELECT)
    sel = jnp.pad(kv[:, :, 2:], ((0, 0), (0, ns * SLC_BLOCK - T), (0, 0), (0, 0), (0, 0)))
    sel = sel.reshape(B, ns, SLC_BLOCK, 2, NSA_KV, NSA_DH).transpose(0, 4, 1, 2, 3, 5)
    blk_ids = jnp.arange(ns)
    b_idx = jnp.arange(B)[:, None, None, None]
    g_idx = jnp.arange(NSA_KV)[None, None, :, None]
    scale = NSA_DH ** -0.5

    def block(qc, pc):
        qb = qc.shape[1]
        qg = qc.reshape(B, qb, NSA_KV, NSA_REP, NSA_DH)
        s = jnp.einsum('bqgrd,bngd->bqgrn', qg, k_cmp, preferred_element_type=jnp.float32) * scale
        ok = cmp_end[None, :] <= pc[:, None]
        p = masked_softmax(s, ok[None, :, None, None, :])
        o_cmp = jnp.einsum('bqgrn,bngd->bqgrd', p.astype(v_cmp.dtype), v_cmp)
        imp = jnp.pad(jnp.sum(p, axis=3), ((0, 0), (0, 0), (0, 0), (0, ns - nc)))
        cur = pc // SLC_BLOCK
        forced = (blk_ids[None, :] == 0) | (blk_ids[None, :] == cur[:, None]) | (blk_ids[None, :] == cur[:, None] - 1)
        score = jnp.where(forced[None, :, None, :], imp + FORCE_BONUS, imp)
        score = jnp.where((blk_ids[None, :] <= cur[:, None])[None, :, None, :], score, -1.0)
        _, idx = lax.top_k(score, N_SELECT)
        kvs = sel[b_idx, g_idx, idx]
        kpos = idx[..., None] * SLC_BLOCK + jnp.arange(SLC_BLOCK)
        s2 = jnp.einsum('bqgrd,bqgnld->bqgrnl', qg, kvs[..., 0, :], preferred_element_type=jnp.float32) * scale
        s2 = s2.reshape(B, qb, NSA_KV, NSA_REP, N_SELECT * SLC_BLOCK)
        ok2 = (kpos <= pc[None, :, None, None, None]).reshape(B, qb, NSA_KV, 1, N_SELECT * SLC_BLOCK)
        p2 = masked_softmax(s2, ok2).reshape(B, qb, NSA_KV, NSA_REP, N_SELECT, SLC_BLOCK)
        o_slc = jnp.einsum('bqgrnl,bqgnld->bqgrd', p2.astype(kvs.dtype), kvs[..., 1, :])
        return (o_cmp.reshape(B, qb, NSA_HEADS, NSA_DH), o_slc.reshape(B, qb, NSA_HEADS, NSA_DH))

    return over_query_blocks(block, q, q_pos)


def window_attention(q, kvw, q_pos, k_start):
    B = q.shape[0]
    kpad = jnp.pad(kvw, ((0, 0), (WINDOW, 0), (0, 0), (0, 0), (0, 0)))
    scale = NSA_DH ** -0.5

    def block(qc, pc):
        qb = qc.shape[1]
        kb = lax.dynamic_slice_in_dim(kpad, pc[0] - k_start, WINDOW + qb, axis=1)
        kp = pc[0] - WINDOW + jnp.arange(WINDOW + qb)
        qg = qc.reshape(B, qb, NSA_KV, NSA_REP, NSA_DH)
        s = jnp.einsum('bqgrd,bkgd->bqgrk', qg, kb[:, :, 0], preferred_element_type=jnp.float32) * scale
        d = pc[:, None] - kp[None, :]
        mask = (d >= 0) & (d < WINDOW) & (kp[None, :] >= 0)
        p = masked_softmax(s, mask[None, :, None, None, :])
        o = jnp.einsum('bqgrk,bkgd->bqgrd', p.astype(kb.dtype), kb[:, :, 1])
        return o.reshape(B, qb, NSA_HEADS, NSA_DH)

    return over_query_blocks(block, q, q_pos)


def hier_moe(h, w_rg, b_rg, w_re, b_re, w_g, w_u, w_d):
    hf = h.reshape(-1, D_MODEL)
    N = hf.shape[0]
    h32 = hf.astype(jnp.float32)
    g_logit = h32 @ w_rg.astype(jnp.float32) + b_rg.astype(jnp.float32)
    g_prob = jax.nn.softmax(g_logit, axis=-1)
    g_sel = jnp.argmax(g_logit, axis=-1)
    p_grp = jnp.take_along_axis(g_prob, g_sel[:, None], axis=1)[:, 0]
    e_logit = (h32 @ w_re.astype(jnp.float32) + b_re.astype(jnp.float32)).reshape(N, N_GROUPS, EXP_PER_GROUP)
    e_logit = jnp.take_along_axis(e_logit, g_sel[:, None, None], axis=1)[:, 0]
    top_v, top_i = lax.top_k(e_logit, TOP_K)
    w_tok = jax.nn.softmax(top_v, axis=-1) * p_grp[:, None]
    eid = (g_sel[:, None] * EXP_PER_GROUP + top_i).reshape(-1)
    A = N * TOP_K
    order = jnp.argsort(eid)
    e_sorted = eid[order]
    tok_sorted = order // TOP_K
    sizes = jnp.bincount(eid, length=N_EXPERTS)
    padded = (sizes + MOE_BLOCK - 1) // MOE_BLOCK * MOE_BLOCK
    start = jnp.cumsum(sizes) - sizes
    ends = jnp.cumsum(padded)
    pstart = ends - padded
    dest = pstart[e_sorted] + jnp.arange(A) - start[e_sorted]
    nb = -(-A // MOE_BLOCK) + N_EXPERTS
    rows = jnp.zeros((nb * MOE_BLOCK, D_MODEL), hf.dtype).at[dest].set(hf[tok_sorted])
    blk_e = jnp.clip(jnp.searchsorted(ends, jnp.arange(nb) * MOE_BLOCK, side='right'), 0, N_EXPERTS - 1)

    def expert_block(args):
        xb, e = args
        return (jax.nn.silu(xb @ w_g[e]) * (xb @ w_u[e])) @ w_d[e]

    out = lax.map(expert_block, (rows.reshape(nb, MOE_BLOCK, D_MODEL), blk_e)).reshape(-1, D_MODEL)
    y = out[dest] * w_tok.reshape(-1)[order][:, None].astype(out.dtype)
    y = jax.ops.segment_sum(y, tok_sorted, num_segments=N)
    return y.reshape(h.shape).astype(h.dtype)


def layer_tail(x, o_da, o_cmp, o_slc, o_win, g_ns, g_mg, p, w_da_up, w_nsa_up, w_o, g_ffn,
               w_rg, b_rg, w_re, b_re, w_e_gate, w_e_up, w_e_down, g_ple, w_ple_gate, w_ple):
    B, S, _ = x.shape
    o_ns = g_ns[..., 0:1] * o_cmp + g_ns[..., 1:2] * o_slc + g_ns[..., 2:3] * o_win
    y_da = o_da @ w_da_up
    y_ns = o_ns.reshape(B, S, NSA_HEADS * NSA_DH) @ w_nsa_up
    x = x + (g_mg[:, :, 0] * y_da + g_mg[:, :, 1] * y_ns) @ w_o
    x = x + hier_moe(rms_norm(x, g_ffn), w_rg, b_rg, w_re, b_re, w_e_gate, w_e_up, w_e_down)
    gate = jax.nn.sigmoid(rms_norm(x, g_ple) @ w_ple_gate)
    return x + gate * (p @ w_ple)


def setup_inputs(seed: int = 0) -> dict:
    key = jax.random.key(seed)
    ks = iter(jax.random.split(key, 40))

    def nrm(shape, scale):
        return jax.random.normal(next(ks), shape, jnp.float32) * scale

    n_pages = PAST_LEN // PAGE_SIZE
    n_pool = (DEC_BATCH * n_pages * 5) // 4
    wb = min(WINDOW, PAST_LEN)
    perm = jax.random.permutation(next(ks), n_pool)[:DEC_BATCH * n_pages]
    page_table = perm.reshape(DEC_BATCH, n_pages).astype(jnp.int32)
    return {
        'x_prompt': nrm((BATCH, SEQ, D_MODEL), 1.0),
        'x_sample': nrm((DEC_BATCH, DEC_SEQ, D_MODEL), 1.0),
        'cache_da_kv': nrm((DEPTH, n_pool, PAGE_SIZE, 2, DA_KV, DA_QK), 1.0),
        'cache_nsa_kv': nrm((DEPTH, n_pool, PAGE_SIZE, 4, NSA_KV, NSA_DH), 1.0),
        'cache_win_kv': nrm((DEPTH, DEC_BATCH, wb, 2, NSA_KV, NSA_DH), 1.0),
        'page_table': page_table,
        'p_prompt': nrm((DEPTH, BATCH, SEQ, D_PLE), 1.0),
        'p_sample': nrm((DEPTH, DEC_BATCH, DEC_SEQ, D_PLE), 1.0),
        'g_mix': 1.0 + nrm((DEPTH, D_MODEL), 0.01),
        'w_in': nrm((DEPTH, D_MODEL, D_IN), D_MODEL ** -0.5),
        'da_lambda': nrm((DEPTH, 4, DA_DH), 0.1),
        'da_subln': 1.0 + nrm((DEPTH, DA_DV), 0.01),
        'nsa_cpos': CMP_BLOCK ** -0.5 * (1.0 + nrm((DEPTH, 2, CMP_BLOCK, NSA_DH), 0.1)),
        'nsa_cmp': nrm((DEPTH, 2, NSA_DH, NSA_DH), NSA_DH ** -0.5),
        'w_da_up': nrm((DEPTH, DA_HEADS * DA_DV, D_MODEL), (DA_HEADS * DA_DV) ** -0.5),
        'w_nsa_up': nrm((DEPTH, NSA_HEADS * NSA_DH, D_MODEL), (NSA_HEADS * NSA_DH) ** -0.5),
        'w_o': nrm((DEPTH, D_MODEL, D_MODEL), D_MODEL ** -0.5),
        'g_ffn': 1.0 + nrm((DEPTH, D_MODEL), 0.01),
        'w_rg': nrm((DEPTH, D_MODEL, N_GROUPS), D_MODEL ** -0.5),
        'b_rg': nrm((DEPTH, N_GROUPS), 0.01),
        'w_re': nrm((DEPTH, D_MODEL, N_EXPERTS), D_MODEL ** -0.5),
        'b_re': nrm((DEPTH, N_EXPERTS), 0.01),
        'w_e_gate': nrm((DEPTH, N_EXPERTS, D_MODEL, D_EXPERT), D_MODEL ** -0.5),
        'w_e_up': nrm((DEPTH, N_EXPERTS, D_MODEL, D_EXPERT), D_MODEL ** -0.5),
        'w_e_down': nrm((DEPTH, N_EXPERTS, D_EXPERT, D_MODEL), D_EXPERT ** -0.5),
        'g_ple': 1.0 + nrm((DEPTH, D_MODEL), 0.01),
        'w_ple_gate': nrm((DEPTH, D_MODEL, D_MODEL), D_MODEL ** -0.5),
        'w_ple': nrm((DEPTH, D_PLE, D_MODEL), D_PLE ** -0.5),
        'g_final': 1.0 + nrm((D_MODEL,), 0.01),
    }


def reference(x_prompt, x_sample, cache_da_kv, cache_nsa_kv, cache_win_kv, page_table, p_prompt, p_sample,
              g_mix, w_in, da_lambda, da_subln, nsa_cpos, nsa_cmp, w_da_up, w_nsa_up, w_o, g_ffn,
              w_rg, b_rg, w_re, b_re, w_e_gate, w_e_up, w_e_down, g_ple, w_ple_gate, w_ple, g_final):
    xp, xs = x_prompt, x_sample
    s_p = xp.shape[1]
    n_dec = xs.shape[0]
    pos_p = jnp.arange(s_p)
    pos_s = PAST_LEN + jnp.arange(xs.shape[1])
    wb = cache_win_kv.shape[2]
    new_da_p, new_da_s, new_ns_p, new_ns_s, new_win_p, new_win_s = [], [], [], [], [], []
    for i in range(DEPTH):
        lam_init = 0.8 - 0.6 * math.exp(-0.3 * i)
        lp = da_lambda[i].astype(jnp.float32)
        lam = jnp.exp(jnp.sum(lp[0] * lp[1])) - jnp.exp(jnp.sum(lp[2] * lp[3])) + lam_init
        tail_w = (w_da_up[i], w_nsa_up[i], w_o[i], g_ffn[i], w_rg[i], b_rg[i], w_re[i], b_re[i],
                  w_e_gate[i], w_e_up[i], w_e_down[i], g_ple[i], w_ple_gate[i], w_ple[i])
        q_da, da_kv, q_ns, ns_kv, win_kv, g_ns, g_mg = mixer_inputs(rms_norm(xp, g_mix[i]), pos_p, w_in[i])
        o_da = diff_attention(q_da, da_kv, pos_p, lam, da_subln[i], lam_init)
        o_cmp, o_slc = nsa_global(q_ns, ns_kv, pos_p, nsa_cpos[i], nsa_cmp[i])
        o_win = window_attention(q_ns, win_kv, pos_p, 0)
        xp = layer_tail(xp, o_da, o_cmp, o_slc, o_win, g_ns, g_mg, p_prompt[i], *tail_w)
        new_da_p.append(da_kv)
        new_ns_p.append(ns_kv)
        new_win_p.append(win_kv[:, s_p - min(WINDOW, s_p):])
        q_da, da_kv, q_ns, ns_kv, win_kv, g_ns, g_mg = mixer_inputs(rms_norm(xs, g_mix[i]), pos_s, w_in[i])
        past_da = cache_da_kv[i][page_table].reshape((n_dec, -1) + cache_da_kv.shape[3:])
        past_ns = cache_nsa_kv[i][page_table].reshape((n_dec, -1) + cache_nsa_kv.shape[3:])
        o_da = diff_attention(q_da, jnp.concatenate([past_da, da_kv], axis=1), pos_s, lam, da_subln[i], lam_init)
        o_cmp, o_slc = nsa_global(q_ns, jnp.concatenate([past_ns, ns_kv], axis=1), pos_s, nsa_cpos[i], nsa_cmp[i])
        win_all = jnp.concatenate([cache_win_kv[i], win_kv], axis=1)
        o_win = window_attention(q_ns, win_all, pos_s, PAST_LEN - wb)
        xs = layer_tail(xs, o_da, o_cmp, o_slc, o_win, g_ns, g_mg, p_sample[i], *tail_w)
        new_da_s.append(da_kv)
        new_ns_s.append(ns_kv)
        new_win_s.append(win_all[:, win_all.shape[1] - wb:])
    y_prompt = rms_norm(xp, g_final)
    y_sample = rms_norm(xs, g_final)
    return (y_prompt, y_sample, jnp.stack(new_da_p), jnp.stack(new_da_s), jnp.stack(new_ns_p),
            jnp.stack(new_ns_s), jnp.stack(new_win_p), jnp.stack(new_win_s))
```

```python
import functools
import math

import jax
import jax.numpy as jnp
from jax import lax
from jax.experimental import pallas as pl
from jax.experimental.pallas import tpu as pltpu

F32 = jnp.float32
BF16 = jnp.bfloat16
I32 = jnp.int32
HIGHEST = lax.Precision.HIGHEST

DA_HEADS, DA_KV, DA_REP, DA_DH = 8, 4, 2, 64
DA_DV = 2 * DA_DH
NSA_HEADS, NSA_KV, NSA_REP, NSA_DH = 16, 2, 8, 64
BLOCK = 64
BLOCK_SHIFT = 6
N_SELECT = 16
WINDOW = 512
FORCE_BONUS = 1000.0
ROPE_THETA = 500000.0
ROPE_HALF = 8
N_GROUPS, EXP_PER_GROUP = 8, 8
N_EXPERTS = N_GROUPS * EXP_PER_GROUP
EPS = 1e-6
NEG = -1e30
LOWEST = -3e38

LANES = 128
VMEM_LIMIT_BYTES = 56 * 1024 * 1024

C_DA_Q = DA_HEADS * 2 * DA_DH
C_DA_K = DA_KV * 2 * DA_DH
C_DA_V = DA_KV * DA_DV
C_NSA_Q = NSA_HEADS * NSA_DH
C_NSA_KV = 6 * NSA_KV * NSA_DH
C_NSA_G = 3 * NSA_HEADS
S1 = C_DA_Q
S2 = S1 + C_DA_K
S3 = S2 + C_DA_V
S4 = S3 + C_NSA_Q
S5 = S4 + C_NSA_KV
S6 = S5 + C_NSA_G

EXPERT_ROWS = 128
DMA_WINDOW = 32


def _cparams(sem):
    return pltpu.CompilerParams(dimension_semantics=sem, vmem_limit_bytes=VMEM_LIMIT_BYTES)


def _pick_tile(n, target, mult):
    best = None
    for t in range(mult, min(n, target) + 1, mult):
        if n % t == 0:
            best = t
    return n if best is None else best


def _round_up(x, m):
    return (x + m - 1) // m * m


def _nt(a, b):
    return lax.dot_general(a, b, (((1,), (1,)), ((), ())), preferred_element_type=F32)


def _dot(a, b, **kw):
    return jnp.dot(a, b, preferred_element_type=F32, **kw)


def _rms(x, g):
    ms = jnp.mean(x * x, axis=-1, keepdims=True)
    return x * lax.rsqrt(ms + EPS) * g


def _proj_rope_kernel(x_ref, g_ref, w_ref, cos_ref, sa_ref, sb_ref, rm_ref, cs_ref,
                      of_ref, ob_ref, h_scr, *, n_chunks):
    @pl.when(pl.program_id(1) == 0)
    def _():
        h_scr[...] = _rms(x_ref[...], g_ref[...]).astype(BF16)

    z = _dot(h_scr[...], w_ref[...])
    c, sa, sb = cos_ref[...], sa_ref[...], sb_ref[...]
    for k in range(n_chunks):
        sl = slice(k * LANES, (k + 1) * LANES)
        zc = z[:, sl]
        rot = zc * c + pltpu.roll(zc, LANES - ROPE_HALF, 1) * sa + pltpu.roll(zc, ROPE_HALF, 1) * sb
        out = jnp.where(rm_ref[:, sl] > 0.5, rot, zc) * cs_ref[:, sl]
        of_ref[:, sl] = out
        ob_ref[:, sl] = out.astype(BF16)


def _proj_rope(xa, g, w, cos, sa, sb, rmask, cscale):
    n, d = xa.shape
    ca = w.shape[1]
    tm = _pick_tile(n, 640, 16)
    tn = _pick_tile(ca, 1024, LANES)
    row = lambda i, j: (i, 0)
    col = lambda i, j: (0, j)
    return pl.pallas_call(
        functools.partial(_proj_rope_kernel, n_chunks=tn // LANES),
        out_shape=(jax.ShapeDtypeStruct((n, ca), F32), jax.ShapeDtypeStruct((n, ca), BF16)),
        grid=(n // tm, ca // tn),
        in_specs=[pl.BlockSpec((tm, d), row), pl.BlockSpec((1, d), lambda i, j: (0, 0)),
                  pl.BlockSpec((d, tn), col),
                  pl.BlockSpec((tm, LANES), row), pl.BlockSpec((tm, LANES), row),
                  pl.BlockSpec((tm, LANES), row),
                  pl.BlockSpec((1, tn), col), pl.BlockSpec((1, tn), col)],
        out_specs=(pl.BlockSpec((tm, tn), lambda i, j: (i, j)),
                   pl.BlockSpec((tm, tn), lambda i, j: (i, j))),
        scratch_shapes=[pltpu.VMEM((tm, d), BF16)],
        compiler_params=_cparams(("parallel", "arbitrary")),
        name="proj_rope",
    )(xa, g, w, cos, sa, sb, rmask, cscale)


def _gate_kernel(x_ref, g_ref, w_ref, o_ref, h_scr):
    @pl.when(pl.program_id(1) == 0)
    def _():
        h_scr[...] = _rms(x_ref[...], g_ref[...]).astype(BF16)

    o_ref[...] = jax.nn.sigmoid(_dot(h_scr[...], w_ref[...]))


def _gate_proj(xa, g, w):
    n, d = xa.shape
    ca = w.shape[1]
    tm = _pick_tile(n, 640, 16)
    tn = _pick_tile(ca, 1024, LANES)
    return pl.pallas_call(
        _gate_kernel,
        out_shape=jax.ShapeDtypeStruct((n, ca), F32),
        grid=(n // tm, ca // tn),
        in_specs=[pl.BlockSpec((tm, d), lambda i, j: (i, 0)), pl.BlockSpec((1, d), lambda i, j: (0, 0)),
                  pl.BlockSpec((d, tn), lambda i, j: (0, j))],
        out_specs=pl.BlockSpec((tm, tn), lambda i, j: (i, j)),
        scratch_shapes=[pltpu.VMEM((tm, d), BF16)],
        compiler_params=_cparams(("parallel", "arbitrary")),
        name="gate_proj",
    )(xa, g, w)


def _softmax_block(s, mask, v, m_old, l_old, acc_old):
    if mask is not None:
        s = jnp.where(mask, s, NEG)
    m_new = jnp.maximum(m_old, jnp.max(s, axis=-1, keepdims=True))
    p = jnp.exp(s - m_new)
    if mask is not None:
        p = jnp.where(mask, p, 0.0)
    alpha = jnp.exp(m_old - m_new)
    l_new = alpha * l_old + jnp.sum(p, axis=-1, keepdims=True)
    acc_new = alpha * acc_old + _dot(p.astype(v.dtype), v)
    return m_new, l_new, acc_new


def _flash_update(s, mask, v, m_ref, l_ref, acc_ref, idx):
    m, l, acc = _softmax_block(s, mask, v, m_ref[idx], l_ref[idx], acc_ref[idx])
    m_ref[idx] = m
    l_ref[idx] = l
    acc_ref[idx] = acc


def _flash_init(m_ref, l_ref, acc_ref):
    m_ref[...] = jnp.full(m_ref.shape, NEG, F32)
    l_ref[...] = jnp.zeros(l_ref.shape, F32)
    acc_ref[...] = jnp.zeros(acc_ref.shape, F32)


def _da_lambda(lam_ref, lam_init):
    a = jnp.sum(lam_ref[0:1, :] * lam_ref[1:2, :], axis=-1, keepdims=True)
    b = jnp.sum(lam_ref[2:3, :] * lam_ref[3:4, :], axis=-1, keepdims=True)
    return jnp.exp(a) - jnp.exp(b) + lam_init


def _da_finish(a0, a1, lam, sub, lam_init):
    o = a0 - lam * a1
    return _rms(o, sub) * (1.0 - lam_init)


def _da_flash_kernel(lam_ref, sub_ref, q_ref, k_ref, v_ref, o_ref, m_scr, l_scr, acc_scr, *, tq, lam_init):
    qi = pl.program_id(2)
    ki = pl.program_id(3)

    @pl.when(ki == 0)
    def _():
        _flash_init(m_scr, l_scr, acc_scr)

    def step(diag):
        v = v_ref[0, 0]
        mask = None
        if diag:
            row = lax.broadcasted_iota(I32, (DA_REP * tq, tq), 0)
            row = jnp.where(row >= tq, row - tq, row)
            col = lax.broadcasted_iota(I32, (DA_REP * tq, tq), 1)
            mask = col <= row
        for c in range(2):
            q = q_ref[0, 0, c].reshape(DA_REP * tq, DA_DH)
            _flash_update(_nt(q, k_ref[0, 0, c]), mask, v, m_scr, l_scr, acc_scr, c)

    @pl.when(ki < qi)
    def _():
        step(False)

    @pl.when(ki == qi)
    def _():
        step(True)
        lam = _da_lambda(lam_ref, lam_init)
        o = _da_finish(acc_scr[0] / l_scr[0], acc_scr[1] / l_scr[1], lam, sub_ref[...], lam_init)
        o_ref[0, 0] = o.reshape(DA_REP, tq, DA_DV).astype(o_ref.dtype)


def _da_flash(qd, kd, vd, lam_p, subln, lam_init):
    b, _, _, _, s, _ = qd.shape
    tq = _pick_tile(s, 256, 16)
    nq = s // tq
    kmap = lambda bi, g, qi, ki: (bi, g, 0, jnp.minimum(ki, qi), 0)
    return pl.pallas_call(
        functools.partial(_da_flash_kernel, tq=tq, lam_init=lam_init),
        out_shape=jax.ShapeDtypeStruct((b, DA_KV, DA_REP, s, DA_DV), BF16),
        grid=(b, DA_KV, nq, nq),
        in_specs=[pl.BlockSpec((4, DA_DH), lambda bi, g, qi, ki: (0, 0)),
                  pl.BlockSpec((1, DA_DV), lambda bi, g, qi, ki: (0, 0)),
                  pl.BlockSpec((1, 1, 2, DA_REP, tq, DA_DH), lambda bi, g, qi, ki: (bi, g, 0, 0, qi, 0)),
                  pl.BlockSpec((1, 1, 2, tq, DA_DH), kmap),
                  pl.BlockSpec((1, 1, tq, DA_DV), lambda bi, g, qi, ki: (bi, g, jnp.minimum(ki, qi), 0))],
        out_specs=pl.BlockSpec((1, 1, DA_REP, tq, DA_DV), lambda bi, g, qi, ki: (bi, g, 0, qi, 0)),
        scratch_shapes=[pltpu.VMEM((2, DA_REP * tq, 1), F32), pltpu.VMEM((2, DA_REP * tq, 1), F32),
                        pltpu.VMEM((2, DA_REP * tq, DA_DV), F32)],
        compiler_params=_cparams(("parallel", "parallel", "parallel", "arbitrary")),
        name="da_flash",
    )(lam_p, subln, qd, kd, vd)


def _da_decode_kernel(pt_ref, lam_ref, sub_ref, q_ref, c_ref, n_ref, o_ref, m_scr, l_scr, acc_scr,
                      *, npages, rq, lam_init):
    del pt_ref
    p = pl.program_id(1)
    ck = DA_KV * 2 * DA_DH

    @pl.when(p == 0)
    def _():
        _flash_init(m_scr, l_scr, acc_scr)

    q = q_ref[0]
    rows = q.shape[0]

    def step(kv, mask):
        k = kv[:, :ck].astype(BF16)
        v = kv[:, ck:].astype(BF16)
        _flash_update(_nt(q, k), mask, v, m_scr, l_scr, acc_scr, 0)

    @pl.when(p < npages)
    def _():
        step(c_ref[0], None)

    @pl.when(p == npages)
    def _():
        row = lax.broadcasted_iota(I32, (rows, n_ref.shape[1]), 0)
        col = lax.broadcasted_iota(I32, (rows, n_ref.shape[1]), 1)
        step(n_ref[0], col <= (row % rq) // DA_REP)
        lam = _da_lambda(lam_ref, lam_init)
        a = acc_scr[0] / l_scr[0]
        for g in range(DA_KV):
            cols = slice(g * DA_DV, (g + 1) * DA_DV)
            a0 = a[g * 2 * rq:g * 2 * rq + rq, cols]
            a1 = a[g * 2 * rq + rq:(g + 1) * 2 * rq, cols]
            o_ref[0, g] = _da_finish(a0, a1, lam, sub_ref[...], lam_init)


def _da_decode(page_table, qbd, cache, newkv, lam_p, subln, lam_init):
    nb, rows, ck = qbd.shape
    npages = page_table.shape[1]
    page, cw = cache.shape[1], cache.shape[2]
    rq = rows // (DA_KV * 2)
    grid_spec = pltpu.PrefetchScalarGridSpec(
        num_scalar_prefetch=1, grid=(nb, npages + 1),
        in_specs=[pl.BlockSpec((4, DA_DH), lambda b, p, pt: (0, 0)),
                  pl.BlockSpec((1, DA_DV), lambda b, p, pt: (0, 0)),
                  pl.BlockSpec((1, rows, ck), lambda b, p, pt: (b, 0, 0)),
                  pl.BlockSpec((1, page, cw), lambda b, p, pt: (pt[b, jnp.minimum(p, npages - 1)], 0, 0)),
                  pl.BlockSpec((1, page, cw), lambda b, p, pt: (b, 0, 0))],
        out_specs=pl.BlockSpec((1, DA_KV, rq, DA_DV), lambda b, p, pt: (b, 0, 0, 0)),
        scratch_shapes=[pltpu.VMEM((1, rows, 1), F32), pltpu.VMEM((1, rows, 1), F32),
                        pltpu.VMEM((1, rows, DA_KV * DA_DV), F32)])
    return pl.pallas_call(
        functools.partial(_da_decode_kernel, npages=npages, rq=rq, lam_init=lam_init),
        out_shape=jax.ShapeDtypeStruct((nb, DA_KV, rq, DA_DV), F32),
        grid_spec=grid_spec,
        compiler_params=_cparams(("parallel", "arbitrary")),
        name="da_decode",
    )(page_table, lam_p, subln, qbd, cache, newkv)


def _compress_kernel(k_ref, v_ref, wp_ref, wc_ref, ko_ref, vo_ref, *, apply_w):
    for c, (src, dst) in enumerate(((k_ref, ko_ref), (v_ref, vo_ref))):
        x = src[...]
        t = x.shape[0]
        y = jnp.sum(x.reshape(t // BLOCK, BLOCK, LANES) * wp_ref[c][None], axis=1)
        if apply_w:
            y = _dot(y, wc_ref[c], precision=HIGHEST)
        dst[...] = y


def _rowmat_kernel(k_ref, v_ref, wc_ref, ko_ref, vo_ref):
    ko_ref[...] = _dot(k_ref[...], wc_ref[0], precision=HIGHEST)
    vo_ref[...] = _dot(v_ref[...], wc_ref[1], precision=HIGHEST)


def _compress_prompt(zf, wp, wc, b, s):
    nc = s // BLOCK
    kc0 = S4 // LANES
    out = jax.ShapeDtypeStruct((b * nc, LANES), F32)
    return pl.pallas_call(
        functools.partial(_compress_kernel, apply_w=True),
        out_shape=(out, out),
        grid=(b,),
        in_specs=[pl.BlockSpec((s, LANES), lambda i: (i, kc0)), pl.BlockSpec((s, LANES), lambda i: (i, kc0 + 1)),
                  pl.BlockSpec((2, BLOCK, LANES), lambda i: (0, 0, 0)),
                  pl.BlockSpec((2, LANES, LANES), lambda i: (0, 0, 0))],
        out_specs=(pl.BlockSpec((nc, LANES), lambda i: (i, 0)), pl.BlockSpec((nc, LANES), lambda i: (i, 0))),
        compiler_params=_cparams(("parallel",)),
        name="nsa_compress",
    )(zf, zf, wp, wc)


def _compress_pages(page_table, cache, wp, wc):
    nb, npages = page_table.shape
    page = cache.shape[1]
    per = page // BLOCK
    out = jax.ShapeDtypeStruct((nb, npages, per, LANES), F32)
    grid_spec = pltpu.PrefetchScalarGridSpec(
        num_scalar_prefetch=1, grid=(nb, npages),
        in_specs=[pl.BlockSpec((None, page, LANES), lambda b, p, pt: (pt[b, p], 0, 0)),
                  pl.BlockSpec((None, page, LANES), lambda b, p, pt: (pt[b, p], 0, 1)),
                  pl.BlockSpec((2, BLOCK, LANES), lambda b, p, pt: (0, 0, 0)),
                  pl.BlockSpec((2, LANES, LANES), lambda b, p, pt: (0, 0, 0))],
        out_specs=(pl.BlockSpec((None, None, per, LANES), lambda b, p, pt: (b, p, 0, 0)),
                   pl.BlockSpec((None, None, per, LANES), lambda b, p, pt: (b, p, 0, 0))))

    def body(pt_ref, *refs):
        del pt_ref
        _compress_kernel(*refs, apply_w=False)

    yk, yv = pl.pallas_call(
        body, out_shape=(out, out), grid_spec=grid_spec,
        compiler_params=_cparams(("parallel", "arbitrary")), name="nsa_compress_pages",
    )(page_table, cache, cache, wp, wc)
    rows = nb * npages * per
    tr = _pick_tile(rows, 1024, 8)
    flat = jax.ShapeDtypeStruct((rows, LANES), F32)
    return pl.pallas_call(
        _rowmat_kernel, out_shape=(flat, flat), grid=(rows // tr,),
        in_specs=[pl.BlockSpec((tr, LANES), lambda i: (i, 0)), pl.BlockSpec((tr, LANES), lambda i: (i, 0)),
                  pl.BlockSpec((2, LANES, LANES), lambda i: (0, 0, 0))],
        out_specs=(pl.BlockSpec((tr, LANES), lambda i: (i, 0)), pl.BlockSpec((tr, LANES), lambda i: (i, 0))),
        compiler_params=_cparams(("parallel",)), name="nsa_compress_map",
    )(yk.reshape(rows, LANES), yv.reshape(rows, LANES), wc)


def _cmp_select_kernel(q_ref, kc_ref, vc_ref, o_ref, sel_ref, *, tq, pos0, nc, ns, width):
    qi = pl.program_id(1)
    pos = pos0 + qi * tq + lax.broadcasted_iota(I32, (tq, 1), 0)
    blk = lax.broadcasted_iota(I32, (1, width), 1)
    cmp_ok = (blk < nc) & ((blk + 1) * BLOCK - 1 <= pos)
    cur = lax.shift_right_logical(pos, BLOCK_SHIFT)
    forced = (blk == 0) | (blk == cur) | (blk == cur - 1)
    for g in range(NSA_KV):
        kc = kc_ref[0, g]
        vc = vc_ref[0, g]
        imp = jnp.zeros((tq, width), F32)
        for r in range(NSA_REP):
            h = g * NSA_REP + r
            s = jnp.where(cmp_ok, _nt(q_ref[0, h], kc), NEG)
            e = jnp.where(cmp_ok, jnp.exp(s - jnp.max(s, axis=-1, keepdims=True)), 0.0)
            den = jnp.sum(e, axis=-1, keepdims=True)
            p = e / jnp.where(den > 0.0, den, 1.0)
            imp = imp + p
            o_ref[0, h] = _dot(p.astype(vc.dtype), vc)
        score = jnp.where(forced, imp + FORCE_BONUS, imp)
        score = jnp.where(blk <= cur, score, -1.0)
        score = jnp.where(blk < ns, score, 0.5 * LOWEST)
        sel = jnp.zeros((tq, width), F32)
        for _ in range(N_SELECT):
            mx = jnp.max(score, axis=-1, keepdims=True)
            idx = jnp.min(jnp.where(score == mx, blk, width), axis=-1, keepdims=True)
            hit = blk == idx
            sel = jnp.where(hit, 1.0, sel)
            score = jnp.where(hit, LOWEST, score)
        sel_ref[0, g] = sel.astype(sel_ref.dtype)


def _cmp_select(qn, kc, vc, *, pos0, nc, ns):
    b, _, s, _ = qn.shape
    width = kc.shape[2]
    tq = _pick_tile(s, 256, 16)
    return pl.pallas_call(
        functools.partial(_cmp_select_kernel, tq=tq, pos0=pos0, nc=nc, ns=ns, width=width),
        out_shape=(jax.ShapeDtypeStruct((b, NSA_HEADS, s, NSA_DH), F32),
                   jax.ShapeDtypeStruct((b, NSA_KV, s, width), BF16)),
        grid=(b, s // tq),
        in_specs=[pl.BlockSpec((1, NSA_HEADS, tq, NSA_DH), lambda bi, qi: (bi, 0, qi, 0)),
                  pl.BlockSpec((1, NSA_KV, width, NSA_DH), lambda bi, qi: (bi, 0, 0, 0)),
                  pl.BlockSpec((1, NSA_KV, width, NSA_DH), lambda bi, qi: (bi, 0, 0, 0))],
        out_specs=(pl.BlockSpec((1, NSA_HEADS, tq, NSA_DH), lambda bi, qi: (bi, 0, qi, 0)),
                   pl.BlockSpec((1, NSA_KV, tq, width), lambda bi, qi: (bi, 0, qi, 0))),
        compiler_params=_cparams(("parallel", "parallel")),
        name="nsa_cmp_select",
    )(qn, kc, vc)


def _nsa_flash_kernel(*refs, mode, tq, nwin):
    if mode == "slc":
        q_ref, k_ref, v_ref, sel_ref, e_ref, o_ref, m_scr, l_scr, acc_scr = refs
    else:
        q_ref, k_ref, v_ref, o_ref, m_scr, l_scr, acc_scr = refs
    qi = pl.program_id(2)
    kk = pl.program_id(3)
    if mode == "slc":
        ki, active, last = kk, kk <= qi, kk == qi
    else:
        ki = qi - (nwin - 1) + kk
        active, last = ki >= 0, kk == nwin - 1

    @pl.when(kk == 0)
    def _():
        _flash_init(m_scr, l_scr, acc_scr)

    @pl.when(active)
    def _():
        rowpos = qi * tq + lax.broadcasted_iota(I32, (tq, tq), 0)
        colpos = ki * tq + lax.broadcasted_iota(I32, (tq, tq), 1)
        if mode == "slc":
            mask = (_dot(sel_ref[0, 0], e_ref[...]) > 0.5) & (colpos <= rowpos)
        else:
            d = rowpos - colpos
            mask = (d >= 0) & (d < WINDOW)
        q = q_ref[0].reshape(NSA_REP * tq, NSA_DH)
        v = v_ref[0, 0]
        mask3 = mask[None]
        s = jnp.where(mask3, _nt(q, k_ref[0, 0]).reshape(NSA_REP, tq, tq), NEG)
        m_old = m_scr[...]
        m_new = jnp.maximum(m_old, jnp.max(s, axis=-1, keepdims=True))
        p = jnp.where(mask3, jnp.exp(s - m_new), 0.0)
        alpha = jnp.exp(m_old - m_new)
        l_scr[...] = alpha * l_scr[...] + jnp.sum(p, axis=-1, keepdims=True)
        pv = _dot(p.reshape(NSA_REP * tq, tq).astype(v.dtype), v)
        acc_scr[...] = alpha * acc_scr[...] + pv.reshape(NSA_REP, tq, NSA_DH)
        m_scr[...] = m_new

    @pl.when(last)
    def _():
        o_ref[0] = acc_scr[...] / l_scr[...]


def _nsa_flash(qn, k, v, sel=None, emat=None):
    b, _, s, _ = qn.shape
    tq = _pick_tile(s, 256, 16)
    nq = s // tq
    if sel is not None:
        mode, nwin, nk = "slc", 0, nq
        kidx = lambda qi, kk: jnp.minimum(kk, qi)
    else:
        mode = "win"
        nwin = min(-(-WINDOW // tq) + 1, nq)
        nk = nwin
        kidx = lambda qi, kk: jnp.maximum(qi - (nwin - 1) + kk, 0)
    in_specs = [pl.BlockSpec((1, NSA_REP, tq, NSA_DH), lambda bi, g, qi, kk: (bi, g, qi, 0)),
                pl.BlockSpec((1, 1, tq, NSA_DH), lambda bi, g, qi, kk: (bi, g, kidx(qi, kk), 0)),
                pl.BlockSpec((1, 1, tq, NSA_DH), lambda bi, g, qi, kk: (bi, g, kidx(qi, kk), 0))]
    args = [qn, k, v]
    if sel is not None:
        width = sel.shape[3]
        in_specs += [pl.BlockSpec((1, 1, tq, width), lambda bi, g, qi, kk: (bi, g, qi, 0)),
                     pl.BlockSpec((width, tq), lambda bi, g, qi, kk: (0, kidx(qi, kk)))]
        args += [sel, emat]
    return pl.pallas_call(
        functools.partial(_nsa_flash_kernel, mode=mode, tq=tq, nwin=nwin),
        out_shape=jax.ShapeDtypeStruct((b, NSA_HEADS, s, NSA_DH), F32),
        grid=(b, NSA_KV, nq, nk),
        in_specs=in_specs,
        out_specs=pl.BlockSpec((1, NSA_REP, tq, NSA_DH), lambda bi, g, qi, kk: (bi, g, qi, 0)),
        scratch_shapes=[pltpu.VMEM((NSA_REP, tq, 1), F32), pltpu.VMEM((NSA_REP, tq, 1), F32),
                        pltpu.VMEM((NSA_REP, tq, NSA_DH), F32)],
        compiler_params=_cparams(("parallel", "parallel", "parallel", "arbitrary")),
        name="nsa_" + mode,
    )(*args)


def _row_token(rows, cols, t4):
    row = lax.broadcasted_iota(I32, (rows, cols), 0)
    return (row // NSA_REP) % t4


def _slc_decode_kernel(pt_ref, q_ref, sel_ref, e_ref, c_ref, n_ref, o_ref, m_scr, l_scr, acc_scr,
                       *, npages, past, t4):
    del pt_ref
    p = pl.program_id(1)

    @pl.when(p == 0)
    def _():
        _flash_init(m_scr, l_scr, acc_scr)

    q = q_ref[0]
    rows = q.shape[0]
    page = c_ref.shape[1]
    kw = NSA_KV * NSA_DH

    def step(kv):
        k = kv[:, :kw].astype(BF16)
        v = kv[:, kw:].astype(BF16)
        kpos = p * page + lax.broadcasted_iota(I32, (rows, page), 1)
        mask = (_dot(sel_ref[0], e_ref[...]) > 0.5) & (kpos <= past + _row_token(rows, page, t4))
        _flash_update(_nt(q, k), mask, v, m_scr, l_scr, acc_scr, 0)

    @pl.when(p < npages)
    def _():
        step(c_ref[0])

    @pl.when(p == npages)
    def _():
        step(n_ref[0])
        o_ref[0] = acc_scr[0] / l_scr[0]


def _slc_decode(page_table, qnb, selrows, emat, cache, newkv, *, past, t4):
    nb, rows, _ = qnb.shape
    npages = page_table.shape[1]
    page = cache.shape[1]
    width = selrows.shape[2]
    cw = 2 * NSA_KV * NSA_DH
    grid_spec = pltpu.PrefetchScalarGridSpec(
        num_scalar_prefetch=1, grid=(nb, npages + 1),
        in_specs=[pl.BlockSpec((1, rows, LANES), lambda b, p, pt: (b, 0, 0)),
                  pl.BlockSpec((1, rows, width), lambda b, p, pt: (b, 0, 0)),
                  pl.BlockSpec((width, page), lambda b, p, pt: (0, p)),
                  pl.BlockSpec((1, page, cw), lambda b, p, pt: (pt[b, jnp.minimum(p, npages - 1)], 0, 1)),
                  pl.BlockSpec((1, page, cw), lambda b, p, pt: (b, 0, 0))],
        out_specs=pl.BlockSpec((1, rows, LANES), lambda b, p, pt: (b, 0, 0)),
        scratch_shapes=[pltpu.VMEM((1, rows, 1), F32), pltpu.VMEM((1, rows, 1), F32),
                        pltpu.VMEM((1, rows, LANES), F32)])
    return pl.pallas_call(
        functools.partial(_slc_decode_kernel, npages=npages, past=past, t4=t4),
        out_shape=jax.ShapeDtypeStruct((nb, rows, LANES), F32),
        grid_spec=grid_spec,
        compiler_params=_cparams(("parallel", "arbitrary")),
        name="nsa_slc_decode",
    )(page_table, qnb, selrows, emat, cache, newkv)


def _win_decode_kernel(q_ref, c_ref, n_ref, o_ref, *, past, t4):
    q = q_ref[0]
    rows = q.shape[0]
    kw = NSA_KV * NSA_DH
    wb = c_ref.shape[1]
    m = jnp.full((rows, 1), NEG, F32)
    l = jnp.zeros((rows, 1), F32)
    acc = jnp.zeros((rows, LANES), F32)
    for ref, kstart in ((c_ref, past - wb), (n_ref, past)):
        kv = ref[0]
        nk = kv.shape[0]
        d = past + _row_token(rows, nk, t4) - (kstart + lax.broadcasted_iota(I32, (rows, nk), 1))
        mask = (d >= 0) & (d < WINDOW)
        m, l, acc = _softmax_block(_nt(q, kv[:, :kw].astype(BF16)), mask, kv[:, kw:].astype(BF16), m, l, acc)
    o_ref[0] = acc / l


def _win_decode(qnb, cwin, newkv, *, past, t4):
    nb, rows, _ = qnb.shape
    wb, cw = cwin.shape[1], cwin.shape[2]
    return pl.pallas_call(
        functools.partial(_win_decode_kernel, past=past, t4=t4),
        out_shape=jax.ShapeDtypeStruct((nb, rows, LANES), F32),
        grid=(nb,),
        in_specs=[pl.BlockSpec((1, rows, LANES), lambda b: (b, 0, 0)),
                  pl.BlockSpec((1, wb, cw), lambda b: (b, 0, 0)),
                  pl.BlockSpec((1, newkv.shape[1], cw), lambda b: (b, 0, 0))],
        out_specs=pl.BlockSpec((1, rows, LANES), lambda b: (b, 0, 0)),
        compiler_params=_cparams(("parallel",)),
        name="nsa_win_decode",
    )(qnb, cwin, newkv)


def _merge_kernel(oda_ref, oc_ref, os_ref, ow_ref, gn_ref, ga_ref, gb_ref, e_ref, wda_ref, wns_ref, o_ref):
    g = gn_ref[...]
    ghi = g.astype(BF16)
    glo = (g - ghi.astype(F32)).astype(BF16)
    ons = None
    for b, src in enumerate((oc_ref, os_ref, ow_ref)):
        ge = _dot(ghi, e_ref[b]) + _dot(glo, e_ref[b])
        term = ge * src[...]
        ons = term if ons is None else ons + term
    yda = _dot(oda_ref[...], wda_ref[...])
    yns = _dot(ons.astype(BF16), wns_ref[...])
    o_ref[...] = (ga_ref[...] * yda + gb_ref[...] * yns).astype(o_ref.dtype)


def _merge(oda, ocmp, oslc, owin, gns, gmg, emat, wda, wns):
    n, hd = oda.shape
    d = wda.shape[1]
    tm = _pick_tile(n, 320, 16)
    row = lambda i: (i, 0)
    const2 = lambda i: (0, 0)
    return pl.pallas_call(
        _merge_kernel,
        out_shape=jax.ShapeDtypeStruct((n, d), BF16),
        grid=(n // tm,),
        in_specs=[pl.BlockSpec((tm, hd), row), pl.BlockSpec((tm, hd), row), pl.BlockSpec((tm, hd), row),
                  pl.BlockSpec((tm, hd), row), pl.BlockSpec((tm, LANES), row),
                  pl.BlockSpec((tm, d), lambda i: (i, 0)), pl.BlockSpec((tm, d), lambda i: (i, 1)),
                  pl.BlockSpec((3, LANES, hd), lambda i: (0, 0, 0)),
                  pl.BlockSpec((hd, d), const2), pl.BlockSpec((hd, d), const2)],
        out_specs=pl.BlockSpec((tm, d), row),
        compiler_params=_cparams(("parallel",)),
        name="mixer_merge",
    )(oda, ocmp, oslc, owin, gns, gmg, gmg, emat, wda, wns)


def _lane_pick(x, lane, k):
    return jnp.sum(jnp.where(lane == k, x, 0.0), axis=-1, keepdims=True)


def _route(logits):
    lane = lax.broadcasted_iota(I32, logits.shape, 1)
    isg = lane < N_GROUPS
    gmax = jnp.max(jnp.where(isg, logits, LOWEST), axis=-1, keepdims=True)
    gsel = jnp.min(jnp.where(isg & (logits == gmax), lane, LANES), axis=-1, keepdims=True)
    pgrp = 1.0 / jnp.sum(jnp.where(isg, jnp.exp(logits - gmax), 0.0), axis=-1, keepdims=True)
    lo = N_GROUPS + gsel * EXP_PER_GROUP
    ing = (lane >= lo) & (lane < lo + EXP_PER_GROUP)
    v1 = jnp.max(jnp.where(ing, logits, LOWEST), axis=-1, keepdims=True)
    i1 = jnp.min(jnp.where(ing & (logits == v1), lane, LANES), axis=-1, keepdims=True)
    ing2 = ing & (lane != i1)
    v2 = jnp.max(jnp.where(ing2, logits, LOWEST), axis=-1, keepdims=True)
    i2 = jnp.min(jnp.where(ing2 & (logits == v2), lane, LANES), axis=-1, keepdims=True)
    t = jnp.exp(v2 - v1)
    w1 = pgrp / (1.0 + t)
    w2 = pgrp * t / (1.0 + t)
    e1 = (i1 - N_GROUPS).astype(F32)
    e2 = (i2 - N_GROUPS).astype(F32)
    return jnp.where(lane == 0, e1, jnp.where(lane == 1, e2, jnp.where(lane == 2, w1, jnp.where(lane == 3, w2, 0.0))))


def _store_rows(ref, y, rows, nchunk):
    for s in range(nchunk):
        ref[pl.ds(s, rows, stride=nchunk), :] = y[:, s * LANES:(s + 1) * LANES]


def _load_rows(ref, start, rows, stride, nchunk):
    return jnp.concatenate([ref[pl.ds(start + s, rows, stride=stride), :] for s in range(nchunk)], axis=1)


def _wo_route_kernel(m_ref, x_ref, wo_ref, gf_ref, wr_ref, br_ref, x1_ref, h_ref, rt_ref, *, nchunk):
    x1 = x_ref[...] + _dot(m_ref[...], wo_ref[...])
    x1_ref[...] = x1
    h = _rms(x1, gf_ref[...])
    _store_rows(h_ref, h, h.shape[0], nchunk)
    rt_ref[...] = _route(_dot(h, wr_ref[...], precision=HIGHEST) + br_ref[...])


def _wo_route(mrg, xa, wo, gffn, wr, br):
    n, d = xa.shape
    tm = _pick_tile(n, 320, 16)
    nchunk = d // LANES
    row = lambda i: (i, 0)
    const2 = lambda i: (0, 0)
    return pl.pallas_call(
        functools.partial(_wo_route_kernel, nchunk=nchunk),
        out_shape=(jax.ShapeDtypeStruct((n, d), F32), jax.ShapeDtypeStruct((n * nchunk, LANES), F32),
                   jax.ShapeDtypeStruct((n, LANES), F32)),
        grid=(n // tm,),
        in_specs=[pl.BlockSpec((tm, d), row), pl.BlockSpec((tm, d), row), pl.BlockSpec((d, d), const2),
                  pl.BlockSpec((1, d), const2), pl.BlockSpec((d, LANES), const2), pl.BlockSpec((1, LANES), const2)],
        out_specs=(pl.BlockSpec((tm, d), row), pl.BlockSpec((tm * nchunk, LANES), row),
                   pl.BlockSpec((tm, LANES), row)),
        compiler_params=_cparams(("parallel",)),
        name="wo_route",
    )(mrg, xa, wo, gffn, wr, br)


def _onehots(rt):
    lane = lax.broadcasted_iota(I32, rt.shape, 1)
    lane_f = lane.astype(F32)
    oh0 = jnp.where(lane_f == _lane_pick(rt, lane, 0), 1.0, 0.0)
    oh1 = jnp.where(lane_f == _lane_pick(rt, lane, 1), 1.0, 0.0)
    return lane, oh0, oh1


def _rank_kernel(rt_ref, tri_ref, rk_ref, sz_ref, carry):
    @pl.when(pl.program_id(0) == 0)
    def _():
        carry[...] = jnp.zeros(carry.shape, F32)

    lane, oh0, oh1 = _onehots(rt_ref[...])
    oh = oh0 + oh1
    before = _dot(tri_ref[...], oh.astype(BF16)) + carry[0:1, :]
    r0 = jnp.sum(before * oh0, axis=-1, keepdims=True)
    r1 = jnp.sum(before * oh1, axis=-1, keepdims=True)
    rk_ref[...] = jnp.where(lane == 0, r0, jnp.where(lane == 1, r1, 0.0))
    carry[...] = carry[...] + jnp.sum(oh, axis=0, keepdims=True)
    sz_ref[...] = carry[...]


def _expert_ranks(route, tri):
    n = route.shape[0]
    tm = tri.shape[0]
    return pl.pallas_call(
        _rank_kernel,
        out_shape=(jax.ShapeDtypeStruct((n, LANES), F32), jax.ShapeDtypeStruct((8, LANES), F32)),
        grid=(n // tm,),
        in_specs=[pl.BlockSpec((tm, LANES), lambda i: (i, 0)), pl.BlockSpec((tm, tm), lambda i: (0, 0))],
        out_specs=(pl.BlockSpec((tm, LANES), lambda i: (i, 0)), pl.BlockSpec((8, LANES), lambda i: (0, 0))),
        scratch_shapes=[pltpu.VMEM((8, LANES), F32)],
        compiler_params=_cparams(("arbitrary",)),
        name="moe_rank",
    )(route, tri)


def _dest_kernel(rt_ref, rk_ref, sz_ref, dest_ref, be_ref, nu_ref, *, shift):
    lane8 = lax.broadcasted_iota(I32, (8, LANES), 1)
    sizes = sz_ref[...].astype(I32)
    blk = 1 << shift
    padded = jnp.where(lane8 < N_EXPERTS, lax.shift_right_logical(sizes + (blk - 1), shift) * blk, 0)
    ends = padded
    d = 1
    while d < LANES:
        ends = ends + jnp.where(lane8 >= d, pltpu.roll(ends, d, 1), 0)
        d *= 2
    pstart = (ends - padded)[0:1, :].astype(F32)
    rt = rt_ref[...]
    lane, oh0, oh1 = _onehots(rt)
    rk = rk_ref[...]
    d0 = jnp.sum(oh0 * pstart, axis=-1, keepdims=True) + _lane_pick(rk, lane, 0)
    d1 = jnp.sum(oh1 * pstart, axis=-1, keepdims=True) + _lane_pick(rk, lane, 1)
    dest_ref[...] = jnp.where(lane == 0, d0, jnp.where(lane == 1, d1, 0.0)).astype(I32)
    nbt = be_ref.shape[0]
    lane_b = lax.broadcasted_iota(I32, (nbt, LANES), 1)
    first_row = lax.broadcasted_iota(I32, (nbt, LANES), 0) * blk
    cnt = jnp.sum(jnp.where((lane_b < N_EXPERTS) & (ends[0:1, :] <= first_row), 1, 0), axis=-1, keepdims=True)
    be_ref[...] = jnp.broadcast_to(jnp.minimum(cnt, N_EXPERTS - 1), (nbt, LANES))
    total = jnp.sum(jnp.where(lane8 == N_EXPERTS - 1, ends, 0), axis=-1, keepdims=True)
    nu_ref[...] = jnp.broadcast_to(lax.shift_right_logical(total, shift), (8, LANES))


def _destinations(route, rank, sizes, nbt, shift):
    n = route.shape[0]
    tm = _pick_tile(n, 640, 8)
    nbt_p = _round_up(nbt, 8)
    return pl.pallas_call(
        functools.partial(_dest_kernel, shift=shift),
        out_shape=(jax.ShapeDtypeStruct((n, LANES), I32), jax.ShapeDtypeStruct((nbt_p, LANES), I32),
                   jax.ShapeDtypeStruct((8, LANES), I32)),
        grid=(n // tm,),
        in_specs=[pl.BlockSpec((tm, LANES), lambda i: (i, 0)), pl.BlockSpec((tm, LANES), lambda i: (i, 0)),
                  pl.BlockSpec((8, LANES), lambda i: (0, 0))],
        out_specs=(pl.BlockSpec((tm, LANES), lambda i: (i, 0)), pl.BlockSpec((nbt_p, LANES), lambda i: (0, 0)),
                   pl.BlockSpec((8, LANES), lambda i: (0, 0))),
        compiler_params=_cparams(("arbitrary",)),
        name="moe_dest",
    )(route, rank, sizes)


def _dispatch_kernel(dest_ref, h_ref, z_ref, o_ref, sem, *, n_assign):
    del z_ref

    def copy(a):
        return pltpu.make_async_copy(h_ref.at[lax.shift_right_logical(a, 1)], o_ref.at[dest_ref[a]], sem)

    def issue(a, carry):
        @pl.when(a >= DMA_WINDOW)
        def _():
            copy(a - DMA_WINDOW).wait()

        copy(a).start()
        return carry

    lax.fori_loop(0, n_assign, issue, 0)

    def drain(a, carry):
        copy(a).wait()
        return carry

    lax.fori_loop(max(n_assign - DMA_WINDOW, 0), n_assign, drain, 0)


def _dispatch(dest, h3, zeros3):
    n_assign = dest.shape[0]
    grid_spec = pltpu.PrefetchScalarGridSpec(
        num_scalar_prefetch=1, grid=(1,),
        in_specs=[pl.BlockSpec(memory_space=pl.ANY), pl.BlockSpec(memory_space=pl.ANY)],
        out_specs=pl.BlockSpec(memory_space=pl.ANY),
        scratch_shapes=[pltpu.SemaphoreType.DMA(())])
    return pl.pallas_call(
        functools.partial(_dispatch_kernel, n_assign=n_assign),
        out_shape=jax.ShapeDtypeStruct(zeros3.shape, zeros3.dtype),
        grid_spec=grid_spec,
        input_output_aliases={2: 0},
        compiler_params=_cparams(("arbitrary",)),
        name="moe_dispatch",
    )(dest, h3, zeros3)


def _expert_kernel(be_ref, nu_ref, x_ref, wg_ref, wu_ref, wd_ref, o_ref, wgb, wub, wdb, *, rows, nchunk):
    j = pl.program_id(0)
    active = j < nu_ref[0]
    changed = (j == 0) | (be_ref[j] != be_ref[jnp.maximum(j - 1, 0)])

    @pl.when(active & changed)
    def _():
        wgb[...] = wg_ref[0].astype(BF16)
        wub[...] = wu_ref[0].astype(BF16)
        wdb[...] = wd_ref[0].astype(BF16)

    @pl.when(active)
    def _():
        x = _load_rows(x_ref, 0, rows, nchunk, nchunk).astype(BF16)
        a = _dot(x, wgb[...])
        u = _dot(x, wub[...])
        y = _dot((jax.nn.silu(a) * u).astype(BF16), wdb[...])
        _store_rows(o_ref, y, rows, nchunk)

    @pl.when(jnp.logical_not(active))
    def _():
        o_ref[...] = jnp.zeros(o_ref.shape, F32)


def _experts(blk_e, n_used, xs, wg, wu, wd, rows):
    d, de = wg.shape[1], wg.shape[2]
    nchunk = d // LANES
    nbt = xs.shape[0] // (rows * nchunk)
    grid_spec = pltpu.PrefetchScalarGridSpec(
        num_scalar_prefetch=2, grid=(nbt,),
        in_specs=[pl.BlockSpec((rows * nchunk, LANES), lambda j, be, nu: (j, 0)),
                  pl.BlockSpec((1, d, de), lambda j, be, nu: (be[j], 0, 0)),
                  pl.BlockSpec((1, d, de), lambda j, be, nu: (be[j], 0, 0)),
                  pl.BlockSpec((1, de, d), lambda j, be, nu: (be[j], 0, 0))],
        out_specs=pl.BlockSpec((rows * nchunk, LANES), lambda j, be, nu: (j, 0)),
        scratch_shapes=[pltpu.VMEM((d, de), BF16), pltpu.VMEM((d, de), BF16), pltpu.VMEM((de, d), BF16)])
    return pl.pallas_call(
        functools.partial(_expert_kernel, rows=rows, nchunk=nchunk),
        out_shape=jax.ShapeDtypeStruct(xs.shape, F32),
        grid_spec=grid_spec,
        compiler_params=_cparams(("arbitrary",)),
        name="moe_experts",
    )(blk_e, n_used, xs, wg, wu, wd)


def _combine_kernel(dest_ref, ys_ref, rt_ref, x1_ref, p_ref, gp_ref, wpg_ref, wpl_ref, gfin_ref, o_ref,
                    gbuf, sem, *, tm, nchunk, final):
    base = pl.program_id(0) * (2 * tm)

    def copy(j):
        return pltpu.make_async_copy(ys_ref.at[dest_ref[base + j]], gbuf.at[pl.ds(j * nchunk, nchunk)], sem)

    def issue(j, carry):
        copy(j).start()
        return carry

    def drain(j, carry):
        copy(j).wait()
        return carry

    lax.fori_loop(0, 2 * tm, issue, 0)
    lax.fori_loop(0, 2 * tm, drain, 0)
    rt = rt_ref[...]
    lane = lax.broadcasted_iota(I32, rt.shape, 1)
    y0 = _load_rows(gbuf, 0, tm, 2 * nchunk, nchunk)
    y1 = _load_rows(gbuf, nchunk, tm, 2 * nchunk, nchunk)
    x2 = x1_ref[...] + (_lane_pick(rt, lane, 2) * y0 + _lane_pick(rt, lane, 3) * y1)
    gate = jax.nn.sigmoid(_dot(_rms(x2, gp_ref[...]).astype(BF16), wpg_ref[...]))
    x3 = x2 + gate * _dot(p_ref[...].astype(BF16), wpl_ref[...])
    o_ref[...] = _rms(x3, gfin_ref[...]) if final else x3


def _combine(dest, ys3, route, x1, pa, gple, wpg, wpl, gfin, final):
    n, d = x1.shape
    nchunk = d // LANES
    tm = _pick_tile(n, 320, 16)
    dp = pa.shape[1]
    row = lambda i, dst: (i, 0)
    const2 = lambda i, dst: (0, 0)
    grid_spec = pltpu.PrefetchScalarGridSpec(
        num_scalar_prefetch=1, grid=(n // tm,),
        in_specs=[pl.BlockSpec(memory_space=pl.ANY),
                  pl.BlockSpec((tm, LANES), row), pl.BlockSpec((tm, d), row), pl.BlockSpec((tm, dp), row),
                  pl.BlockSpec((1, d), const2), pl.BlockSpec((d, d), const2), pl.BlockSpec((dp, d), const2),
                  pl.BlockSpec((1, d), const2)],
        out_specs=pl.BlockSpec((tm, d), row),
        scratch_shapes=[pltpu.VMEM((2 * tm * nchunk, LANES), F32), pltpu.SemaphoreType.DMA(())])
    return pl.pallas_call(
        functools.partial(_combine_kernel, tm=tm, nchunk=nchunk, final=final),
        out_shape=jax.ShapeDtypeStruct((n, d), F32),
        grid_spec=grid_spec,
        compiler_params=_cparams(("arbitrary",)),
        name="moe_combine_tail",
    )(dest, ys3, route, x1, pa, gple, wpg, wpl, gfin)


def _rope_tables(pos):
    inv = ROPE_THETA ** (-jnp.arange(ROPE_HALF, dtype=F32) / ROPE_HALF)
    ang = pos.astype(F32)[:, None] * inv[None, :]
    cos, sin = jnp.cos(ang), jnp.sin(ang)
    n = pos.shape[0]
    rest = DA_DH - 2 * ROPE_HALF
    z = jnp.zeros((n, ROPE_HALF), F32)
    zr = jnp.zeros((n, rest), F32)
    c64 = jnp.concatenate([cos, cos, jnp.ones((n, rest), F32)], axis=1)
    a64 = jnp.concatenate([-sin, z, zr], axis=1)
    b64 = jnp.concatenate([z, sin, zr], axis=1)
    rep = LANES // DA_DH
    return jnp.tile(c64, (1, rep)), jnp.tile(a64, (1, rep)), jnp.tile(b64, (1, rep))


def _column_tables():
    ones = lambda n: jnp.ones((n,), F32)
    zeros = lambda n: jnp.zeros((n,), F32)
    gw = NSA_KV * NSA_DH
    rmask = jnp.concatenate([ones(C_DA_Q), ones(C_DA_K), zeros(C_DA_V), ones(C_NSA_Q)]
                            + [ones(gw), zeros(gw)] * 3)
    cscale = jnp.concatenate([ones(C_DA_Q) * DA_DH ** -0.5, ones(C_DA_K), ones(C_DA_V),
                              ones(C_NSA_Q) * NSA_DH ** -0.5, ones(C_NSA_KV)])
    return rmask[None, :], cscale[None, :]


def _block_expand(width, nkeys):
    blk = jnp.arange(width, dtype=I32)[:, None]
    key = jnp.arange(nkeys, dtype=I32)[None, :]
    return (key // BLOCK == blk).astype(BF16)


def _pad_rows(x, axis, size):
    pad = [(0, 0)] * x.ndim
    pad[axis] = (0, size - x.shape[axis])
    return jnp.pad(x, pad)


def _cmp_layout(kc, b, nc, width):
    x = kc.reshape(b, nc, NSA_KV, NSA_DH).transpose(0, 2, 1, 3)
    return _pad_rows(x, 2, width).astype(BF16)


def kernel(x_prompt, x_sample, cache_da_kv, cache_nsa_kv, cache_win_kv, page_table, p_prompt, p_sample,
           g_mix, w_in, da_lambda, da_subln, nsa_cpos, nsa_cmp, w_da_up, w_nsa_up, w_o, g_ffn,
           w_rg, b_rg, w_re, b_re, w_e_gate, w_e_up, w_e_down, g_ple, w_ple_gate, w_ple, g_final):
    b, s, d = x_prompt.shape
    nb, t4, _ = x_sample.shape
    depth = g_mix.shape[0]
    npr, nsm = b * s, nb * t4
    n = npr + nsm
    page = cache_da_kv.shape[2]
    npages = page_table.shape[1]
    past = npages * page
    wb = cache_win_kv.shape[2]
    nchunk = d // LANES
    rq = t4 * DA_REP
    assert rq % 8 == 0 and page % BLOCK == 0 and s % BLOCK == 0 and t4 <= page and d % LANES == 0

    xa = jnp.concatenate([x_prompt.reshape(npr, d), x_sample.reshape(nsm, d)], axis=0)
    pos = jnp.concatenate([jnp.tile(jnp.arange(s, dtype=I32), b), jnp.tile(past + jnp.arange(t4, dtype=I32), nb)])
    cos, sa, sb = _rope_tables(pos)
    rmask, cscale = _column_tables()

    nc_p, ns_p = s // BLOCK, max(-(-s // BLOCK), N_SELECT)
    tot_s = past + t4
    nc_s, ns_s = tot_s // BLOCK, max(-(-tot_s // BLOCK), N_SELECT)
    wid_p, wid_s = _round_up(ns_p, LANES), _round_up(ns_s, LANES)
    emat_p = _block_expand(wid_p, s)
    emat_s = _block_expand(wid_s, (npages + 1) * page)
    tq_cs = 16

    col = jnp.arange(LANES, dtype=I32)[:, None]
    lane = jnp.arange(NSA_HEADS * NSA_DH, dtype=I32)[None, :]
    gate_expand = jnp.stack([(col == br * NSA_HEADS + lane // NSA_DH) for br in range(3)]).astype(BF16)

    n_assign = 2 * n
    shift = int(math.log2(EXPERT_ROWS))
    nbt = -(-n_assign // EXPERT_ROWS) + N_EXPERTS
    tm_rank = _pick_tile(n, 640, 16)
    tri = jnp.tril(jnp.ones((tm_rank, tm_rank), BF16), -1)

    outs = {k: [] for k in ("da_p", "da_s", "ns_p", "ns_s", "win_p", "win_s")}
    for i in range(depth):
        lam_init = 0.8 - 0.6 * math.exp(-0.3 * i)
        wi = w_in[i]
        g1 = g_mix[i][None, :]
        zf, zb = _proj_rope(xa, g1, wi[:, :S5].astype(BF16), cos, sa, sb, rmask, cscale)
        wg_cols = wi[:, S5:S6].reshape(d, NSA_HEADS, 3).transpose(0, 2, 1).reshape(d, C_NSA_G)
        gns = _gate_proj(xa, g1, _pad_rows(wg_cols, 1, LANES).astype(BF16))
        gmg = _gate_proj(xa, g1, wi[:, S6:].astype(BF16))
        lam_p = da_lambda[i].astype(F32)
        subln = da_subln[i][None, :].astype(F32)
        wp = jnp.tile(nsa_cpos[i].astype(F32), (1, 1, NSA_KV))
        wcm = nsa_cmp[i].astype(F32)
        zc = jnp.zeros_like(wcm)
        wc = jnp.concatenate([jnp.concatenate([wcm, zc], axis=2), jnp.concatenate([zc, wcm], axis=2)], axis=1)

        zp = zb[:npr].reshape(b, s, S5)
        qd = zp[..., :S1].reshape(b, s, DA_KV, DA_REP, 2, DA_DH).transpose(0, 2, 4, 3, 1, 5)
        kd = zp[..., S1:S2].reshape(b, s, DA_KV, 2, DA_DH).transpose(0, 2, 3, 1, 4)
        vd = zp[..., S2:S3].reshape(b, s, DA_KV, DA_DV).transpose(0, 2, 1, 3)
        oda_p = _da_flash(qd, kd, vd, lam_p, subln, lam_init)
        oda_p = oda_p.transpose(0, 3, 1, 2, 4).reshape(npr, DA_HEADS * DA_DV)
        qn = zp[..., S3:S4].reshape(b, s, NSA_HEADS, NSA_DH).transpose(0, 2, 1, 3)
        kvn = zp[..., S4:S5].reshape(b, s, 6, NSA_KV, NSA_DH).transpose(0, 2, 3, 1, 4)
        kc, vc = _compress_prompt(zf, wp, wc, b, s)
        ocmp_p, sel_p = _cmp_select(qn, _cmp_layout(kc, b, nc_p, wid_p), _cmp_layout(vc, b, nc_p, wid_p),
                                    pos0=0, nc=nc_p, ns=ns_p)
        oslc_p = _nsa_flash(qn, kvn[:, 2], kvn[:, 3], sel_p, emat_p)
        owin_p = _nsa_flash(qn, kvn[:, 4], kvn[:, 5])
        tok_major = lambda o: o.transpose(0, 2, 1, 3).reshape(npr, NSA_HEADS * NSA_DH)
        ocmp_p, oslc_p, owin_p = tok_major(ocmp_p), tok_major(oslc_p), tok_major(owin_p)

        zs = zb[npr:].reshape(nb, t4, S5)
        zfs = zf[npr:].reshape(nb, t4, S5)
        q6 = zs[..., :S1].reshape(nb, t4, DA_KV, DA_REP, 2, DA_DH).transpose(0, 2, 4, 1, 3, 5)
        eye_g = jnp.eye(DA_KV, dtype=BF16)
        eye_c = jnp.eye(2, dtype=BF16)
        qbd = (q6[:, :, :, :, :, None, None, :] * eye_g[None, :, None, None, None, :, None, None]
               * eye_c[None, None, :, None, None, None, :, None]).reshape(nb, DA_KV * 2 * rq, DA_KV * 2 * DA_DH)
        cda = cache_da_kv[i].reshape(cache_da_kv.shape[1], page, 2 * DA_KV * DA_DV)
        new_da = _pad_rows(zfs[..., S1:S3], 1, page)
        oda_s = _da_decode(page_table, qbd, cda, new_da, lam_p, subln, lam_init)
        oda_s = oda_s.reshape(nb, DA_KV, t4, DA_REP, DA_DV).transpose(0, 2, 1, 3, 4).reshape(nsm, DA_HEADS * DA_DV)

        qns = zs[..., S3:S4].reshape(nb, t4, NSA_KV, NSA_REP, NSA_DH)
        eye_n = jnp.eye(NSA_KV, dtype=BF16)
        qnb = (qns.transpose(0, 2, 1, 3, 4)[:, :, :, :, None, :] * eye_n[None, :, None, None, :, None]
               ).reshape(nb, NSA_KV * t4 * NSA_REP, LANES)
        cns = cache_nsa_kv[i].reshape(cache_nsa_kv.shape[1], page, 4 * NSA_KV * NSA_DH)
        kc_s, vc_s = _compress_pages(page_table, cns, wp, wc)
        q_cs = _pad_rows(zs[..., S3:S4].reshape(nb, t4, NSA_HEADS, NSA_DH).transpose(0, 2, 1, 3), 2, tq_cs)
        ocmp_s, sel_s = _cmp_select(q_cs, _cmp_layout(kc_s, nb, nc_s, wid_s), _cmp_layout(vc_s, nb, nc_s, wid_s),
                                    pos0=past, nc=nc_s, ns=ns_s)
        ocmp_s = ocmp_s[:, :, :t4].transpose(0, 2, 1, 3).reshape(nsm, NSA_HEADS * NSA_DH)
        selrows = jnp.broadcast_to(sel_s[:, :, :t4, None, :], (nb, NSA_KV, t4, NSA_REP, wid_s)
                                   ).reshape(nb, NSA_KV * t4 * NSA_REP, wid_s)
        kw = NSA_KV * NSA_DH
        new_slc = _pad_rows(zfs[..., S4 + 2 * kw:S4 + 4 * kw], 1, page)
        oslc_s = _slc_decode(page_table, qnb, selrows, emat_s, cns, new_slc, past=past, t4=t4)
        new_win = _pad_rows(zfs[..., S4 + 4 * kw:S5], 1, page)
        cwin = cache_win_kv[i].reshape(nb, wb, 2 * kw)
        owin_s = _win_decode(qnb, cwin, new_win, past=past, t4=t4)

        def halves(o):
            o = o.reshape(nb, NSA_KV, t4, NSA_REP, NSA_KV, NSA_DH)
            o = jnp.stack([o[:, g, :, :, g] for g in range(NSA_KV)], axis=2)
            return o.reshape(nsm, NSA_HEADS * NSA_DH)

        oslc_s, owin_s = halves(oslc_s), halves(owin_s)

        oda = jnp.concatenate([oda_p, oda_s.astype(BF16)], axis=0)
        ocmp = jnp.concatenate([ocmp_p, ocmp_s], axis=0)
        oslc = jnp.concatenate([oslc_p, oslc_s], axis=0)
        owin = jnp.concatenate([owin_p, owin_s], axis=0)
        mrg = _merge(oda, ocmp, oslc, owin, gns, gmg, gate_expand,
                     w_da_up[i].astype(BF16), w_nsa_up[i].astype(BF16))
        wr = _pad_rows(jnp.concatenate([w_rg[i], w_re[i]], axis=1).astype(F32), 1, LANES)
        br = _pad_rows(jnp.concatenate([b_rg[i], b_re[i]]).astype(F32)[None, :], 1, LANES)
        x1, hrows, route = _wo_route(mrg, xa, w_o[i].astype(BF16), g_ffn[i][None, :], wr, br)
        rank, sizes = _expert_ranks(route, tri)
        dest, be, nu = _destinations(route, rank, sizes, nbt, shift)
        dest_flat = dest[:, :2].reshape(n_assign)
        zeros3 = jnp.zeros((nbt * EXPERT_ROWS, nchunk, LANES), F32)
        xs = _dispatch(dest_flat, hrows.reshape(n, nchunk, LANES), zeros3)
        ys = _experts(be[:nbt, 0], nu[0, :1], xs.reshape(nbt * EXPERT_ROWS * nchunk, LANES),
                      w_e_gate[i], w_e_up[i], w_e_down[i], EXPERT_ROWS)
        pa = jnp.concatenate([p_prompt[i].reshape(npr, -1), p_sample[i].reshape(nsm, -1)], axis=0)
        xa = _combine(dest_flat, ys.reshape(nbt * EXPERT_ROWS, nchunk, LANES), route, x1, pa,
                      g_ple[i][None, :], w_ple_gate[i].astype(BF16), w_ple[i].astype(BF16),
                      g_final[None, :], i == depth - 1)

        zfp = zf[:npr].reshape(b, s, S5)
        outs["da_p"].append(zfp[..., S1:S3].reshape(b, s, 2, DA_KV, DA_DV))
        outs["da_s"].append(zfs[..., S1:S3].reshape(nb, t4, 2, DA_KV, DA_DV))
        outs["ns_p"].append(zfp[..., S4:S4 + 4 * kw].reshape(b, s, 4, NSA_KV, NSA_DH))
        outs["ns_s"].append(zfs[..., S4:S4 + 4 * kw].reshape(nb, t4, 4, NSA_KV, NSA_DH))
        win_p = zfp[..., S4 + 4 * kw:S5].reshape(b, s, 2, NSA_KV, NSA_DH)
        outs["win_p"].append(win_p[:, s - min(WINDOW, s):])
        win_s = jnp.concatenate([cache_win_kv[i], zfs[..., S4 + 4 * kw:S5].reshape(nb, t4, 2, NSA_KV, NSA_DH)], axis=1)
        outs["win_s"].append(win_s[:, win_s.shape[1] - wb:])

    y_prompt = xa[:npr].reshape(b, s, d)
    y_sample = xa[npr:].reshape(nb, t4, d)
    return (y_prompt, y_sample, jnp.stack(outs["da_p"]), jnp.stack(outs["da_s"]), jnp.stack(outs["ns_p"]),
            jnp.stack(outs["ns_s"]), jnp.stack(outs["win_p"]), jnp.stack(outs["win_s"]))
```

```python
import functools
import math

import jax
import jax.numpy as jnp
from jax import lax
from jax.experimental import pallas as pl
from jax.experimental.pallas import tpu as pltpu

F32 = jnp.float32
BF16 = jnp.bfloat16
I32 = jnp.int32
HIGHEST = lax.Precision.HIGHEST

DA_HEADS, DA_KV, DA_REP, DA_DH = 8, 4, 2, 64
DA_DV = 2 * DA_DH
NSA_HEADS, NSA_KV, NSA_REP, NSA_DH = 16, 2, 8, 64
BLOCK = 64
BLOCK_SHIFT = 6
N_SELECT = 16
WINDOW = 512
FORCE_BONUS = 1000.0
ROPE_THETA = 500000.0
ROPE_HALF = 8
N_GROUPS, EXP_PER_GROUP = 8, 8
N_EXPERTS = N_GROUPS * EXP_PER_GROUP
EPS = 1e-6
NEG = -1e30
LOWEST = -3e38

LANES = 128
VMEM_LIMIT_BYTES = 56 * 1024 * 1024

C_DA_Q = DA_HEADS * 2 * DA_DH
C_DA_K = DA_KV * 2 * DA_DH
C_DA_V = DA_KV * DA_DV
C_NSA_Q = NSA_HEADS * NSA_DH
C_NSA_KV = 6 * NSA_KV * NSA_DH
C_NSA_G = 3 * NSA_HEADS
S1 = C_DA_Q
S2 = S1 + C_DA_K
S3 = S2 + C_DA_V
S4 = S3 + C_NSA_Q
S5 = S4 + C_NSA_KV
S6 = S5 + C_NSA_G

EXPERT_ROWS = 128
DECODE_PAGES = 8
FLASH_TILE = 256
FLASH_ROWS = 128


def _cparams(sem):
    return pltpu.CompilerParams(dimension_semantics=sem, vmem_limit_bytes=VMEM_LIMIT_BYTES)


def _pick_tile(n, target, mult):
    best = None
    for t in range(mult, min(n, target) + 1, mult):
        if n % t == 0:
            best = t
    return n if best is None else best


def _round_up(x, m):
    return (x + m - 1) // m * m


def _nt(a, b):
    return lax.dot_general(a, b, (((1,), (1,)), ((), ())), preferred_element_type=F32)


def _dot(a, b, **kw):
    return jnp.dot(a, b, preferred_element_type=F32, **kw)


def _rms(x, g):
    ms = jnp.mean(x * x, axis=-1, keepdims=True)
    return x * lax.rsqrt(ms + EPS) * g


def _proj_rope_kernel(x_ref, g_ref, w_ref, cos_ref, sa_ref, sb_ref, rm_ref, cs_ref,
                      of_ref, ob_ref, h_scr, *, n_chunks):
    @pl.when(pl.program_id(1) == 0)
    def _():
        h_scr[...] = _rms(x_ref[...], g_ref[...]).astype(BF16)

    z = _dot(h_scr[...], w_ref[...])
    c, sa, sb = cos_ref[...], sa_ref[...], sb_ref[...]
    for k in range(n_chunks):
        sl = slice(k * LANES, (k + 1) * LANES)
        zc = z[:, sl]
        rot = zc * c + pltpu.roll(zc, LANES - ROPE_HALF, 1) * sa + pltpu.roll(zc, ROPE_HALF, 1) * sb
        out = jnp.where(rm_ref[:, sl] > 0.5, rot, zc) * cs_ref[:, sl]
        of_ref[:, sl] = out
        ob_ref[:, sl] = out.astype(BF16)


def _proj_rope(xa, g, w, cos, sa, sb, rmask, cscale):
    n, d = xa.shape
    ca = w.shape[1]
    tm = _pick_tile(n, 640, 16)
    tn = _pick_tile(ca, 1024, LANES)
    row = lambda i, j: (i, 0)
    col = lambda i, j: (0, j)
    return pl.pallas_call(
        functools.partial(_proj_rope_kernel, n_chunks=tn // LANES),
        out_shape=(jax.ShapeDtypeStruct((n, ca), F32), jax.ShapeDtypeStruct((n, ca), BF16)),
        grid=(n // tm, ca // tn),
        in_specs=[pl.BlockSpec((tm, d), row), pl.BlockSpec((1, d), lambda i, j: (0, 0)),
                  pl.BlockSpec((d, tn), col),
                  pl.BlockSpec((tm, LANES), row), pl.BlockSpec((tm, LANES), row),
                  pl.BlockSpec((tm, LANES), row),
                  pl.BlockSpec((1, tn), col), pl.BlockSpec((1, tn), col)],
        out_specs=(pl.BlockSpec((tm, tn), lambda i, j: (i, j)),
                   pl.BlockSpec((tm, tn), lambda i, j: (i, j))),
        scratch_shapes=[pltpu.VMEM((tm, d), BF16)],
        compiler_params=_cparams(("parallel", "arbitrary")),
        name="proj_rope",
    )(xa, g, w, cos, sa, sb, rmask, cscale)


def _gate_kernel(x_ref, g_ref, w_ref, o_ref, h_scr):
    @pl.when(pl.program_id(1) == 0)
    def _():
        h_scr[...] = _rms(x_ref[...], g_ref[...]).astype(BF16)

    o_ref[...] = jax.nn.sigmoid(_dot(h_scr[...], w_ref[...]))


def _gate_proj(xa, g, w):
    n, d = xa.shape
    ca = w.shape[1]
    tm = _pick_tile(n, 640, 16)
    tn = _pick_tile(ca, 1024, LANES)
    return pl.pallas_call(
        _gate_kernel,
        out_shape=jax.ShapeDtypeStruct((n, ca), F32),
        grid=(n // tm, ca // tn),
        in_specs=[pl.BlockSpec((tm, d), lambda i, j: (i, 0)), pl.BlockSpec((1, d), lambda i, j: (0, 0)),
                  pl.BlockSpec((d, tn), lambda i, j: (0, j))],
        out_specs=pl.BlockSpec((tm, tn), lambda i, j: (i, j)),
        scratch_shapes=[pltpu.VMEM((tm, d), BF16)],
        compiler_params=_cparams(("parallel", "arbitrary")),
        name="gate_proj",
    )(xa, g, w)


def _softmax_block(s, mask, v, m_old, l_old, acc_old, v_is_transposed=False):
    if mask is not None:
        s = jnp.where(mask, s, NEG)
    m_new = jnp.maximum(m_old, jnp.max(s, axis=-1, keepdims=True))
    p = jnp.exp(s - m_new)
    if mask is not None:
        p = jnp.where(mask, p, 0.0)
    alpha = jnp.exp(m_old - m_new)
    l_new = alpha * l_old + jnp.sum(p, axis=-1, keepdims=True)
    pv = _nt(p.astype(v.dtype), v) if v_is_transposed else _dot(p.astype(v.dtype), v)
    return m_new, l_new, alpha * acc_old + pv


def _flash_update(s, mask, v, m_ref, l_ref, acc_ref, idx, v_is_transposed=False):
    m, l, acc = _softmax_block(s, mask, v, m_ref[idx], l_ref[idx], acc_ref[idx], v_is_transposed)
    m_ref[idx] = m
    l_ref[idx] = l
    acc_ref[idx] = acc


def _flash_init(m_ref, l_ref, acc_ref):
    m_ref[...] = jnp.full(m_ref.shape, NEG, F32)
    l_ref[...] = jnp.zeros(l_ref.shape, F32)
    acc_ref[...] = jnp.zeros(acc_ref.shape, F32)


def _da_lambda(lam_ref, lam_init):
    a = jnp.sum(lam_ref[0:1, :] * lam_ref[1:2, :], axis=-1, keepdims=True)
    b = jnp.sum(lam_ref[2:3, :] * lam_ref[3:4, :], axis=-1, keepdims=True)
    return jnp.exp(a) - jnp.exp(b) + lam_init


def _da_finish(a0, a1, lam, sub, lam_init):
    o = a0 - lam * a1
    return _rms(o, sub) * (1.0 - lam_init)


def _flash_rows(s, mask, v1, m_ref, acc_ref, idx):
    if mask is not None:
        s = jnp.where(mask, s, NEG)
    m_old = m_ref[idx]
    m_new = jnp.maximum(m_old, jnp.max(s, axis=-1, keepdims=True))
    p = jnp.exp(s - jnp.tile(m_new, (1, s.shape[1] // LANES)))
    if mask is not None:
        p = jnp.where(mask, p, 0.0)
    alpha = jnp.exp(m_old - m_new)
    acc_ref[idx] = jnp.tile(alpha, (1, v1.shape[1] // LANES)) * acc_ref[idx] + _dot(p.astype(v1.dtype), v1)
    m_ref[idx] = m_new


def _tri_tables(nq):
    qi = [q for q in range(nq) for _ in range(q + 1)]
    ki = [k for q in range(nq) for k in range(q + 1)]
    return jnp.asarray(qi, I32), jnp.asarray(ki, I32)


def _da_flash_kernel(qi_ref, ki_ref, lam_ref, sub_ref, q_ref, k_ref, v_ref, o_ref, m_scr, acc_scr,
                     *, tq, rb, lam_init):
    t = pl.program_id(2)
    qi = qi_ref[t]
    ki = ki_ref[t]

    @pl.when(ki == 0)
    def _():
        m_scr[...] = jnp.full(m_scr.shape, NEG, F32)
        acc_scr[...] = jnp.zeros(acc_scr.shape, F32)

    def step(diag):
        v1 = jnp.concatenate([v_ref[0, 0], jnp.ones((tq, DA_DV), BF16)], axis=1)
        for c in range(2):
            k = k_ref[0, 0, c]
            for r in range(DA_REP):
                for r0 in range(0, tq, rb):
                    mask = None
                    if diag:
                        row = r0 + lax.broadcasted_iota(I32, (rb, tq), 0)
                        mask = lax.broadcasted_iota(I32, (rb, tq), 1) <= row
                    s = _nt(q_ref[0, 0, c, r, pl.ds(r0, rb), :], k)
                    _flash_rows(s, mask, v1, m_scr, acc_scr, (c, pl.ds(r * tq + r0, rb)))

    @pl.when(ki < qi)
    def _():
        step(False)

    @pl.when(ki == qi)
    def _():
        step(True)
        lam = _da_lambda(lam_ref, lam_init)
        a0 = acc_scr[0, :, :DA_DV] / acc_scr[0, :, DA_DV:]
        a1 = acc_scr[1, :, :DA_DV] / acc_scr[1, :, DA_DV:]
        o = _da_finish(a0, a1, lam, sub_ref[...], lam_init)
        o_ref[0, 0] = o.reshape(DA_REP, tq, DA_DV).astype(o_ref.dtype)


def _da_flash(qd, kd, vd, lam_p, subln, lam_init):
    b, _, _, _, s, _ = qd.shape
    tq = _pick_tile(s, FLASH_TILE, 16)
    rb = _pick_tile(tq, FLASH_ROWS, 16)
    qtab, ktab = _tri_tables(s // tq)
    grid_spec = pltpu.PrefetchScalarGridSpec(
        num_scalar_prefetch=2, grid=(b, DA_KV, qtab.shape[0]),
        in_specs=[pl.BlockSpec((4, DA_DH), lambda bi, g, t, qt, kt: (0, 0)),
                  pl.BlockSpec((1, DA_DV), lambda bi, g, t, qt, kt: (0, 0)),
                  pl.BlockSpec((1, 1, 2, DA_REP, tq, DA_DH), lambda bi, g, t, qt, kt: (bi, g, 0, 0, qt[t], 0)),
                  pl.BlockSpec((1, 1, 2, tq, DA_DH), lambda bi, g, t, qt, kt: (bi, g, 0, kt[t], 0)),
                  pl.BlockSpec((1, 1, tq, DA_DV), lambda bi, g, t, qt, kt: (bi, g, kt[t], 0))],
        out_specs=pl.BlockSpec((1, 1, DA_REP, tq, DA_DV), lambda bi, g, t, qt, kt: (bi, g, 0, qt[t], 0)),
        scratch_shapes=[pltpu.VMEM((2, DA_REP * tq, LANES), F32),
                        pltpu.VMEM((2, DA_REP * tq, 2 * DA_DV), F32)])
    return pl.pallas_call(
        functools.partial(_da_flash_kernel, tq=tq, rb=rb, lam_init=lam_init),
        out_shape=jax.ShapeDtypeStruct((b, DA_KV, DA_REP, s, DA_DV), BF16),
        grid_spec=grid_spec,
        compiler_params=_cparams(("parallel", "parallel", "arbitrary")),
        name="da_flash",
    )(qtab, ktab, lam_p, subln, qd, kd, vd)


def _page_steps(npages):
    pg = _pick_tile(npages, DECODE_PAGES, 1)
    return pg, npages // pg


def _page_specs(block, npages, pg, col_block):
    nd = len(block) - 1

    def spec(u):
        def index(b, p, pt):
            idx = [pt[b, jnp.minimum(p * pg + u, npages - 1)]] + [0] * nd
            if col_block is not None:
                idx[1] = col_block
            return tuple(idx)
        return pl.BlockSpec(block, index)

    return [spec(u) for u in range(pg)]


def _da_decode_kernel(pt_ref, lam_ref, sub_ref, q_ref, *refs, nsteps, pg, page, rq, lam_init):
    del pt_ref
    c_refs = refs[:pg]
    n_ref, o_ref, m_scr, l_scr, acc_scr = refs[pg:]
    p = pl.program_id(1)
    per = 2 * DA_KV

    @pl.when(p == 0)
    def _():
        _flash_init(m_scr, l_scr, acc_scr)

    q = q_ref[0]
    rows = q.shape[0]

    def heads(ref, first):
        return jnp.concatenate([ref[0, pl.ds(first + g, page, stride=per), :] for g in range(DA_KV)],
                               axis=1).astype(BF16)

    def step(blocks, mask):
        k = jnp.concatenate([heads(r, 0) for r in blocks], axis=0)
        v = jnp.concatenate([heads(r, DA_KV) for r in blocks], axis=0)
        _flash_update(_nt(q, k), mask, v, m_scr, l_scr, acc_scr, 0)

    @pl.when(p < nsteps)
    def _():
        step(c_refs, None)

    @pl.when(p == nsteps)
    def _():
        row = lax.broadcasted_iota(I32, (rows, page), 0)
        col = lax.broadcasted_iota(I32, (rows, page), 1)
        step([n_ref], col <= (row % rq) // DA_REP)
        lam = _da_lambda(lam_ref, lam_init)
        a = acc_scr[0] / l_scr[0]
        for g in range(DA_KV):
            cols = slice(g * DA_DV, (g + 1) * DA_DV)
            a0 = a[g * 2 * rq:g * 2 * rq + rq, cols]
            a1 = a[g * 2 * rq + rq:(g + 1) * 2 * rq, cols]
            o_ref[0, g] = _da_finish(a0, a1, lam, sub_ref[...], lam_init)


def _da_decode(page_table, qbd, cache, newkv, lam_p, subln, lam_init):
    nb, rows, ck = qbd.shape
    npages = page_table.shape[1]
    prow = cache.shape[1]
    page = prow // (2 * DA_KV)
    rq = rows // (DA_KV * 2)
    pg, nsteps = _page_steps(npages)
    grid_spec = pltpu.PrefetchScalarGridSpec(
        num_scalar_prefetch=1, grid=(nb, nsteps + 1),
        in_specs=[pl.BlockSpec((4, DA_DH), lambda b, p, pt: (0, 0)),
                  pl.BlockSpec((1, DA_DV), lambda b, p, pt: (0, 0)),
                  pl.BlockSpec((1, rows, ck), lambda b, p, pt: (b, 0, 0))]
        + _page_specs((1, prow, DA_DV), npages, pg, None)
        + [pl.BlockSpec((1, prow, DA_DV), lambda b, p, pt: (b, 0, 0))],
        out_specs=pl.BlockSpec((1, DA_KV, rq, DA_DV), lambda b, p, pt: (b, 0, 0, 0)),
        scratch_shapes=[pltpu.VMEM((1, rows, 1), F32), pltpu.VMEM((1, rows, 1), F32),
                        pltpu.VMEM((1, rows, DA_KV * DA_DV), F32)])
    return pl.pallas_call(
        functools.partial(_da_decode_kernel, nsteps=nsteps, pg=pg, page=page, rq=rq, lam_init=lam_init),
        out_shape=jax.ShapeDtypeStruct((nb, DA_KV, rq, DA_DV), F32),
        grid_spec=grid_spec,
        compiler_params=_cparams(("parallel", "arbitrary")),
        name="da_decode",
    )(page_table, lam_p, subln, qbd, *([cache] * pg), newkv)


def _compress_kernel(k_ref, v_ref, wp_ref, wc_ref, ko_ref, vo_ref):
    for c, (src, dst) in enumerate(((k_ref, ko_ref), (v_ref, vo_ref))):
        x = src[...]
        t = x.shape[0]
        y = jnp.sum(x.reshape(t // BLOCK, BLOCK, LANES) * wp_ref[c][None], axis=1)
        dst[...] = _dot(y, wc_ref[c], precision=HIGHEST)


def _compress_prompt(zf, wp, wc, b, s):
    nc = s // BLOCK
    kc0 = S4 // LANES
    out = jax.ShapeDtypeStruct((b * nc, LANES), F32)
    return pl.pallas_call(
        _compress_kernel,
        out_shape=(out, out),
        grid=(b,),
        in_specs=[pl.BlockSpec((s, LANES), lambda i: (i, kc0)), pl.BlockSpec((s, LANES), lambda i: (i, kc0 + 1)),
                  pl.BlockSpec((2, BLOCK, LANES), lambda i: (0, 0, 0)),
                  pl.BlockSpec((2, LANES, LANES), lambda i: (0, 0, 0))],
        out_specs=(pl.BlockSpec((nc, LANES), lambda i: (i, 0)), pl.BlockSpec((nc, LANES), lambda i: (i, 0))),
        compiler_params=_cparams(("parallel",)),
        name="nsa_compress",
    )(zf, zf, wp, wc)


def _masked_probs(s, ok):
    s = jnp.where(ok, s, NEG)
    e = jnp.where(ok, jnp.exp(s - jnp.max(s, axis=-1, keepdims=True)), 0.0)
    den = jnp.sum(e, axis=-1, keepdims=True)
    return e / jnp.where(den > 0.0, den, 1.0)


def _select_blocks(imp, pos, ns):
    width = imp.shape[1]
    blk = lax.broadcasted_iota(I32, (1, width), 1)
    cur = lax.shift_right_logical(pos, BLOCK_SHIFT)
    forced = (blk == 0) | (blk == cur) | (blk == cur - 1)
    score = jnp.where(forced, imp + FORCE_BONUS, imp)
    score = jnp.where(blk <= cur, score, -1.0)
    score = jnp.where(blk < ns, score, 0.5 * LOWEST)
    sel = jnp.zeros(imp.shape, F32)
    for _ in range(N_SELECT):
        mx = jnp.max(score, axis=-1, keepdims=True)
        idx = jnp.min(jnp.where(score == mx, blk, width), axis=-1, keepdims=True)
        hit = blk == idx
        sel = jnp.where(hit, 1.0, sel)
        score = jnp.where(hit, LOWEST, score)
    return sel


def _cmp_select_kernel(q_ref, kc_ref, vc_ref, o_ref, sel_ref, *, tq, nc, ns, width):
    qi = pl.program_id(1)
    pos = qi * tq + lax.broadcasted_iota(I32, (tq, 1), 0)
    blk = lax.broadcasted_iota(I32, (1, width), 1)
    cmp_ok = (blk < nc) & ((blk + 1) * BLOCK - 1 <= pos)
    for g in range(NSA_KV):
        kc = kc_ref[0, g]
        vc = vc_ref[0, g]
        imp = jnp.zeros((tq, width), F32)
        for r in range(NSA_REP):
            h = g * NSA_REP + r
            p = _masked_probs(_nt(q_ref[0, h], kc), cmp_ok)
            imp = imp + p
            o_ref[0, h] = _dot(p.astype(vc.dtype), vc)
        sel_ref[0, g] = _select_blocks(imp, pos, ns).astype(sel_ref.dtype)


def _cmp_decode_kernel(pt_ref, q_ref, wt_ref, wct_ref, *refs, nsteps, pg, page, past, t4, nc, ns, width):
    del pt_ref
    c_refs = refs[:pg]
    o_ref, sel_ref, acc = refs[pg:]
    p = pl.program_id(1)
    kw = NSA_KV * NSA_DH
    cw = acc.shape[1]

    @pl.when(p == 0)
    def _():
        acc[...] = jnp.zeros(acc.shape, F32)

    z = jnp.concatenate([c_refs[u][0].reshape(2 * kw, page) * wt_ref[...] for u in range(pg)], axis=1)
    tokblk = lax.shift_right_logical(lax.broadcasted_iota(I32, (pg * page, cw), 0), BLOCK_SHIFT)
    lane = lax.broadcasted_iota(I32, (pg * page, cw), 1)
    place = jnp.where(lane == p * (pg * page // BLOCK) + tokblk, 1.0, 0.0)
    acc[...] += _dot(z, place, precision=HIGHEST)

    @pl.when(p == nsteps - 1)
    def _():
        y = acc[...]
        rows = t4 * NSA_REP
        pos = past + lax.broadcasted_iota(I32, (rows, 1), 0) // NSA_REP
        blk = lax.broadcasted_iota(I32, (1, cw), 1)
        cmp_ok = (blk < nc) & ((blk + 1) * BLOCK - 1 <= pos)
        for g in range(NSA_KV):
            kct = _dot(wct_ref[0], y[g * NSA_DH:(g + 1) * NSA_DH], precision=HIGHEST)
            vct = _dot(wct_ref[1], y[kw + g * NSA_DH:kw + (g + 1) * NSA_DH], precision=HIGHEST)
            pr = _masked_probs(_dot(q_ref[0, g], kct.astype(BF16)), cmp_ok)
            o_ref[0, g] = _nt(pr.astype(BF16), vct.astype(BF16))
            imp = jnp.sum(pr.reshape(t4, NSA_REP, cw), axis=1)
            if width > cw:
                imp = jnp.concatenate([imp, jnp.zeros((t4, width - cw), F32)], axis=1)
            tpos = past + lax.broadcasted_iota(I32, (t4, 1), 0)
            sel_ref[0, g] = _select_blocks(imp, tpos, ns).astype(sel_ref.dtype)


def _cmp_decode(page_table, qst, cache_t, wt, wct, *, past, t4, nc, ns, width):
    nb, npages = page_table.shape
    page = cache_t.shape[4]
    pg, nsteps = _page_steps(npages)
    rows = t4 * NSA_REP
    cw = _round_up(nc, LANES)
    kw = NSA_KV * NSA_DH
    grid_spec = pltpu.PrefetchScalarGridSpec(
        num_scalar_prefetch=1, grid=(nb, nsteps),
        in_specs=[pl.BlockSpec((1, NSA_KV, rows, NSA_DH), lambda b, p, pt: (b, 0, 0, 0)),
                  pl.BlockSpec((2 * kw, page), lambda b, p, pt: (0, 0)),
                  pl.BlockSpec((2, NSA_DH, NSA_DH), lambda b, p, pt: (0, 0, 0))]
        + _page_specs((1, 2, NSA_KV, NSA_DH, page), npages, pg, 0),
        out_specs=(pl.BlockSpec((1, NSA_KV, rows, NSA_DH), lambda b, p, pt: (b, 0, 0, 0)),
                   pl.BlockSpec((1, NSA_KV, t4, width), lambda b, p, pt: (b, 0, 0, 0))),
        scratch_shapes=[pltpu.VMEM((2 * kw, cw), F32)])
    return pl.pallas_call(
        functools.partial(_cmp_decode_kernel, nsteps=nsteps, pg=pg, page=page, past=past, t4=t4,
                          nc=nc, ns=ns, width=width),
        out_shape=(jax.ShapeDtypeStruct((nb, NSA_KV, rows, NSA_DH), F32),
                   jax.ShapeDtypeStruct((nb, NSA_KV, t4, width), BF16)),
        grid_spec=grid_spec,
        compiler_params=_cparams(("parallel", "arbitrary")),
        name="nsa_cmp_decode",
    )(page_table, qst, wt, wct, *([cache_t] * pg))


def _cmp_select(qn, kc, vc, *, nc, ns):
    b, _, s, _ = qn.shape
    width = kc.shape[2]
    tq = _pick_tile(s, 256, 16)
    return pl.pallas_call(
        functools.partial(_cmp_select_kernel, tq=tq, nc=nc, ns=ns, width=width),
        out_shape=(jax.ShapeDtypeStruct((b, NSA_HEADS, s, NSA_DH), F32),
                   jax.ShapeDtypeStruct((b, NSA_KV, s, width), BF16)),
        grid=(b, s // tq),
        in_specs=[pl.BlockSpec((1, NSA_HEADS, tq, NSA_DH), lambda bi, qi: (bi, 0, qi, 0)),
                  pl.BlockSpec((1, NSA_KV, width, NSA_DH), lambda bi, qi: (bi, 0, 0, 0)),
                  pl.BlockSpec((1, NSA_KV, width, NSA_DH), lambda bi, qi: (bi, 0, 0, 0))],
        out_specs=(pl.BlockSpec((1, NSA_HEADS, tq, NSA_DH), lambda bi, qi: (bi, 0, qi, 0)),
                   pl.BlockSpec((1, NSA_KV, tq, width), lambda bi, qi: (bi, 0, qi, 0))),
        compiler_params=_cparams(("parallel", "parallel")),
        name="nsa_cmp_select",
    )(qn, kc, vc)


def _nsa_flash_kernel(qi_ref, ki_ref, *refs, mode, tq, rb, nsteps):
    if mode == "slc":
        q_ref, k_ref, v_ref, sel_ref, e_ref, o_ref, m_scr, acc_scr = refs
    else:
        q_ref, k_ref, v_ref, o_ref, m_scr, acc_scr = refs
    t = pl.program_id(2)
    qi = qi_ref[t]
    ki = ki_ref[t]
    first = (t == 0) | (qi_ref[jnp.maximum(t - 1, 0)] != qi)
    last = (t == nsteps - 1) | (qi_ref[jnp.minimum(t + 1, nsteps - 1)] != qi)

    @pl.when(first)
    def _():
        m_scr[...] = jnp.full(m_scr.shape, NEG, F32)
        acc_scr[...] = jnp.zeros(acc_scr.shape, F32)

    if mode == "slc":
        picked = _dot(sel_ref[0, 0], e_ref[...])
    k = k_ref[0, 0]
    v1 = v_ref[0, 0]
    for r0 in range(0, tq, rb):
        d = ((qi - ki) * tq + r0 + lax.broadcasted_iota(I32, (rb, tq), 0)
             - lax.broadcasted_iota(I32, (rb, tq), 1))
        if mode == "slc":
            mask = (picked[r0:r0 + rb] > 0.5) & (d >= 0)
        else:
            mask = (d >= 0) & (d < WINDOW)
        for h in range(NSA_REP):
            s = _nt(q_ref[0, h, pl.ds(r0, rb), :], k)
            _flash_rows(s, mask, v1, m_scr, acc_scr, (h, pl.ds(r0, rb)))

    @pl.when(last)
    def _():
        acc = acc_scr[...].reshape(NSA_REP * tq, LANES)
        den = pltpu.roll(acc, NSA_DH, 1)
        o_ref[0] = (acc[:, :NSA_DH] / den[:, :NSA_DH]).reshape(NSA_REP, tq, NSA_DH)


def _nsa_flash(qn, k, v1, sel=None, emat=None):
    b, _, s, _ = qn.shape
    tq = _pick_tile(s, FLASH_TILE, 16)
    rb = _pick_tile(tq, FLASH_ROWS, 16)
    nq = s // tq
    if sel is not None:
        mode = "slc"
        qtab, ktab = _tri_tables(nq)
    else:
        mode = "win"
        back = -(-WINDOW // tq)
        pairs = [(q, kk) for q in range(nq) for kk in range(max(q - back, 0), q + 1)]
        qtab = jnp.asarray([p[0] for p in pairs], I32)
        ktab = jnp.asarray([p[1] for p in pairs], I32)
    nsteps = qtab.shape[0]
    in_specs = [pl.BlockSpec((1, NSA_REP, tq, NSA_DH), lambda bi, g, t, qt, kt: (bi, g, qt[t], 0)),
                pl.BlockSpec((1, 1, tq, NSA_DH), lambda bi, g, t, qt, kt: (bi, g, kt[t], 0)),
                pl.BlockSpec((1, 1, tq, LANES), lambda bi, g, t, qt, kt: (bi, g, kt[t], 0))]
    args = [qn, k, v1]
    if sel is not None:
        width = sel.shape[3]
        in_specs += [pl.BlockSpec((1, 1, tq, width), lambda bi, g, t, qt, kt: (bi, g, qt[t], 0)),
                     pl.BlockSpec((width, tq), lambda bi, g, t, qt, kt: (0, kt[t]))]
        args += [sel, emat]
    grid_spec = pltpu.PrefetchScalarGridSpec(
        num_scalar_prefetch=2, grid=(b, NSA_KV, nsteps),
        in_specs=in_specs,
        out_specs=pl.BlockSpec((1, NSA_REP, tq, NSA_DH), lambda bi, g, t, qt, kt: (bi, g, qt[t], 0)),
        scratch_shapes=[pltpu.VMEM((NSA_REP, tq, LANES), F32), pltpu.VMEM((NSA_REP, tq, LANES), F32)])
    return pl.pallas_call(
        functools.partial(_nsa_flash_kernel, mode=mode, tq=tq, rb=rb, nsteps=nsteps),
        out_shape=jax.ShapeDtypeStruct((b, NSA_HEADS, s, NSA_DH), F32),
        grid_spec=grid_spec,
        compiler_params=_cparams(("parallel", "parallel", "arbitrary")),
        name="nsa_" + mode,
    )(qtab, ktab, *args)


def _row_token(rows, cols, t4):
    row = lax.broadcasted_iota(I32, (rows, cols), 0)
    return (row // NSA_REP) % t4


def _kv_t(blocks):
    kw = NSA_KV * NSA_DH
    kt = [r[0, 0].reshape(kw, r.shape[4]) for r in blocks]
    vt = [r[0, 1].reshape(kw, r.shape[4]) for r in blocks]
    cat = lambda xs: (xs[0] if len(xs) == 1 else jnp.concatenate(xs, axis=1)).astype(BF16)
    return cat(kt), cat(vt)


def _slc_decode_kernel(pt_ref, q_ref, sel_ref, e_ref, *refs, nsteps, pg, page, past, t4):
    del pt_ref
    c_refs = refs[:pg]
    n_ref, o_ref, m_scr, l_scr, acc_scr = refs[pg:]
    p = pl.program_id(1)

    @pl.when(p == 0)
    def _():
        _flash_init(m_scr, l_scr, acc_scr)

    q = q_ref[0]
    rows = q.shape[0]

    def step(blocks):
        kt, vt = _kv_t(blocks)
        nk = kt.shape[1]
        kpos = p * (pg * page) + lax.broadcasted_iota(I32, (rows, nk), 1)
        picked = _dot(sel_ref[0], e_ref[:, :nk]) > 0.5
        mask = picked & (kpos <= past + _row_token(rows, nk, t4))
        _flash_update(_dot(q, kt), mask, vt, m_scr, l_scr, acc_scr, 0, v_is_transposed=True)

    @pl.when(p < nsteps)
    def _():
        step(c_refs)

    @pl.when(p == nsteps)
    def _():
        step([n_ref])
        o_ref[0] = acc_scr[0] / l_scr[0]


def _slc_decode(page_table, qnb, selrows, emat, cache_t, new_t, *, past, t4):
    nb, rows, _ = qnb.shape
    npages = page_table.shape[1]
    page = cache_t.shape[4]
    width = selrows.shape[2]
    pg, nsteps = _page_steps(npages)
    blk = (1, 2, NSA_KV, NSA_DH, page)
    grid_spec = pltpu.PrefetchScalarGridSpec(
        num_scalar_prefetch=1, grid=(nb, nsteps + 1),
        in_specs=[pl.BlockSpec((1, rows, LANES), lambda b, p, pt: (b, 0, 0)),
                  pl.BlockSpec((1, rows, width), lambda b, p, pt: (b, 0, 0)),
                  pl.BlockSpec((width, pg * page), lambda b, p, pt: (0, p))]
        + _page_specs(blk, npages, pg, 1)
        + [pl.BlockSpec(blk, lambda b, p, pt: (b, 0, 0, 0, 0))],
        out_specs=pl.BlockSpec((1, rows, LANES), lambda b, p, pt: (b, 0, 0)),
        scratch_shapes=[pltpu.VMEM((1, rows, 1), F32), pltpu.VMEM((1, rows, 1), F32),
                        pltpu.VMEM((1, rows, LANES), F32)])
    return pl.pallas_call(
        functools.partial(_slc_decode_kernel, nsteps=nsteps, pg=pg, page=page, past=past, t4=t4),
        out_shape=jax.ShapeDtypeStruct((nb, rows, LANES), F32),
        grid_spec=grid_spec,
        compiler_params=_cparams(("parallel", "arbitrary")),
        name="nsa_slc_decode",
    )(page_table, qnb, selrows, emat, *([cache_t] * pg), new_t)


def _win_decode_kernel(q_ref, c_ref, n_ref, o_ref, *, past, t4):
    q = q_ref[0]
    rows = q.shape[0]
    wb = c_ref.shape[4]
    m = jnp.full((rows, 1), NEG, F32)
    l = jnp.zeros((rows, 1), F32)
    acc = jnp.zeros((rows, LANES), F32)
    for ref, kstart in ((c_ref, past - wb), (n_ref, past)):
        kt, vt = _kv_t([ref])
        nk = kt.shape[1]
        d = past + _row_token(rows, nk, t4) - (kstart + lax.broadcasted_iota(I32, (rows, nk), 1))
        mask = (d >= 0) & (d < WINDOW)
        m, l, acc = _softmax_block(_dot(q, kt), mask, vt, m, l, acc, v_is_transposed=True)
    o_ref[0] = acc / l


def _win_decode(qnb, cwin_t, new_t, *, past, t4):
    nb, rows, _ = qnb.shape
    return pl.pallas_call(
        functools.partial(_win_decode_kernel, past=past, t4=t4),
        out_shape=jax.ShapeDtypeStruct((nb, rows, LANES), F32),
        grid=(nb,),
        in_specs=[pl.BlockSpec((1, rows, LANES), lambda b: (b, 0, 0)),
                  pl.BlockSpec((1,) + cwin_t.shape[1:], lambda b: (b, 0, 0, 0, 0)),
                  pl.BlockSpec((1,) + new_t.shape[1:], lambda b: (b, 0, 0, 0, 0))],
        out_specs=pl.BlockSpec((1, rows, LANES), lambda b: (b, 0, 0)),
        compiler_params=_cparams(("parallel",)),
        name="nsa_win_decode",
    )(qnb, cwin_t, new_t)


def _merge_kernel(oda_ref, oc_ref, os_ref, ow_ref, gn_ref, ga_ref, gb_ref, e_ref, wda_ref, wns_ref, o_ref):
    g = gn_ref[...]
    ghi = g.astype(BF16)
    glo = (g - ghi.astype(F32)).astype(BF16)
    ons = None
    for b, src in enumerate((oc_ref, os_ref, ow_ref)):
        ge = _dot(ghi, e_ref[b]) + _dot(glo, e_ref[b])
        term = ge * src[...]
        ons = term if ons is None else ons + term
    yda = _dot(oda_ref[...], wda_ref[...])
    yns = _dot(ons.astype(BF16), wns_ref[...])
    o_ref[...] = (ga_ref[...] * yda + gb_ref[...] * yns).astype(o_ref.dtype)


def _merge(oda, ocmp, oslc, owin, gns, gmg, emat, wda, wns):
    n, hd = oda.shape
    d = wda.shape[1]
    tm = _pick_tile(n, 320, 16)
    row = lambda i: (i, 0)
    const2 = lambda i: (0, 0)
    return pl.pallas_call(
        _merge_kernel,
        out_shape=jax.ShapeDtypeStruct((n, d), BF16),
        grid=(n // tm,),
        in_specs=[pl.BlockSpec((tm, hd), row), pl.BlockSpec((tm, hd), row), pl.BlockSpec((tm, hd), row),
                  pl.BlockSpec((tm, hd), row), pl.BlockSpec((tm, LANES), row),
                  pl.BlockSpec((tm, d), lambda i: (i, 0)), pl.BlockSpec((tm, d), lambda i: (i, 1)),
                  pl.BlockSpec((3, LANES, hd), lambda i: (0, 0, 0)),
                  pl.BlockSpec((hd, d), const2), pl.BlockSpec((hd, d), const2)],
        out_specs=pl.BlockSpec((tm, d), row),
        compiler_params=_cparams(("parallel",)),
        name="mixer_merge",
    )(oda, ocmp, oslc, owin, gns, gmg, gmg, emat, wda, wns)


def _lane_pick(x, lane, k):
    return jnp.sum(jnp.where(lane == k, x, 0.0), axis=-1, keepdims=True)


def _route(logits):
    lane = lax.broadcasted_iota(I32, logits.shape, 1)
    isg = lane < N_GROUPS
    gmax = jnp.max(jnp.where(isg, logits, LOWEST), axis=-1, keepdims=True)
    gsel = jnp.min(jnp.where(isg & (logits == gmax), lane, LANES), axis=-1, keepdims=True)
    pgrp = 1.0 / jnp.sum(jnp.where(isg, jnp.exp(logits - gmax), 0.0), axis=-1, keepdims=True)
    lo = N_GROUPS + gsel * EXP_PER_GROUP
    ing = (lane >= lo) & (lane < lo + EXP_PER_GROUP)
    v1 = jnp.max(jnp.where(ing, logits, LOWEST), axis=-1, keepdims=True)
    i1 = jnp.min(jnp.where(ing & (logits == v1), lane, LANES), axis=-1, keepdims=True)
    ing2 = ing & (lane != i1)
    v2 = jnp.max(jnp.where(ing2, logits, LOWEST), axis=-1, keepdims=True)
    i2 = jnp.min(jnp.where(ing2 & (logits == v2), lane, LANES), axis=-1, keepdims=True)
    t = jnp.exp(v2 - v1)
    w1 = pgrp / (1.0 + t)
    w2 = pgrp * t / (1.0 + t)
    e1 = (i1 - N_GROUPS).astype(F32)
    e2 = (i2 - N_GROUPS).astype(F32)
    return jnp.where(lane == 0, e1, jnp.where(lane == 1, e2, jnp.where(lane == 2, w1, jnp.where(lane == 3, w2, 0.0))))


def _store_rows(ref, y, rows, nchunk):
    for s in range(nchunk):
        ref[pl.ds(s, rows, stride=nchunk), :] = y[:, s * LANES:(s + 1) * LANES]


def _load_rows(ref, start, rows, stride, nchunk):
    return jnp.concatenate([ref[pl.ds(start + s, rows, stride=stride), :] for s in range(nchunk)], axis=1)


def _wo_route_kernel(m_ref, x_ref, wo_ref, gf_ref, wr_ref, br_ref, x1_ref, h_ref, rt_ref, *, nchunk):
    x1 = x_ref[...] + _dot(m_ref[...], wo_ref[...])
    x1_ref[...] = x1
    h = _rms(x1, gf_ref[...])
    _store_rows(h_ref, h, h.shape[0], nchunk)
    rt_ref[...] = _route(_dot(h, wr_ref[...], precision=HIGHEST) + br_ref[...])


def _wo_route(mrg, xa, wo, gffn, wr, br):
    n, d = xa.shape
    tm = _pick_tile(n, 320, 16)
    nchunk = d // LANES
    row = lambda i: (i, 0)
    const2 = lambda i: (0, 0)
    return pl.pallas_call(
        functools.partial(_wo_route_kernel, nchunk=nchunk),
        out_shape=(jax.ShapeDtypeStruct((n, d), F32), jax.ShapeDtypeStruct((n * nchunk, LANES), F32),
                   jax.ShapeDtypeStruct((n, LANES), F32)),
        grid=(n // tm,),
        in_specs=[pl.BlockSpec((tm, d), row), pl.BlockSpec((tm, d), row), pl.BlockSpec((d, d), const2),
                  pl.BlockSpec((1, d), const2), pl.BlockSpec((d, LANES), const2), pl.BlockSpec((1, LANES), const2)],
        out_specs=(pl.BlockSpec((tm, d), row), pl.BlockSpec((tm * nchunk, LANES), row),
                   pl.BlockSpec((tm, LANES), row)),
        compiler_params=_cparams(("parallel",)),
        name="wo_route",
    )(mrg, xa, wo, gffn, wr, br)


def _onehots(rt):
    lane = lax.broadcasted_iota(I32, rt.shape, 1)
    lane_f = lane.astype(F32)
    oh0 = jnp.where(lane_f == _lane_pick(rt, lane, 0), 1.0, 0.0)
    oh1 = jnp.where(lane_f == _lane_pick(rt, lane, 1), 1.0, 0.0)
    return lane, oh0, oh1


def _rank_kernel(rt_ref, tri_ref, rk_ref, sz_ref, carry):
    @pl.when(pl.program_id(0) == 0)
    def _():
        carry[...] = jnp.zeros(carry.shape, F32)

    lane, oh0, oh1 = _onehots(rt_ref[...])
    oh = oh0 + oh1
    before = _dot(tri_ref[...], oh.astype(BF16)) + carry[0:1, :]
    r0 = jnp.sum(before * oh0, axis=-1, keepdims=True)
    r1 = jnp.sum(before * oh1, axis=-1, keepdims=True)
    rk_ref[...] = jnp.where(lane == 0, r0, jnp.where(lane == 1, r1, 0.0))
    carry[...] = carry[...] + jnp.sum(oh, axis=0, keepdims=True)
    sz_ref[...] = carry[...]


def _expert_ranks(route, tri):
    n = route.shape[0]
    tm = tri.shape[0]
    return pl.pallas_call(
        _rank_kernel,
        out_shape=(jax.ShapeDtypeStruct((n, LANES), F32), jax.ShapeDtypeStruct((8, LANES), F32)),
        grid=(n // tm,),
        in_specs=[pl.BlockSpec((tm, LANES), lambda i: (i, 0)), pl.BlockSpec((tm, tm), lambda i: (0, 0))],
        out_specs=(pl.BlockSpec((tm, LANES), lambda i: (i, 0)), pl.BlockSpec((8, LANES), lambda i: (0, 0))),
        scratch_shapes=[pltpu.VMEM((8, LANES), F32)],
        compiler_params=_cparams(("arbitrary",)),
        name="moe_rank",
    )(route, tri)


def _dest_kernel(rt_ref, rk_ref, sz_ref, dest_ref, be_ref, nu_ref, *, shift):
    lane8 = lax.broadcasted_iota(I32, (8, LANES), 1)
    sizes = sz_ref[...].astype(I32)
    blk = 1 << shift
    padded = jnp.where(lane8 < N_EXPERTS, lax.shift_right_logical(sizes + (blk - 1), shift) * blk, 0)
    ends = padded
    d = 1
    while d < LANES:
        ends = ends + jnp.where(lane8 >= d, pltpu.roll(ends, d, 1), 0)
        d *= 2
    pstart = (ends - padded)[0:1, :].astype(F32)
    rt = rt_ref[...]
    lane, oh0, oh1 = _onehots(rt)
    rk = rk_ref[...]
    d0 = jnp.sum(oh0 * pstart, axis=-1, keepdims=True) + _lane_pick(rk, lane, 0)
    d1 = jnp.sum(oh1 * pstart, axis=-1, keepdims=True) + _lane_pick(rk, lane, 1)
    dest_ref[...] = jnp.where(lane == 0, d0, jnp.where(lane == 1, d1, 0.0)).astype(I32)
    nbt = be_ref.shape[0]
    lane_b = lax.broadcasted_iota(I32, (nbt, LANES), 1)
    first_row = lax.broadcasted_iota(I32, (nbt, LANES), 0) * blk
    cnt = jnp.sum(jnp.where((lane_b < N_EXPERTS) & (ends[0:1, :] <= first_row), 1, 0), axis=-1, keepdims=True)
    be_ref[...] = jnp.broadcast_to(jnp.minimum(cnt, N_EXPERTS - 1), (nbt, LANES))
    total = jnp.sum(jnp.where(lane8 == N_EXPERTS - 1, ends, 0), axis=-1, keepdims=True)
    nu_ref[...] = jnp.broadcast_to(lax.shift_right_logical(total, shift), (8, LANES))


def _destinations(route, rank, sizes, nbt, shift):
    n = route.shape[0]
    tm = _pick_tile(n, 640, 8)
    nbt_p = _round_up(nbt, 8)
    return pl.pallas_call(
        functools.partial(_dest_kernel, shift=shift),
        out_shape=(jax.ShapeDtypeStruct((n, LANES), I32), jax.ShapeDtypeStruct((nbt_p, LANES), I32),
                   jax.ShapeDtypeStruct((8, LANES), I32)),
        grid=(n // tm,),
        in_specs=[pl.BlockSpec((tm, LANES), lambda i: (i, 0)), pl.BlockSpec((tm, LANES), lambda i: (i, 0)),
                  pl.BlockSpec((8, LANES), lambda i: (0, 0))],
        out_specs=(pl.BlockSpec((tm, LANES), lambda i: (i, 0)), pl.BlockSpec((nbt_p, LANES), lambda i: (0, 0)),
                   pl.BlockSpec((8, LANES), lambda i: (0, 0))),
        compiler_params=_cparams(("arbitrary",)),
        name="moe_dest",
    )(route, rank, sizes)


def _expert_kernel(dest_ref, be_ref, nu_ref, h_ref, wg_ref, wu_ref, wd_ref, o_ref,
                   tok, xbuf, sem, wgb, wub, wdb, *, rows, nchunk, n_assign):
    j = pl.program_id(0)
    n_used = nu_ref[0]

    def row_copy(tile, slot, r):
        return pltpu.make_async_copy(h_ref.at[tok[tile * rows + r]],
                                     xbuf.at[slot, pl.ds(r * nchunk, nchunk)], sem.at[slot])

    def start_tile(tile, slot):
        def body(r, carry):
            row_copy(tile, slot, r).start()
            return carry
        lax.fori_loop(0, rows, body, 0)

    def wait_tile(tile, slot):
        def body(r, carry):
            row_copy(tile, slot, r).wait()
            return carry
        lax.fori_loop(0, rows, body, 0)

    @pl.when(j == 0)
    def _():
        def clear(r, carry):
            tok[r] = 0
            return carry
        lax.fori_loop(0, n_used * rows, clear, 0)

        def place(a, carry):
            tok[dest_ref[a]] = lax.shift_right_logical(a, 1)
            return carry
        lax.fori_loop(0, n_assign, place, 0)
        start_tile(0, 0)

    slot = lax.rem(j, 2)
    active = j < n_used
    changed = (j == 0) | (be_ref[j] != be_ref[jnp.maximum(j - 1, 0)])

    @pl.when(active & changed)
    def _():
        wgb[...] = wg_ref[0].astype(BF16)
        wub[...] = wu_ref[0].astype(BF16)
        wdb[...] = wd_ref[0].astype(BF16)

    @pl.when(active)
    def _():
        @pl.when(j + 1 < n_used)
        def _():
            start_tile(j + 1, 1 - slot)

        wait_tile(j, slot)
        x = jnp.concatenate([xbuf[slot, pl.ds(s, rows, stride=nchunk), :] for s in range(nchunk)],
                            axis=1).astype(BF16)
        a = _dot(x, wgb[...])
        u = _dot(x, wub[...])
        y = _dot((jax.nn.silu(a) * u).astype(BF16), wdb[...])
        _store_rows(o_ref, y, rows, nchunk)

    @pl.when(jnp.logical_not(active))
    def _():
        o_ref[...] = jnp.zeros(o_ref.shape, F32)


def _experts(dest, blk_e, n_used, h3, wg, wu, wd, rows):
    d, de = wg.shape[1], wg.shape[2]
    nchunk = d // LANES
    nbt = blk_e.shape[0]
    grid_spec = pltpu.PrefetchScalarGridSpec(
        num_scalar_prefetch=3, grid=(nbt,),
        in_specs=[pl.BlockSpec(memory_space=pl.ANY),
                  pl.BlockSpec((1, d, de), lambda j, dst, be, nu: (be[j], 0, 0)),
                  pl.BlockSpec((1, d, de), lambda j, dst, be, nu: (be[j], 0, 0)),
                  pl.BlockSpec((1, de, d), lambda j, dst, be, nu: (be[j], 0, 0))],
        out_specs=pl.BlockSpec((rows * nchunk, LANES), lambda j, dst, be, nu: (j, 0)),
        scratch_shapes=[pltpu.SMEM((nbt * rows,), I32), pltpu.VMEM((2, rows * nchunk, LANES), F32),
                        pltpu.SemaphoreType.DMA((2,)),
                        pltpu.VMEM((d, de), BF16), pltpu.VMEM((d, de), BF16), pltpu.VMEM((de, d), BF16)])
    return pl.pallas_call(
        functools.partial(_expert_kernel, rows=rows, nchunk=nchunk, n_assign=dest.shape[0]),
        out_shape=jax.ShapeDtypeStruct((nbt * rows * nchunk, LANES), F32),
        grid_spec=grid_spec,
        compiler_params=_cparams(("arbitrary",)),
        name="moe_experts",
    )(dest, blk_e, n_used, h3, wg, wu, wd)


def _combine_kernel(dest_ref, ys_ref, rt_ref, x1_ref, p_ref, gp_ref, wpg_ref, wpl_ref, gfin_ref, o_ref,
                    gbuf, sem, *, tm, nchunk, final):
    base = pl.program_id(0) * (2 * tm)

    def copy(j):
        return pltpu.make_async_copy(ys_ref.at[dest_ref[base + j]], gbuf.at[pl.ds(j * nchunk, nchunk)], sem)

    def issue(j, carry):
        copy(j).start()
        return carry

    def drain(j, carry):
        copy(j).wait()
        return carry

    lax.fori_loop(0, 2 * tm, issue, 0)
    lax.fori_loop(0, 2 * tm, drain, 0)
    rt = rt_ref[...]
    lane = lax.broadcasted_iota(I32, rt.shape, 1)
    y0 = _load_rows(gbuf, 0, tm, 2 * nchunk, nchunk)
    y1 = _load_rows(gbuf, nchunk, tm, 2 * nchunk, nchunk)
    x2 = x1_ref[...] + (_lane_pick(rt, lane, 2) * y0 + _lane_pick(rt, lane, 3) * y1)
    gate = jax.nn.sigmoid(_dot(_rms(x2, gp_ref[...]).astype(BF16), wpg_ref[...]))
    x3 = x2 + gate * _dot(p_ref[...].astype(BF16), wpl_ref[...])
    o_ref[...] = _rms(x3, gfin_ref[...]) if final else x3


def _combine(dest, ys3, route, x1, pa, gple, wpg, wpl, gfin, final):
    n, d = x1.shape
    nchunk = d // LANES
    tm = _pick_tile(n, 320, 16)
    dp = pa.shape[1]
    row = lambda i, dst: (i, 0)
    const2 = lambda i, dst: (0, 0)
    grid_spec = pltpu.PrefetchScalarGridSpec(
        num_scalar_prefetch=1, grid=(n // tm,),
        in_specs=[pl.BlockSpec(memory_space=pl.ANY),
                  pl.BlockSpec((tm, LANES), row), pl.BlockSpec((tm, d), row), pl.BlockSpec((tm, dp), row),
                  pl.BlockSpec((1, d), const2), pl.BlockSpec((d, d), const2), pl.BlockSpec((dp, d), const2),
                  pl.BlockSpec((1, d), const2)],
        out_specs=pl.BlockSpec((tm, d), row),
        scratch_shapes=[pltpu.VMEM((2 * tm * nchunk, LANES), F32), pltpu.SemaphoreType.DMA(())])
    return pl.pallas_call(
        functools.partial(_combine_kernel, tm=tm, nchunk=nchunk, final=final),
        out_shape=jax.ShapeDtypeStruct((n, d), F32),
        grid_spec=grid_spec,
        compiler_params=_cparams(("arbitrary",)),
        name="moe_combine_tail",
    )(dest, ys3, route, x1, pa, gple, wpg, wpl, gfin)


def _rope_tables(pos):
    inv = ROPE_THETA ** (-jnp.arange(ROPE_HALF, dtype=F32) / ROPE_HALF)
    ang = pos.astype(F32)[:, None] * inv[None, :]
    cos, sin = jnp.cos(ang), jnp.sin(ang)
    n = pos.shape[0]
    rest = DA_DH - 2 * ROPE_HALF
    z = jnp.zeros((n, ROPE_HALF), F32)
    zr = jnp.zeros((n, rest), F32)
    c64 = jnp.concatenate([cos, cos, jnp.ones((n, rest), F32)], axis=1)
    a64 = jnp.concatenate([-sin, z, zr], axis=1)
    b64 = jnp.concatenate([z, sin, zr], axis=1)
    rep = LANES // DA_DH
    return jnp.tile(c64, (1, rep)), jnp.tile(a64, (1, rep)), jnp.tile(b64, (1, rep))


def _column_tables():
    ones = lambda n: jnp.ones((n,), F32)
    zeros = lambda n: jnp.zeros((n,), F32)
    gw = NSA_KV * NSA_DH
    rmask = jnp.concatenate([ones(C_DA_Q), ones(C_DA_K), zeros(C_DA_V), ones(C_NSA_Q)]
                            + [ones(gw), zeros(gw)] * 3)
    cscale = jnp.concatenate([ones(C_DA_Q) * DA_DH ** -0.5, ones(C_DA_K), ones(C_DA_V),
                              ones(C_NSA_Q) * NSA_DH ** -0.5, ones(C_NSA_KV)])
    return rmask[None, :], cscale[None, :]


def _block_expand(width, nkeys):
    blk = jnp.arange(width, dtype=I32)[:, None]
    key = jnp.arange(nkeys, dtype=I32)[None, :]
    return (key // BLOCK == blk).astype(BF16)


def _pad_rows(x, axis, size):
    pad = [(0, 0)] * x.ndim
    pad[axis] = (0, size - x.shape[axis])
    return jnp.pad(x, pad)


def _cmp_layout(kc, b, nc, width):
    x = kc.reshape(b, nc, NSA_KV, NSA_DH).transpose(0, 2, 1, 3)
    return _pad_rows(x, 2, width).astype(BF16)


def kernel(x_prompt, x_sample, cache_da_kv, cache_nsa_kv, cache_win_kv, page_table, p_prompt, p_sample,
           g_mix, w_in, da_lambda, da_subln, nsa_cpos, nsa_cmp, w_da_up, w_nsa_up, w_o, g_ffn,
           w_rg, b_rg, w_re, b_re, w_e_gate, w_e_up, w_e_down, g_ple, w_ple_gate, w_ple, g_final):
    b, s, d = x_prompt.shape
    nb, t4, _ = x_sample.shape
    depth = g_mix.shape[0]
    npr, nsm = b * s, nb * t4
    n = npr + nsm
    page = cache_da_kv.shape[2]
    npages = page_table.shape[1]
    past = npages * page
    wb = cache_win_kv.shape[2]
    nchunk = d // LANES
    rq = t4 * DA_REP
    assert rq % 8 == 0 and page % BLOCK == 0 and s % BLOCK == 0 and t4 <= page and d % LANES == 0

    xa = jnp.concatenate([x_prompt.reshape(npr, d), x_sample.reshape(nsm, d)], axis=0)
    pos = jnp.concatenate([jnp.tile(jnp.arange(s, dtype=I32), b), jnp.tile(past + jnp.arange(t4, dtype=I32), nb)])
    cos, sa, sb = _rope_tables(pos)
    rmask, cscale = _column_tables()

    nc_p, ns_p = s // BLOCK, max(-(-s // BLOCK), N_SELECT)
    tot_s = past + t4
    nc_s, ns_s = tot_s // BLOCK, max(-(-tot_s // BLOCK), N_SELECT)
    wid_p, wid_s = _round_up(ns_p, LANES), _round_up(ns_s, LANES)
    emat_p = _block_expand(wid_p, s)
    pg_s, nsteps_s = _page_steps(npages)
    emat_s = _block_expand(wid_s, (nsteps_s + 1) * pg_s * page)

    col = jnp.arange(LANES, dtype=I32)[:, None]
    lane = jnp.arange(NSA_HEADS * NSA_DH, dtype=I32)[None, :]
    gate_expand = jnp.stack([(col == br * NSA_HEADS + lane // NSA_DH) for br in range(3)]).astype(BF16)

    n_assign = 2 * n
    shift = int(math.log2(EXPERT_ROWS))
    nbt = -(-n_assign // EXPERT_ROWS) + N_EXPERTS
    tm_rank = _pick_tile(n, 640, 16)
    tri = jnp.tril(jnp.ones((tm_rank, tm_rank), BF16), -1)

    outs = {k: [] for k in ("da_p", "da_s", "ns_p", "ns_s", "win_p", "win_s")}
    for i in range(depth):
        lam_init = 0.8 - 0.6 * math.exp(-0.3 * i)
        wi = w_in[i]
        g1 = g_mix[i][None, :]
        zf, zb = _proj_rope(xa, g1, wi[:, :S5].astype(BF16), cos, sa, sb, rmask, cscale)
        wg_cols = wi[:, S5:S6].reshape(d, NSA_HEADS, 3).transpose(0, 2, 1).reshape(d, C_NSA_G)
        gns = _gate_proj(xa, g1, _pad_rows(wg_cols, 1, LANES).astype(BF16))
        gmg = _gate_proj(xa, g1, wi[:, S6:].astype(BF16))
        lam_p = da_lambda[i].astype(F32)
        subln = da_subln[i][None, :].astype(F32)
        wp = jnp.tile(nsa_cpos[i].astype(F32), (1, 1, NSA_KV))
        wcm = nsa_cmp[i].astype(F32)
        zc = jnp.zeros_like(wcm)
        wc = jnp.concatenate([jnp.concatenate([wcm, zc], axis=2), jnp.concatenate([zc, wcm], axis=2)], axis=1)

        zp = zb[:npr].reshape(b, s, S5)
        qd = zp[..., :S1].reshape(b, s, DA_KV, DA_REP, 2, DA_DH).transpose(0, 2, 4, 3, 1, 5)
        kd = zp[..., S1:S2].reshape(b, s, DA_KV, 2, DA_DH).transpose(0, 2, 3, 1, 4)
        vd = zp[..., S2:S3].reshape(b, s, DA_KV, DA_DV).transpose(0, 2, 1, 3)
        oda_p = _da_flash(qd, kd, vd, lam_p, subln, lam_init)
        oda_p = oda_p.transpose(0, 3, 1, 2, 4).reshape(npr, DA_HEADS * DA_DV)
        qn = zp[..., S3:S4].reshape(b, s, NSA_HEADS, NSA_DH).transpose(0, 2, 1, 3)
        kvn = zp[..., S4:S5].reshape(b, s, 6, NSA_KV, NSA_DH).transpose(0, 2, 3, 1, 4)
        kc, vc = _compress_prompt(zf, wp, wc, b, s)
        ocmp_p, sel_p = _cmp_select(qn, _cmp_layout(kc, b, nc_p, wid_p), _cmp_layout(vc, b, nc_p, wid_p),
                                    nc=nc_p, ns=ns_p)
        with_ones = lambda v: jnp.concatenate([v, jnp.ones_like(v)], axis=-1)
        oslc_p = _nsa_flash(qn, kvn[:, 2], with_ones(kvn[:, 3]), sel_p, emat_p)
        owin_p = _nsa_flash(qn, kvn[:, 4], with_ones(kvn[:, 5]))
        tok_major = lambda o: o.transpose(0, 2, 1, 3).reshape(npr, NSA_HEADS * NSA_DH)
        ocmp_p, oslc_p, owin_p = tok_major(ocmp_p), tok_major(oslc_p), tok_major(owin_p)

        zs = zb[npr:].reshape(nb, t4, S5)
        zfs = zf[npr:].reshape(nb, t4, S5)
        q6 = zs[..., :S1].reshape(nb, t4, DA_KV, DA_REP, 2, DA_DH).transpose(0, 2, 4, 1, 3, 5)
        eye_g = jnp.eye(DA_KV, dtype=BF16)
        eye_c = jnp.eye(2, dtype=BF16)
        qbd = (q6[:, :, :, :, :, None, None, :] * eye_g[None, :, None, None, None, :, None, None]
               * eye_c[None, None, :, None, None, None, :, None]).reshape(nb, DA_KV * 2 * rq, DA_KV * 2 * DA_DH)
        cda = cache_da_kv[i].reshape(cache_da_kv.shape[1], page * 2 * DA_KV, DA_DV)
        new_da = _pad_rows(zfs[..., S1:S3], 1, page).reshape(nb, page * 2 * DA_KV, DA_DV)
        oda_s = _da_decode(page_table, qbd, cda, new_da, lam_p, subln, lam_init)
        oda_s = oda_s.reshape(nb, DA_KV, t4, DA_REP, DA_DV).transpose(0, 2, 1, 3, 4).reshape(nsm, DA_HEADS * DA_DV)

        qns = zs[..., S3:S4].reshape(nb, t4, NSA_KV, NSA_REP, NSA_DH).transpose(0, 2, 1, 3, 4)
        eye_n = jnp.eye(NSA_KV, dtype=BF16)
        qnb = (qns[:, :, :, :, None, :] * eye_n[None, :, None, None, :, None]
               ).reshape(nb, NSA_KV * t4 * NSA_REP, LANES)
        cns_t = cache_nsa_kv[i].transpose(0, 2, 3, 4, 1)
        cwin_t = cache_win_kv[i].transpose(0, 2, 3, 4, 1)
        kw = NSA_KV * NSA_DH
        wt = jnp.tile(nsa_cpos[i].astype(F32).transpose(0, 2, 1)[:, None],
                      (1, NSA_KV, 1, page // BLOCK)).reshape(2 * kw, page)
        ocmp_s, sel_s = _cmp_decode(page_table, qns.reshape(nb, NSA_KV, t4 * NSA_REP, NSA_DH), cns_t, wt,
                                    nsa_cmp[i].astype(F32).transpose(0, 2, 1),
                                    past=past, t4=t4, nc=nc_s, ns=ns_s, width=wid_s)
        ocmp_s = ocmp_s.reshape(nb, NSA_KV, t4, NSA_REP, NSA_DH).transpose(0, 2, 1, 3, 4
                                                                             ).reshape(nsm, NSA_HEADS * NSA_DH)
        selrows = jnp.broadcast_to(sel_s[:, :, :, None, :], (nb, NSA_KV, t4, NSA_REP, wid_s)
                                   ).reshape(nb, NSA_KV * t4 * NSA_REP, wid_s)
        new_t = lambda x: _pad_rows(x.reshape(nb, t4, 2, NSA_KV, NSA_DH), 1, page).transpose(0, 2, 3, 4, 1)
        oslc_s = _slc_decode(page_table, qnb, selrows, emat_s, cns_t, new_t(zfs[..., S4 + 2 * kw:S4 + 4 * kw]),
                             past=past, t4=t4)
        owin_s = _win_decode(qnb, cwin_t, new_t(zfs[..., S4 + 4 * kw:S5]), past=past, t4=t4)

        def halves(o):
            o = o.reshape(nb, NSA_KV, t4, NSA_REP, NSA_KV, NSA_DH)
            o = jnp.stack([o[:, g, :, :, g] for g in range(NSA_KV)], axis=2)
            return o.reshape(nsm, NSA_HEADS * NSA_DH)

        oslc_s, owin_s = halves(oslc_s), halves(owin_s)

        oda = jnp.concatenate([oda_p, oda_s.astype(BF16)], axis=0)
        ocmp = jnp.concatenate([ocmp_p, ocmp_s], axis=0)
        oslc = jnp.concatenate([oslc_p, oslc_s], axis=0)
        owin = jnp.concatenate([owin_p, owin_s], axis=0)
        mrg = _merge(oda, ocmp, oslc, owin, gns, gmg, gate_expand,
                     w_da_up[i].astype(BF16), w_nsa_up[i].astype(BF16))
        wr = _pad_rows(jnp.concatenate([w_rg[i], w_re[i]], axis=1).astype(F32), 1, LANES)
        br = _pad_rows(jnp.concatenate([b_rg[i], b_re[i]]).astype(F32)[None, :], 1, LANES)
        x1, hrows, route = _wo_route(mrg, xa, w_o[i].astype(BF16), g_ffn[i][None, :], wr, br)
        rank, sizes = _expert_ranks(route, tri)
        dest, be, nu = _destinations(route, rank, sizes, nbt, shift)
        dest_flat = dest[:, :2].reshape(n_assign)
        ys = _experts(dest_flat, be[:nbt, 0], nu[0, :1], hrows.reshape(n, nchunk, LANES),
                      w_e_gate[i], w_e_up[i], w_e_down[i], EXPERT_ROWS)
        pa = jnp.concatenate([p_prompt[i].reshape(npr, -1), p_sample[i].reshape(nsm, -1)], axis=0)
        xa = _combine(dest_flat, ys.reshape(nbt * EXPERT_ROWS, nchunk, LANES), route, x1, pa,
                      g_ple[i][None, :], w_ple_gate[i].astype(BF16), w_ple[i].astype(BF16),
                      g_final[None, :], i == depth - 1)

        zfp = zf[:npr].reshape(b, s, S5)
        outs["da_p"].append(zfp[..., S1:S3].reshape(b, s, 2, DA_KV, DA_DV))
        outs["da_s"].append(zfs[..., S1:S3].reshape(nb, t4, 2, DA_KV, DA_DV))
        outs["ns_p"].append(zfp[..., S4:S4 + 4 * kw].reshape(b, s, 4, NSA_KV, NSA_DH))
        outs["ns_s"].append(zfs[..., S4:S4 + 4 * kw].reshape(nb, t4, 4, NSA_KV, NSA_DH))
        win_p = zfp[..., S4 + 4 * kw:S5].reshape(b, s, 2, NSA_KV, NSA_DH)
        outs["win_p"].append(win_p[:, s - min(WINDOW, s):])
        win_s = jnp.concatenate([cache_win_kv[i], zfs[..., S4 + 4 * kw:S5].reshape(nb, t4, 2, NSA_KV, NSA_DH)], axis=1)
        outs["win_s"].append(win_s[:, win_s.shape[1] - wb:])

    y_prompt = xa[:npr].reshape(b, s, d)
    y_sample = xa[npr:].reshape(nb, t4, d)
    return (y_prompt, y_sample, jnp.stack(outs["da_p"]), jnp.stack(outs["da_s"]), jnp.stack(outs["ns_p"]),
            jnp.stack(outs["ns_s"]), jnp.stack(outs["win_p"]), jnp.stack(outs["win_s"]))
```

```python
import functools
import math

import jax
import jax.numpy as jnp
from jax import lax
from jax.experimental import pallas as pl
from jax.experimental.pallas import tpu as pltpu

F32 = jnp.float32
BF16 = jnp.bfloat16
I32 = jnp.int32
HIGHEST = lax.Precision.HIGHEST

DA_HEADS, DA_KV, DA_REP, DA_DH = 8, 4, 2, 64
DA_DV = 2 * DA_DH
NSA_HEADS, NSA_KV, NSA_REP, NSA_DH = 16, 2, 8, 64
BLOCK = 64
BLOCK_SHIFT = 6
N_SELECT = 16
WINDOW = 512
FORCE_BONUS = 1000.0
ROPE_THETA = 500000.0
ROPE_HALF = 8
N_GROUPS, EXP_PER_GROUP = 8, 8
N_EXPERTS = N_GROUPS * EXP_PER_GROUP
EPS = 1e-6
NEG = -1e30
LOWEST = -3e38

LANES = 128
VMEM_LIMIT_BYTES = 56 * 1024 * 1024

C_DA_Q = DA_HEADS * 2 * DA_DH
C_DA_K = DA_KV * 2 * DA_DH
C_DA_V = DA_KV * DA_DV
C_NSA_Q = NSA_HEADS * NSA_DH
C_NSA_KV = 6 * NSA_KV * NSA_DH
C_NSA_G = 3 * NSA_HEADS
S1 = C_DA_Q
S2 = S1 + C_DA_K
S3 = S2 + C_DA_V
S4 = S3 + C_NSA_Q
S5 = S4 + C_NSA_KV
S6 = S5 + C_NSA_G

EXPERT_ROWS = 128
DECODE_PAGES = 8
FLASH_TILE = 512
FLASH_ROWS = 128
FLASH_KEYS = 256


def _cparams(sem):
    return pltpu.CompilerParams(dimension_semantics=sem, vmem_limit_bytes=VMEM_LIMIT_BYTES)


def _pick_tile(n, target, mult):
    best = None
    for t in range(mult, min(n, target) + 1, mult):
        if n % t == 0:
            best = t
    return n if best is None else best


def _round_up(x, m):
    return (x + m - 1) // m * m


def _nt(a, b):
    return lax.dot_general(a, b, (((1,), (1,)), ((), ())), preferred_element_type=F32)


def _dot(a, b, **kw):
    return jnp.dot(a, b, preferred_element_type=F32, **kw)


def _rms(x, g):
    ms = jnp.mean(x * x, axis=-1, keepdims=True)
    return x * lax.rsqrt(ms + EPS) * g


def _proj_rope_kernel(x_ref, g_ref, w_ref, cos_ref, sa_ref, sb_ref, rm_ref, cs_ref,
                      of_ref, ob_ref, h_scr, *, n_chunks):
    @pl.when(pl.program_id(1) == 0)
    def _():
        h_scr[...] = _rms(x_ref[...], g_ref[...]).astype(BF16)

    z = _dot(h_scr[...], w_ref[...])
    c, sa, sb = cos_ref[...], sa_ref[...], sb_ref[...]
    for k in range(n_chunks):
        sl = slice(k * LANES, (k + 1) * LANES)
        zc = z[:, sl]
        rot = zc * c + pltpu.roll(zc, LANES - ROPE_HALF, 1) * sa + pltpu.roll(zc, ROPE_HALF, 1) * sb
        out = jnp.where(rm_ref[:, sl] > 0.5, rot, zc) * cs_ref[:, sl]
        of_ref[:, sl] = out
        ob_ref[:, sl] = out.astype(BF16)


def _proj_rope(xa, g, w, cos, sa, sb, rmask, cscale):
    n, d = xa.shape
    ca = w.shape[1]
    tm = _pick_tile(n, 640, 16)
    tn = _pick_tile(ca, 1024, LANES)
    row = lambda i, j: (i, 0)
    col = lambda i, j: (0, j)
    return pl.pallas_call(
        functools.partial(_proj_rope_kernel, n_chunks=tn // LANES),
        out_shape=(jax.ShapeDtypeStruct((n, ca), F32), jax.ShapeDtypeStruct((n, ca), BF16)),
        grid=(n // tm, ca // tn),
        in_specs=[pl.BlockSpec((tm, d), row), pl.BlockSpec((1, d), lambda i, j: (0, 0)),
                  pl.BlockSpec((d, tn), col),
                  pl.BlockSpec((tm, LANES), row), pl.BlockSpec((tm, LANES), row),
                  pl.BlockSpec((tm, LANES), row),
                  pl.BlockSpec((1, tn), col), pl.BlockSpec((1, tn), col)],
        out_specs=(pl.BlockSpec((tm, tn), lambda i, j: (i, j)),
                   pl.BlockSpec((tm, tn), lambda i, j: (i, j))),
        scratch_shapes=[pltpu.VMEM((tm, d), BF16)],
        compiler_params=_cparams(("parallel", "arbitrary")),
        name="proj_rope",
    )(xa, g, w, cos, sa, sb, rmask, cscale)


def _gate_kernel(x_ref, g_ref, w_ref, o_ref, h_scr):
    @pl.when(pl.program_id(1) == 0)
    def _():
        h_scr[...] = _rms(x_ref[...], g_ref[...]).astype(BF16)

    o_ref[...] = jax.nn.sigmoid(_dot(h_scr[...], w_ref[...]))


def _gate_proj(xa, g, w):
    n, d = xa.shape
    ca = w.shape[1]
    tm = _pick_tile(n, 640, 16)
    tn = _pick_tile(ca, 1024, LANES)
    return pl.pallas_call(
        _gate_kernel,
        out_shape=jax.ShapeDtypeStruct((n, ca), F32),
        grid=(n // tm, ca // tn),
        in_specs=[pl.BlockSpec((tm, d), lambda i, j: (i, 0)), pl.BlockSpec((1, d), lambda i, j: (0, 0)),
                  pl.BlockSpec((d, tn), lambda i, j: (0, j))],
        out_specs=pl.BlockSpec((tm, tn), lambda i, j: (i, j)),
        scratch_shapes=[pltpu.VMEM((tm, d), BF16)],
        compiler_params=_cparams(("parallel", "arbitrary")),
        name="gate_proj",
    )(xa, g, w)


def _softmax_block(s, mask, v, m_old, l_old, acc_old, v_is_transposed=False):
    if mask is not None:
        s = jnp.where(mask, s, NEG)
    m_new = jnp.maximum(m_old, jnp.max(s, axis=-1, keepdims=True))
    p = jnp.exp(s - m_new)
    if mask is not None:
        p = jnp.where(mask, p, 0.0)
    alpha = jnp.exp(m_old - m_new)
    l_new = alpha * l_old + jnp.sum(p, axis=-1, keepdims=True)
    pv = _nt(p.astype(v.dtype), v) if v_is_transposed else _dot(p.astype(v.dtype), v)
    return m_new, l_new, alpha * acc_old + pv


def _flash_update(s, mask, v, m_ref, l_ref, acc_ref, idx, v_is_transposed=False):
    m, l, acc = _softmax_block(s, mask, v, m_ref[idx], l_ref[idx], acc_ref[idx], v_is_transposed)
    m_ref[idx] = m
    l_ref[idx] = l
    acc_ref[idx] = acc


def _flash_init(m_ref, l_ref, acc_ref):
    m_ref[...] = jnp.full(m_ref.shape, NEG, F32)
    l_ref[...] = jnp.zeros(l_ref.shape, F32)
    acc_ref[...] = jnp.zeros(acc_ref.shape, F32)


def _da_lambda(lam_ref, lam_init):
    a = jnp.sum(lam_ref[0:1, :] * lam_ref[1:2, :], axis=-1, keepdims=True)
    b = jnp.sum(lam_ref[2:3, :] * lam_ref[3:4, :], axis=-1, keepdims=True)
    return jnp.exp(a) - jnp.exp(b) + lam_init


def _da_finish(a0, a1, lam, sub, lam_init):
    o = a0 - lam * a1
    return _rms(o, sub) * (1.0 - lam_init)


def _flash_rows(s, mask, v1, m_ref, acc_ref, idx):
    if mask is not None:
        s = jnp.where(mask, s, NEG)
    m_old = m_ref[idx]
    m_new = jnp.maximum(m_old, jnp.max(s, axis=-1, keepdims=True))
    p = jnp.exp(s - jnp.tile(m_new, (1, s.shape[1] // LANES)))
    if mask is not None:
        p = jnp.where(mask, p, 0.0)
    alpha = jnp.exp(m_old - m_new)
    acc_ref[idx] = jnp.tile(alpha, (1, v1.shape[1] // LANES)) * acc_ref[idx] + _dot(p.astype(v1.dtype), v1)
    m_ref[idx] = m_new


def _tri_tables(nq):
    qi = [q for q in range(nq) for _ in range(q + 1)]
    ki = [k for q in range(nq) for k in range(q + 1)]
    return jnp.asarray(qi, I32), jnp.asarray(ki, I32)


def _da_flash_kernel(qi_ref, ki_ref, lam_ref, sub_ref, q_ref, k_ref, v_ref, o_ref, m_scr, acc_scr,
                     *, tq, rb, kb, lam_init):
    t = pl.program_id(2)
    qi = qi_ref[t]
    ki = ki_ref[t]

    @pl.when(ki == 0)
    def _():
        m_scr[...] = jnp.full(m_scr.shape, NEG, F32)
        acc_scr[...] = jnp.zeros(acc_scr.shape, F32)

    upper = lax.broadcasted_iota(I32, (1, LANES), 1) >= DA_DH

    def step(diag):
        k = k_ref[...]
        v1 = jnp.concatenate([v_ref[...], jnp.ones((tq, DA_DV), BF16)], axis=1)
        for r in range(DA_REP):
            for r0 in range(0, tq, rb):
                nk = r0 + rb if diag else tq
                q = q_ref[pl.ds(r0, rb), r * DA_DV:(r + 1) * DA_DV]
                qc = [jnp.where(upper if c == 1 else jnp.logical_not(upper), q, jnp.zeros_like(q))
                      for c in range(2)]
                for c0 in range(0, nk, kb):
                    c1 = min(c0 + kb, nk)
                    mask = None
                    if diag and c1 > r0:
                        row = r0 - c0 + lax.broadcasted_iota(I32, (rb, c1 - c0), 0)
                        mask = lax.broadcasted_iota(I32, (rb, c1 - c0), 1) <= row
                    for c in range(2):
                        _flash_rows(_nt(qc[c], k[c0:c1]), mask, v1[c0:c1], m_scr, acc_scr,
                                    (c, pl.ds(r * tq + r0, rb)))

    @pl.when(ki < qi)
    def _():
        step(False)

    @pl.when(ki == qi)
    def _():
        step(True)
        lam = _da_lambda(lam_ref, lam_init)
        a0 = acc_scr[0, :, :DA_DV] / acc_scr[0, :, DA_DV:]
        a1 = acc_scr[1, :, :DA_DV] / acc_scr[1, :, DA_DV:]
        o = _da_finish(a0, a1, lam, sub_ref[...], lam_init)
        for r in range(DA_REP):
            o_ref[:, r * DA_DV:(r + 1) * DA_DV] = o[r * tq:(r + 1) * tq].astype(o_ref.dtype)


def _da_flash(zb, n_out, b, s, lam_p, subln, lam_init):
    tq = _pick_tile(s, FLASH_TILE, 16)
    rb = _pick_tile(tq, FLASH_ROWS, 16)
    kb = _pick_tile(tq, FLASH_KEYS, LANES)
    nq = s // tq
    qtab, ktab = _tri_tables(nq)
    kcol, vcol = S1 // DA_DV, S2 // DA_DV
    grid_spec = pltpu.PrefetchScalarGridSpec(
        num_scalar_prefetch=2, grid=(b, DA_KV, qtab.shape[0]),
        in_specs=[pl.BlockSpec((4, DA_DH), lambda bi, g, t, qt, kt: (0, 0)),
                  pl.BlockSpec((1, DA_DV), lambda bi, g, t, qt, kt: (0, 0)),
                  pl.BlockSpec((tq, DA_REP * DA_DV), lambda bi, g, t, qt, kt: (bi * nq + qt[t], g)),
                  pl.BlockSpec((tq, DA_DV), lambda bi, g, t, qt, kt: (bi * nq + kt[t], kcol + g)),
                  pl.BlockSpec((tq, DA_DV), lambda bi, g, t, qt, kt: (bi * nq + kt[t], vcol + g))],
        out_specs=pl.BlockSpec((tq, DA_REP * DA_DV), lambda bi, g, t, qt, kt: (bi * nq + qt[t], g)),
        scratch_shapes=[pltpu.VMEM((2, DA_REP * tq, LANES), F32),
                        pltpu.VMEM((2, DA_REP * tq, 2 * DA_DV), F32)])
    return pl.pallas_call(
        functools.partial(_da_flash_kernel, tq=tq, rb=rb, kb=kb, lam_init=lam_init),
        out_shape=jax.ShapeDtypeStruct((n_out, DA_HEADS * DA_DV), BF16),
        grid_spec=grid_spec,
        compiler_params=_cparams(("parallel", "parallel", "arbitrary")),
        name="da_flash",
    )(qtab, ktab, lam_p, subln, zb, zb, zb)


def _page_steps(npages):
    pg = _pick_tile(npages, DECODE_PAGES, 1)
    return pg, npages // pg


def _page_specs(block, npages, pg, col_block):
    nd = len(block) - 1

    def spec(u):
        def index(b, p, pt):
            idx = [pt[b, jnp.minimum(p * pg + u, npages - 1)]] + [0] * nd
            if col_block is not None:
                idx[1] = col_block
            return tuple(idx)
        return pl.BlockSpec(block, index)

    return [spec(u) for u in range(pg)]


def _da_decode_kernel(pt_ref, lam_ref, sub_ref, q_ref, *refs, nsteps, pg, page, rq, lam_init):
    del pt_ref
    c_refs = refs[:pg]
    n_ref, o_ref, m_scr, l_scr, acc_scr = refs[pg:]
    p = pl.program_id(1)
    per = 2 * DA_KV

    @pl.when(p == 0)
    def _():
        _flash_init(m_scr, l_scr, acc_scr)

    q = q_ref[0]
    rows = q.shape[0]

    def heads(ref, first):
        return jnp.concatenate([ref[0, pl.ds(first + g, page, stride=per), :] for g in range(DA_KV)],
                               axis=1).astype(BF16)

    def step(blocks, mask):
        k = jnp.concatenate([heads(r, 0) for r in blocks], axis=0)
        v = jnp.concatenate([heads(r, DA_KV) for r in blocks], axis=0)
        _flash_update(_nt(q, k), mask, v, m_scr, l_scr, acc_scr, 0)

    @pl.when(p < nsteps)
    def _():
        step(c_refs, None)

    @pl.when(p == nsteps)
    def _():
        row = lax.broadcasted_iota(I32, (rows, page), 0)
        col = lax.broadcasted_iota(I32, (rows, page), 1)
        step([n_ref], col <= (row % rq) // DA_REP)
        lam = _da_lambda(lam_ref, lam_init)
        a = acc_scr[0] / l_scr[0]
        for g in range(DA_KV):
            cols = slice(g * DA_DV, (g + 1) * DA_DV)
            a0 = a[g * 2 * rq:g * 2 * rq + rq, cols]
            a1 = a[g * 2 * rq + rq:(g + 1) * 2 * rq, cols]
            o_ref[0, g] = _da_finish(a0, a1, lam, sub_ref[...], lam_init)


def _da_decode(page_table, qbd, cache, newkv, lam_p, subln, lam_init):
    nb, rows, ck = qbd.shape
    npages = page_table.shape[1]
    prow = cache.shape[1]
    page = prow // (2 * DA_KV)
    rq = rows // (DA_KV * 2)
    pg, nsteps = _page_steps(npages)
    grid_spec = pltpu.PrefetchScalarGridSpec(
        num_scalar_prefetch=1, grid=(nb, nsteps + 1),
        in_specs=[pl.BlockSpec((4, DA_DH), lambda b, p, pt: (0, 0)),
                  pl.BlockSpec((1, DA_DV), lambda b, p, pt: (0, 0)),
                  pl.BlockSpec((1, rows, ck), lambda b, p, pt: (b, 0, 0))]
        + _page_specs((1, prow, DA_DV), npages, pg, None)
        + [pl.BlockSpec((1, prow, DA_DV), lambda b, p, pt: (b, 0, 0))],
        out_specs=pl.BlockSpec((1, DA_KV, rq, DA_DV), lambda b, p, pt: (b, 0, 0, 0)),
        scratch_shapes=[pltpu.VMEM((1, rows, 1), F32), pltpu.VMEM((1, rows, 1), F32),
                        pltpu.VMEM((1, rows, DA_KV * DA_DV), F32)])
    return pl.pallas_call(
        functools.partial(_da_decode_kernel, nsteps=nsteps, pg=pg, page=page, rq=rq, lam_init=lam_init),
        out_shape=jax.ShapeDtypeStruct((nb, DA_KV, rq, DA_DV), F32),
        grid_spec=grid_spec,
        compiler_params=_cparams(("parallel", "arbitrary")),
        name="da_decode",
    )(page_table, lam_p, subln, qbd, *([cache] * pg), newkv)


def _compress_kernel(k_ref, v_ref, wp_ref, wc_ref, ko_ref, vo_ref):
    for c, (src, dst) in enumerate(((k_ref, ko_ref), (v_ref, vo_ref))):
        x = src[...]
        t = x.shape[0]
        y = jnp.sum(x.reshape(t // BLOCK, BLOCK, LANES) * wp_ref[c][None], axis=1)
        dst[...] = _dot(y, wc_ref[c], precision=HIGHEST)


def _compress_prompt(zf, wp, wc, b, s):
    nc = s // BLOCK
    kc0 = S4 // LANES
    out = jax.ShapeDtypeStruct((b * nc, LANES), F32)
    return pl.pallas_call(
        _compress_kernel,
        out_shape=(out, out),
        grid=(b,),
        in_specs=[pl.BlockSpec((s, LANES), lambda i: (i, kc0)), pl.BlockSpec((s, LANES), lambda i: (i, kc0 + 1)),
                  pl.BlockSpec((2, BLOCK, LANES), lambda i: (0, 0, 0)),
                  pl.BlockSpec((2, LANES, LANES), lambda i: (0, 0, 0))],
        out_specs=(pl.BlockSpec((nc, LANES), lambda i: (i, 0)), pl.BlockSpec((nc, LANES), lambda i: (i, 0))),
        compiler_params=_cparams(("parallel",)),
        name="nsa_compress",
    )(zf, zf, wp, wc)


def _masked_probs(s, ok):
    s = jnp.where(ok, s, NEG)
    e = jnp.where(ok, jnp.exp(s - jnp.max(s, axis=-1, keepdims=True)), 0.0)
    den = jnp.sum(e, axis=-1, keepdims=True)
    return e / jnp.where(den > 0.0, den, 1.0)


def _select_blocks(imp, pos, ns):
    width = imp.shape[1]
    blk = lax.broadcasted_iota(I32, (1, width), 1)
    cur = lax.shift_right_logical(pos, BLOCK_SHIFT)
    forced = (blk == 0) | (blk == cur) | (blk == cur - 1)
    score = jnp.where(forced, imp + FORCE_BONUS, imp)
    score = jnp.where(blk <= cur, score, -1.0)
    score = jnp.where(blk < ns, score, 0.5 * LOWEST)
    sel = jnp.zeros(imp.shape, F32)
    for _ in range(N_SELECT):
        mx = jnp.max(score, axis=-1, keepdims=True)
        idx = jnp.min(jnp.where(score == mx, blk, width), axis=-1, keepdims=True)
        hit = blk == idx
        sel = jnp.where(hit, 1.0, sel)
        score = jnp.where(hit, LOWEST, score)
    return sel


def _half_mask(half):
    upper = lax.broadcasted_iota(I32, (1, LANES), 1) >= NSA_DH
    return upper if half == 1 else jnp.logical_not(upper)


def _head_q(q_ref, rows, r):
    q = q_ref[rows, (r // 2) * LANES:(r // 2 + 1) * LANES]
    return jnp.where(_half_mask(r % 2), q, jnp.zeros_like(q))


def _cmp_select_kernel(q_ref, kc_ref, vc_ref, o_ref, sel_ref, *, tq, nc, ns, width):
    qi = pl.program_id(1)
    pos = qi * tq + lax.broadcasted_iota(I32, (tq, 1), 0)
    blk = lax.broadcasted_iota(I32, (1, width), 1)
    cmp_ok = (blk < nc) & ((blk + 1) * BLOCK - 1 <= pos)
    hw = NSA_REP * NSA_DH
    for g in range(NSA_KV):
        kc = kc_ref[0, g]
        imp = jnp.zeros((tq, width), F32)
        for j in range(NSA_REP // 2):
            out = None
            for a in range(2):
                q = q_ref[:, g * hw + j * LANES:g * hw + (j + 1) * LANES]
                q = jnp.where(_half_mask(a), q, jnp.zeros_like(q))
                p = _masked_probs(_nt(q, kc), cmp_ok)
                imp = imp + p
                term = _dot(p.astype(BF16), vc_ref[0, g, a])
                out = term if out is None else out + term
            o_ref[:, g * hw + j * LANES:g * hw + (j + 1) * LANES] = out
        sel_ref[0, g] = _select_blocks(imp, pos, ns).astype(sel_ref.dtype)


def _cmp_decode_kernel(pt_ref, q_ref, wt_ref, wct_ref, *refs, nsteps, pg, page, past, t4, nc, ns, width):
    del pt_ref
    c_refs = refs[:pg]
    o_ref, sel_ref, acc = refs[pg:]
    p = pl.program_id(1)
    kw = NSA_KV * NSA_DH
    cw = acc.shape[1]

    @pl.when(p == 0)
    def _():
        acc[...] = jnp.zeros(acc.shape, F32)

    z = jnp.concatenate([c_refs[u][0].reshape(2 * kw, page) * wt_ref[...] for u in range(pg)], axis=1)
    tokblk = lax.shift_right_logical(lax.broadcasted_iota(I32, (pg * page, cw), 0), BLOCK_SHIFT)
    lane = lax.broadcasted_iota(I32, (pg * page, cw), 1)
    place = jnp.where(lane == p * (pg * page // BLOCK) + tokblk, 1.0, 0.0).astype(BF16)
    total = None
    for _ in range(3):
        piece = z.astype(BF16)
        z = z - piece.astype(F32)
        term = _dot(piece, place)
        total = term if total is None else total + term
    acc[...] += total

    @pl.when(p == nsteps - 1)
    def _():
        y = acc[...]
        rows = t4 * NSA_REP
        pos = past + lax.broadcasted_iota(I32, (rows, 1), 0) // NSA_REP
        blk = lax.broadcasted_iota(I32, (1, cw), 1)
        cmp_ok = (blk < nc) & ((blk + 1) * BLOCK - 1 <= pos)
        for g in range(NSA_KV):
            kct = _dot(wct_ref[0], y[g * NSA_DH:(g + 1) * NSA_DH], precision=HIGHEST)
            vct = _dot(wct_ref[1], y[kw + g * NSA_DH:kw + (g + 1) * NSA_DH], precision=HIGHEST)
            pr = _masked_probs(_dot(q_ref[0, g], kct.astype(BF16)), cmp_ok)
            o_ref[0, g] = _nt(pr.astype(BF16), vct.astype(BF16))
            imp = jnp.sum(pr.reshape(t4, NSA_REP, cw), axis=1)
            if width > cw:
                imp = jnp.concatenate([imp, jnp.zeros((t4, width - cw), F32)], axis=1)
            tpos = past + lax.broadcasted_iota(I32, (t4, 1), 0)
            sel_ref[0, g] = _select_blocks(imp, tpos, ns).astype(sel_ref.dtype)


def _cmp_decode(page_table, qst, cache_t, wt, wct, *, past, t4, nc, ns, width):
    nb, npages = page_table.shape
    page = cache_t.shape[4]
    pg, nsteps = _page_steps(npages)
    rows = t4 * NSA_REP
    cw = _round_up(nc, LANES)
    kw = NSA_KV * NSA_DH
    grid_spec = pltpu.PrefetchScalarGridSpec(
        num_scalar_prefetch=1, grid=(nb, nsteps),
        in_specs=[pl.BlockSpec((1, NSA_KV, rows, NSA_DH), lambda b, p, pt: (b, 0, 0, 0)),
                  pl.BlockSpec((2 * kw, page), lambda b, p, pt: (0, 0)),
                  pl.BlockSpec((2, NSA_DH, NSA_DH), lambda b, p, pt: (0, 0, 0))]
        + _page_specs((1, 2, NSA_KV, NSA_DH, page), npages, pg, 0),
        out_specs=(pl.BlockSpec((1, NSA_KV, rows, NSA_DH), lambda b, p, pt: (b, 0, 0, 0)),
                   pl.BlockSpec((1, NSA_KV, t4, width), lambda b, p, pt: (b, 0, 0, 0))),
        scratch_shapes=[pltpu.VMEM((2 * kw, cw), F32)])
    return pl.pallas_call(
        functools.partial(_cmp_decode_kernel, nsteps=nsteps, pg=pg, page=page, past=past, t4=t4,
                          nc=nc, ns=ns, width=width),
        out_shape=(jax.ShapeDtypeStruct((nb, NSA_KV, rows, NSA_DH), F32),
                   jax.ShapeDtypeStruct((nb, NSA_KV, t4, width), BF16)),
        grid_spec=grid_spec,
        compiler_params=_cparams(("parallel", "arbitrary")),
        name="nsa_cmp_decode",
    )(page_table, qst, wt, wct, *([cache_t] * pg))


def _cmp_select(zb, kc2, vc2, n_out, b, s, *, nc, ns):
    width = kc2.shape[2]
    tq = _pick_tile(s, 256, 16)
    nq = s // tq
    qcol = S3 // C_NSA_Q
    return pl.pallas_call(
        functools.partial(_cmp_select_kernel, tq=tq, nc=nc, ns=ns, width=width),
        out_shape=(jax.ShapeDtypeStruct((n_out, C_NSA_Q), F32),
                   jax.ShapeDtypeStruct((b, NSA_KV, s, width), BF16)),
        grid=(b, nq),
        in_specs=[pl.BlockSpec((tq, C_NSA_Q), lambda bi, qi: (bi * nq + qi, qcol)),
                  pl.BlockSpec((1, NSA_KV, width, LANES), lambda bi, qi: (bi, 0, 0, 0)),
                  pl.BlockSpec((1, NSA_KV, 2, width, LANES), lambda bi, qi: (bi, 0, 0, 0, 0))],
        out_specs=(pl.BlockSpec((tq, C_NSA_Q), lambda bi, qi: (bi * nq + qi, 0)),
                   pl.BlockSpec((1, NSA_KV, tq, width), lambda bi, qi: (bi, 0, qi, 0))),
        compiler_params=_cparams(("parallel", "parallel")),
        name="nsa_cmp_select",
    )(zb, kc2, vc2)


def _nsa_flash_kernel(qi_ref, ki_ref, *refs, mode, tq, rb, kb, nsteps):
    if mode == "slc":
        q_ref, k_ref, v_ref, sel_ref, e_ref, o_ref, m_scr, acc_scr = refs
    else:
        q_ref, k_ref, v_ref, o_ref, m_scr, acc_scr = refs
    t = pl.program_id(2)
    qi = qi_ref[t]
    ki = ki_ref[t]
    first = (t == 0) | (qi_ref[jnp.maximum(t - 1, 0)] != qi)
    last = (t == nsteps - 1) | (qi_ref[jnp.minimum(t + 1, nsteps - 1)] != qi)

    @pl.when(first)
    def _():
        m_scr[...] = jnp.full(m_scr.shape, NEG, F32)
        acc_scr[...] = jnp.zeros(acc_scr.shape, F32)

    if mode == "slc":
        picked = _dot(sel_ref[0, 0], e_ref[...])
    k = k_ref[...]
    v1 = v_ref[...]
    diag = qi == ki
    for r0 in range(0, tq, rb):
        rows = pl.ds(r0, rb)
        nk = r0 + rb

        def update(ncols, r0=r0, rows=rows):
            for c0 in range(0, ncols, kb):
                c1 = min(c0 + kb, ncols)
                d = ((qi - ki) * tq + r0 - c0 + lax.broadcasted_iota(I32, (rb, c1 - c0), 0)
                     - lax.broadcasted_iota(I32, (rb, c1 - c0), 1))
                if mode == "slc":
                    mask = (picked[r0:r0 + rb, c0:c1] > 0.5) & (d >= 0)
                else:
                    mask = (d >= 0) & (d < WINDOW)
                for h in range(NSA_REP):
                    s = _nt(_head_q(q_ref, rows, h), k[c0:c1])
                    _flash_rows(s, mask, v1[c0:c1], m_scr, acc_scr, (h, rows))

        if nk < tq:
            pl.when(diag)(functools.partial(update, nk))
            pl.when(jnp.logical_not(diag))(functools.partial(update, tq))
        else:
            update(tq)

    @pl.when(last)
    def _():
        low = _half_mask(0)
        for j in range(NSA_REP // 2):
            a, bq = acc_scr[2 * j], acc_scr[2 * j + 1]
            num = jnp.where(low, a, pltpu.roll(bq, NSA_DH, 1))
            den = jnp.where(low, pltpu.roll(a, NSA_DH, 1), bq)
            o_ref[:, j * LANES:(j + 1) * LANES] = num / den


def _nsa_flash(zb, k2, v1, n_out, b, s, sel=None, emat=None):
    tq = _pick_tile(s, FLASH_TILE, 16)
    rb = _pick_tile(tq, FLASH_ROWS, 16)
    kb = _pick_tile(tq, FLASH_KEYS, LANES)
    nq = s // tq
    if sel is not None:
        mode = "slc"
        qtab, ktab = _tri_tables(nq)
    else:
        mode = "win"
        back = -(-WINDOW // tq)
        pairs = [(q, kk) for q in range(nq) for kk in range(max(q - back, 0), q + 1)]
        qtab = jnp.asarray([p[0] for p in pairs], I32)
        ktab = jnp.asarray([p[1] for p in pairs], I32)
    nsteps = qtab.shape[0]
    gw = NSA_REP * NSA_DH
    qcol = S3 // gw
    in_specs = [pl.BlockSpec((tq, gw), lambda bi, g, t, qt, kt: (bi * nq + qt[t], qcol + g)),
                pl.BlockSpec((tq, LANES), lambda bi, g, t, qt, kt: (bi * nq + kt[t], g)),
                pl.BlockSpec((tq, LANES), lambda bi, g, t, qt, kt: (bi * nq + kt[t], g))]
    args = [zb, k2, v1]
    if sel is not None:
        width = sel.shape[3]
        in_specs += [pl.BlockSpec((1, 1, tq, width), lambda bi, g, t, qt, kt: (bi, g, qt[t], 0)),
                     pl.BlockSpec((width, tq), lambda bi, g, t, qt, kt: (0, kt[t]))]
        args += [sel, emat]
    grid_spec = pltpu.PrefetchScalarGridSpec(
        num_scalar_prefetch=2, grid=(b, NSA_KV, nsteps),
        in_specs=in_specs,
        out_specs=pl.BlockSpec((tq, gw), lambda bi, g, t, qt, kt: (bi * nq + qt[t], g)),
        scratch_shapes=[pltpu.VMEM((NSA_REP, tq, LANES), F32), pltpu.VMEM((NSA_REP, tq, LANES), F32)])
    return pl.pallas_call(
        functools.partial(_nsa_flash_kernel, mode=mode, tq=tq, rb=rb, kb=kb, nsteps=nsteps),
        out_shape=jax.ShapeDtypeStruct((n_out, C_NSA_Q), F32),
        grid_spec=grid_spec,
        compiler_params=_cparams(("parallel", "parallel", "arbitrary")),
        name="nsa_" + mode,
    )(qtab, ktab, *args)


def _row_token(rows, cols, t4):
    row = lax.broadcasted_iota(I32, (rows, cols), 0)
    return (row // NSA_REP) % t4


def _kv_t(blocks):
    kw = NSA_KV * NSA_DH
    kt = [r[0, 0].reshape(kw, r.shape[4]) for r in blocks]
    vt = [r[0, 1].reshape(kw, r.shape[4]) for r in blocks]
    cat = lambda xs: (xs[0] if len(xs) == 1 else jnp.concatenate(xs, axis=1)).astype(BF16)
    return cat(kt), cat(vt)


def _slc_decode_kernel(pt_ref, q_ref, sel_ref, e_ref, *refs, nsteps, pg, page, past, t4):
    del pt_ref
    c_refs = refs[:pg]
    n_ref, o_ref, m_scr, l_scr, acc_scr = refs[pg:]
    p = pl.program_id(1)

    @pl.when(p == 0)
    def _():
        _flash_init(m_scr, l_scr, acc_scr)

    q = q_ref[0]
    rows = q.shape[0]

    def step(blocks):
        kt, vt = _kv_t(blocks)
        nk = kt.shape[1]
        kpos = p * (pg * page) + lax.broadcasted_iota(I32, (rows, nk), 1)
        picked = _dot(sel_ref[0], e_ref[:, :nk]) > 0.5
        mask = picked & (kpos <= past + _row_token(rows, nk, t4))
        _flash_update(_dot(q, kt), mask, vt, m_scr, l_scr, acc_scr, 0, v_is_transposed=True)

    @pl.when(p < nsteps)
    def _():
        step(c_refs)

    @pl.when(p == nsteps)
    def _():
        step([n_ref])
        o_ref[0] = acc_scr[0] / l_scr[0]


def _slc_decode(page_table, qnb, selrows, emat, cache_t, new_t, *, past, t4):
    nb, rows, _ = qnb.shape
    npages = page_table.shape[1]
    page = cache_t.shape[4]
    width = selrows.shape[2]
    pg, nsteps = _page_steps(npages)
    blk = (1, 2, NSA_KV, NSA_DH, page)
    grid_spec = pltpu.PrefetchScalarGridSpec(
        num_scalar_prefetch=1, grid=(nb, nsteps + 1),
        in_specs=[pl.BlockSpec((1, rows, LANES), lambda b, p, pt: (b, 0, 0)),
                  pl.BlockSpec((1, rows, width), lambda b, p, pt: (b, 0, 0)),
                  pl.BlockSpec((width, pg * page), lambda b, p, pt: (0, p))]
        + _page_specs(blk, npages, pg, 1)
        + [pl.BlockSpec(blk, lambda b, p, pt: (b, 0, 0, 0, 0))],
        out_specs=pl.BlockSpec((1, rows, LANES), lambda b, p, pt: (b, 0, 0)),
        scratch_shapes=[pltpu.VMEM((1, rows, 1), F32), pltpu.VMEM((1, rows, 1), F32),
                        pltpu.VMEM((1, rows, LANES), F32)])
    return pl.pallas_call(
        functools.partial(_slc_decode_kernel, nsteps=nsteps, pg=pg, page=page, past=past, t4=t4),
        out_shape=jax.ShapeDtypeStruct((nb, rows, LANES), F32),
        grid_spec=grid_spec,
        compiler_params=_cparams(("parallel", "arbitrary")),
        name="nsa_slc_decode",
    )(page_table, qnb, selrows, emat, *([cache_t] * pg), new_t)


def _win_decode_kernel(q_ref, c_ref, n_ref, o_ref, *, past, t4):
    q = q_ref[0]
    rows = q.shape[0]
    wb = c_ref.shape[4]
    m = jnp.full((rows, 1), NEG, F32)
    l = jnp.zeros((rows, 1), F32)
    acc = jnp.zeros((rows, LANES), F32)
    for ref, kstart in ((c_ref, past - wb), (n_ref, past)):
        kt, vt = _kv_t([ref])
        nk = kt.shape[1]
        d = past + _row_token(rows, nk, t4) - (kstart + lax.broadcasted_iota(I32, (rows, nk), 1))
        mask = (d >= 0) & (d < WINDOW)
        m, l, acc = _softmax_block(_dot(q, kt), mask, vt, m, l, acc, v_is_transposed=True)
    o_ref[0] = acc / l


def _win_decode(qnb, cwin_t, new_t, *, past, t4):
    nb, rows, _ = qnb.shape
    return pl.pallas_call(
        functools.partial(_win_decode_kernel, past=past, t4=t4),
        out_shape=jax.ShapeDtypeStruct((nb, rows, LANES), F32),
        grid=(nb,),
        in_specs=[pl.BlockSpec((1, rows, LANES), lambda b: (b, 0, 0)),
                  pl.BlockSpec((1,) + cwin_t.shape[1:], lambda b: (b, 0, 0, 0, 0)),
                  pl.BlockSpec((1,) + new_t.shape[1:], lambda b: (b, 0, 0, 0, 0))],
        out_specs=pl.BlockSpec((1, rows, LANES), lambda b: (b, 0, 0)),
        compiler_params=_cparams(("parallel",)),
        name="nsa_win_decode",
    )(qnb, cwin_t, new_t)


def _merge_kernel(oda_ref, oc_ref, os_ref, ow_ref, gn_ref, ga_ref, gb_ref, e_ref, wda_ref, wns_ref, o_ref):
    g = gn_ref[...]
    ghi = g.astype(BF16)
    glo = (g - ghi.astype(F32)).astype(BF16)
    ons = None
    for b, src in enumerate((oc_ref, os_ref, ow_ref)):
        ge = _dot(ghi, e_ref[b]) + _dot(glo, e_ref[b])
        term = ge * src[...]
        ons = term if ons is None else ons + term
    yda = _dot(oda_ref[...], wda_ref[...])
    yns = _dot(ons.astype(BF16), wns_ref[...])
    o_ref[...] = (ga_ref[...] * yda + gb_ref[...] * yns).astype(o_ref.dtype)


def _merge(oda, ocmp, oslc, owin, gns, gmg, emat, wda, wns):
    n, hd = oda.shape
    d = wda.shape[1]
    tm = _pick_tile(n, 320, 16)
    row = lambda i: (i, 0)
    const2 = lambda i: (0, 0)
    return pl.pallas_call(
        _merge_kernel,
        out_shape=jax.ShapeDtypeStruct((n, d), BF16),
        grid=(n // tm,),
        in_specs=[pl.BlockSpec((tm, hd), row), pl.BlockSpec((tm, hd), row), pl.BlockSpec((tm, hd), row),
                  pl.BlockSpec((tm, hd), row), pl.BlockSpec((tm, LANES), row),
                  pl.BlockSpec((tm, d), lambda i: (i, 0)), pl.BlockSpec((tm, d), lambda i: (i, 1)),
                  pl.BlockSpec((3, LANES, hd), lambda i: (0, 0, 0)),
                  pl.BlockSpec((hd, d), const2), pl.BlockSpec((hd, d), const2)],
        out_specs=pl.BlockSpec((tm, d), row),
        compiler_params=_cparams(("parallel",)),
        name="mixer_merge",
    )(oda, ocmp, oslc, owin, gns, gmg, gmg, emat, wda, wns)


def _lane_pick(x, lane, k):
    return jnp.sum(jnp.where(lane == k, x, 0.0), axis=-1, keepdims=True)


def _route(logits):
    lane = lax.broadcasted_iota(I32, logits.shape, 1)
    isg = lane < N_GROUPS
    gmax = jnp.max(jnp.where(isg, logits, LOWEST), axis=-1, keepdims=True)
    gsel = jnp.min(jnp.where(isg & (logits == gmax), lane, LANES), axis=-1, keepdims=True)
    pgrp = 1.0 / jnp.sum(jnp.where(isg, jnp.exp(logits - gmax), 0.0), axis=-1, keepdims=True)
    lo = N_GROUPS + gsel * EXP_PER_GROUP
    ing = (lane >= lo) & (lane < lo + EXP_PER_GROUP)
    v1 = jnp.max(jnp.where(ing, logits, LOWEST), axis=-1, keepdims=True)
    i1 = jnp.min(jnp.where(ing & (logits == v1), lane, LANES), axis=-1, keepdims=True)
    ing2 = ing & (lane != i1)
    v2 = jnp.max(jnp.where(ing2, logits, LOWEST), axis=-1, keepdims=True)
    i2 = jnp.min(jnp.where(ing2 & (logits == v2), lane, LANES), axis=-1, keepdims=True)
    t = jnp.exp(v2 - v1)
    w1 = pgrp / (1.0 + t)
    w2 = pgrp * t / (1.0 + t)
    e1 = (i1 - N_GROUPS).astype(F32)
    e2 = (i2 - N_GROUPS).astype(F32)
    return jnp.where(lane == 0, e1, jnp.where(lane == 1, e2, jnp.where(lane == 2, w1, jnp.where(lane == 3, w2, 0.0))))


def _store_rows(ref, y, rows, nchunk):
    for s in range(nchunk):
        ref[pl.ds(s, rows, stride=nchunk), :] = y[:, s * LANES:(s + 1) * LANES]


def _wo_route_kernel(m_ref, x_ref, wo_ref, gf_ref, wr_ref, br_ref, x1_ref, h_ref, rt_ref, *, nchunk):
    x1 = x_ref[...] + _dot(m_ref[...], wo_ref[...])
    x1_ref[...] = x1
    h = _rms(x1, gf_ref[...])
    _store_rows(h_ref, h, h.shape[0], nchunk)
    hi = h.astype(BF16)
    lo = (h - hi.astype(F32)).astype(BF16)
    w = wr_ref[...]
    w_hi = w.astype(BF16)
    w_lo = (w - w_hi.astype(F32)).astype(BF16)
    logits = _dot(hi, w_hi) + (_dot(hi, w_lo) + _dot(lo, w_hi))
    rt_ref[...] = _route(logits + br_ref[...])


def _wo_route(mrg, xa, wo, gffn, wr, br):
    n, d = xa.shape
    tm = _pick_tile(n, 320, 16)
    nchunk = d // LANES
    row = lambda i: (i, 0)
    const2 = lambda i: (0, 0)
    return pl.pallas_call(
        functools.partial(_wo_route_kernel, nchunk=nchunk),
        out_shape=(jax.ShapeDtypeStruct((n, d), F32), jax.ShapeDtypeStruct((n * nchunk, LANES), F32),
                   jax.ShapeDtypeStruct((n, LANES), F32)),
        grid=(n // tm,),
        in_specs=[pl.BlockSpec((tm, d), row), pl.BlockSpec((tm, d), row), pl.BlockSpec((d, d), const2),
                  pl.BlockSpec((1, d), const2), pl.BlockSpec((d, LANES), const2),
                  pl.BlockSpec((1, LANES), const2)],
        out_specs=(pl.BlockSpec((tm, d), row), pl.BlockSpec((tm * nchunk, LANES), row),
                   pl.BlockSpec((tm, LANES), row)),
        compiler_params=_cparams(("parallel",)),
        name="wo_route",
    )(mrg, xa, wo, gffn, wr, br)


def _onehots(rt):
    lane = lax.broadcasted_iota(I32, rt.shape, 1)
    lane_f = lane.astype(F32)
    oh0 = jnp.where(lane_f == _lane_pick(rt, lane, 0), 1.0, 0.0)
    oh1 = jnp.where(lane_f == _lane_pick(rt, lane, 1), 1.0, 0.0)
    return lane, oh0, oh1


def _rank_kernel(rt_ref, tri_ref, rk_ref, sz_ref, carry):
    @pl.when(pl.program_id(0) == 0)
    def _():
        carry[...] = jnp.zeros(carry.shape, F32)

    lane, oh0, oh1 = _onehots(rt_ref[...])
    oh = oh0 + oh1
    before = _dot(tri_ref[...], oh.astype(BF16)) + carry[0:1, :]
    r0 = jnp.sum(before * oh0, axis=-1, keepdims=True)
    r1 = jnp.sum(before * oh1, axis=-1, keepdims=True)
    rk_ref[...] = jnp.where(lane == 0, r0, jnp.where(lane == 1, r1, 0.0))
    carry[...] = carry[...] + jnp.sum(oh, axis=0, keepdims=True)
    sz_ref[...] = carry[...]


def _expert_ranks(route, tri):
    n = route.shape[0]
    tm = tri.shape[0]
    return pl.pallas_call(
        _rank_kernel,
        out_shape=(jax.ShapeDtypeStruct((n, LANES), F32), jax.ShapeDtypeStruct((8, LANES), F32)),
        grid=(n // tm,),
        in_specs=[pl.BlockSpec((tm, LANES), lambda i: (i, 0)), pl.BlockSpec((tm, tm), lambda i: (0, 0))],
        out_specs=(pl.BlockSpec((tm, LANES), lambda i: (i, 0)), pl.BlockSpec((8, LANES), lambda i: (0, 0))),
        scratch_shapes=[pltpu.VMEM((8, LANES), F32)],
        compiler_params=_cparams(("arbitrary",)),
        name="moe_rank",
    )(route, tri)


def _dest_kernel(rt_ref, rk_ref, sz_ref, dest_ref, be_ref, nu_ref, *, shift):
    lane8 = lax.broadcasted_iota(I32, (8, LANES), 1)
    sizes = sz_ref[...].astype(I32)
    blk = 1 << shift
    padded = jnp.where(lane8 < N_EXPERTS, lax.shift_right_logical(sizes + (blk - 1), shift) * blk, 0)
    ends = padded
    d = 1
    while d < LANES:
        ends = ends + jnp.where(lane8 >= d, pltpu.roll(ends, d, 1), 0)
        d *= 2
    pstart = (ends - padded)[0:1, :].astype(F32)
    rt = rt_ref[...]
    lane, oh0, oh1 = _onehots(rt)
    rk = rk_ref[...]
    d0 = jnp.sum(oh0 * pstart, axis=-1, keepdims=True) + _lane_pick(rk, lane, 0)
    d1 = jnp.sum(oh1 * pstart, axis=-1, keepdims=True) + _lane_pick(rk, lane, 1)
    dest_ref[...] = jnp.where(lane == 0, d0, jnp.where(lane == 1, d1, 0.0)).astype(I32)
    nbt = be_ref.shape[0]
    lane_b = lax.broadcasted_iota(I32, (nbt, LANES), 1)
    first_row = lax.broadcasted_iota(I32, (nbt, LANES), 0) * blk
    cnt = jnp.sum(jnp.where((lane_b < N_EXPERTS) & (ends[0:1, :] <= first_row), 1, 0), axis=-1, keepdims=True)
    be_ref[...] = jnp.broadcast_to(jnp.minimum(cnt, N_EXPERTS - 1), (nbt, LANES))
    total = jnp.sum(jnp.where(lane8 == N_EXPERTS - 1, ends, 0), axis=-1, keepdims=True)
    nu_ref[...] = jnp.broadcast_to(lax.shift_right_logical(total, shift), (8, LANES))


def _destinations(route, rank, sizes, nbt, shift):
    n = route.shape[0]
    tm = _pick_tile(n, 640, 8)
    nbt_p = _round_up(nbt, 8)
    return pl.pallas_call(
        functools.partial(_dest_kernel, shift=shift),
        out_shape=(jax.ShapeDtypeStruct((n, LANES), I32), jax.ShapeDtypeStruct((nbt_p, LANES), I32),
                   jax.ShapeDtypeStruct((8, LANES), I32)),
        grid=(n // tm,),
        in_specs=[pl.BlockSpec((tm, LANES), lambda i: (i, 0)), pl.BlockSpec((tm, LANES), lambda i: (i, 0)),
                  pl.BlockSpec((8, LANES), lambda i: (0, 0))],
        out_specs=(pl.BlockSpec((tm, LANES), lambda i: (i, 0)), pl.BlockSpec((nbt_p, LANES), lambda i: (0, 0)),
                   pl.BlockSpec((8, LANES), lambda i: (0, 0))),
        compiler_params=_cparams(("arbitrary",)),
        name="moe_dest",
    )(route, rank, sizes)


def _expert_kernel(dest_ref, be_ref, nu_ref, h_ref, wg_ref, wu_ref, wd_ref, o_ref,
                   tok, xbuf, sem, wgb, wub, wdb, *, rows, nchunk, n_assign):
    j = pl.program_id(0)
    n_used = nu_ref[0]

    def row_copy(tile, slot, r):
        return pltpu.make_async_copy(h_ref.at[tok[tile * rows + r]],
                                     xbuf.at[slot, pl.ds(r * nchunk, nchunk)], sem.at[slot])

    def start_tile(tile, slot):
        def body(r, carry):
            row_copy(tile, slot, r).start()
            return carry
        lax.fori_loop(0, rows, body, 0)

    def wait_tile(tile, slot):
        def body(r, carry):
            row_copy(tile, slot, r).wait()
            return carry
        lax.fori_loop(0, rows, body, 0)

    @pl.when(j == 0)
    def _():
        def clear(r, carry):
            tok[r] = 0
            return carry
        lax.fori_loop(0, n_used * rows, clear, 0)

        def place(a, carry):
            tok[dest_ref[a]] = lax.shift_right_logical(a, 1)
            return carry
        lax.fori_loop(0, n_assign, place, 0)
        start_tile(0, 0)

    slot = lax.rem(j, 2)
    active = j < n_used
    changed = (j == 0) | (be_ref[j] != be_ref[jnp.maximum(j - 1, 0)])

    @pl.when(active & changed)
    def _():
        wgb[...] = wg_ref[0].astype(BF16)
        wub[...] = wu_ref[0].astype(BF16)
        wdb[...] = wd_ref[0].astype(BF16)

    @pl.when(active)
    def _():
        @pl.when(j + 1 < n_used)
        def _():
            start_tile(j + 1, 1 - slot)

        wait_tile(j, slot)
        x = jnp.concatenate([xbuf[slot, pl.ds(s, rows, stride=nchunk), :] for s in range(nchunk)],
                            axis=1).astype(BF16)
        a = _dot(x, wgb[...])
        u = _dot(x, wub[...])
        y = _dot((jax.nn.silu(a) * u).astype(BF16), wdb[...])
        _store_rows(o_ref, y, rows, nchunk)

    @pl.when(jnp.logical_not(active))
    def _():
        o_ref[...] = jnp.zeros(o_ref.shape, F32)


def _experts(dest, blk_e, n_used, h3, wg, wu, wd, rows):
    d, de = wg.shape[1], wg.shape[2]
    nchunk = d // LANES
    nbt = blk_e.shape[0]
    grid_spec = pltpu.PrefetchScalarGridSpec(
        num_scalar_prefetch=3, grid=(nbt,),
        in_specs=[pl.BlockSpec(memory_space=pl.ANY),
                  pl.BlockSpec((1, d, de), lambda j, dst, be, nu: (be[j], 0, 0)),
                  pl.BlockSpec((1, d, de), lambda j, dst, be, nu: (be[j], 0, 0)),
                  pl.BlockSpec((1, de, d), lambda j, dst, be, nu: (be[j], 0, 0))],
        out_specs=pl.BlockSpec((rows * nchunk, LANES), lambda j, dst, be, nu: (j, 0)),
        scratch_shapes=[pltpu.SMEM((nbt * rows,), I32), pltpu.VMEM((2, rows * nchunk, LANES), F32),
                        pltpu.SemaphoreType.DMA((2,)),
                        pltpu.VMEM((d, de), BF16), pltpu.VMEM((d, de), BF16), pltpu.VMEM((de, d), BF16)])
    return pl.pallas_call(
        functools.partial(_expert_kernel, rows=rows, nchunk=nchunk, n_assign=dest.shape[0]),
        out_shape=jax.ShapeDtypeStruct((nbt * rows * nchunk, LANES), F32),
        grid_spec=grid_spec,
        compiler_params=_cparams(("arbitrary",)),
        name="moe_experts",
    )(dest, blk_e, n_used, h3, wg, wu, wd)


def _combine_kernel(dest_ref, ys_ref, rt_ref, x1_ref, p_ref, gp_ref, wpg_ref, wpl_ref, gfin_ref, o_ref,
                    gbuf, sem, *, tm, nchunk, final):
    i = pl.program_id(0)
    slot = lax.rem(i, 2)

    def copy(tile, sl, j):
        return pltpu.make_async_copy(ys_ref.at[dest_ref[tile * (2 * tm) + j]],
                                     gbuf.at[sl, pl.ds(j * nchunk, nchunk)], sem.at[sl])

    def start_tile(tile, sl):
        def body(j, carry):
            copy(tile, sl, j).start()
            return carry
        lax.fori_loop(0, 2 * tm, body, 0)

    @pl.when(i == 0)
    def _():
        start_tile(0, 0)

    @pl.when(i + 1 < pl.num_programs(0))
    def _():
        start_tile(i + 1, 1 - slot)

    def drain(j, carry):
        copy(i, slot, j).wait()
        return carry

    lax.fori_loop(0, 2 * tm, drain, 0)
    rt = rt_ref[...]
    lane = lax.broadcasted_iota(I32, rt.shape, 1)
    y0 = jnp.concatenate([gbuf[slot, pl.ds(s, tm, stride=2 * nchunk), :] for s in range(nchunk)], axis=1)
    y1 = jnp.concatenate([gbuf[slot, pl.ds(nchunk + s, tm, stride=2 * nchunk), :] for s in range(nchunk)], axis=1)
    x2 = x1_ref[...] + (_lane_pick(rt, lane, 2) * y0 + _lane_pick(rt, lane, 3) * y1)
    gate = jax.nn.sigmoid(_dot(_rms(x2, gp_ref[...]).astype(BF16), wpg_ref[...]))
    x3 = x2 + gate * _dot(p_ref[...].astype(BF16), wpl_ref[...])
    o_ref[...] = _rms(x3, gfin_ref[...]) if final else x3


def _combine(dest, ys3, route, x1, pa, gple, wpg, wpl, gfin, final):
    n, d = x1.shape
    nchunk = d // LANES
    tm = _pick_tile(n, 320, 16)
    dp = pa.shape[1]
    row = lambda i, dst: (i, 0)
    const2 = lambda i, dst: (0, 0)
    grid_spec = pltpu.PrefetchScalarGridSpec(
        num_scalar_prefetch=1, grid=(n // tm,),
        in_specs=[pl.BlockSpec(memory_space=pl.ANY),
                  pl.BlockSpec((tm, LANES), row), pl.BlockSpec((tm, d), row), pl.BlockSpec((tm, dp), row),
                  pl.BlockSpec((1, d), const2), pl.BlockSpec((d, d), const2), pl.BlockSpec((dp, d), const2),
                  pl.BlockSpec((1, d), const2)],
        out_specs=pl.BlockSpec((tm, d), row),
        scratch_shapes=[pltpu.VMEM((2, 2 * tm * nchunk, LANES), F32), pltpu.SemaphoreType.DMA((2,))])
    return pl.pallas_call(
        functools.partial(_combine_kernel, tm=tm, nchunk=nchunk, final=final),
        out_shape=jax.ShapeDtypeStruct((n, d), F32),
        grid_spec=grid_spec,
        compiler_params=_cparams(("arbitrary",)),
        name="moe_combine_tail",
    )(dest, ys3, route, x1, pa, gple, wpg, wpl, gfin)


def _rope_tables(pos):
    inv = ROPE_THETA ** (-jnp.arange(ROPE_HALF, dtype=F32) / ROPE_HALF)
    ang = pos.astype(F32)[:, None] * inv[None, :]
    cos, sin = jnp.cos(ang), jnp.sin(ang)
    n = pos.shape[0]
    rest = DA_DH - 2 * ROPE_HALF
    z = jnp.zeros((n, ROPE_HALF), F32)
    zr = jnp.zeros((n, rest), F32)
    c64 = jnp.concatenate([cos, cos, jnp.ones((n, rest), F32)], axis=1)
    a64 = jnp.concatenate([-sin, z, zr], axis=1)
    b64 = jnp.concatenate([z, sin, zr], axis=1)
    rep = LANES // DA_DH
    return jnp.tile(c64, (1, rep)), jnp.tile(a64, (1, rep)), jnp.tile(b64, (1, rep))


def _column_tables():
    ones = lambda n: jnp.ones((n,), F32)
    zeros = lambda n: jnp.zeros((n,), F32)
    gw = NSA_KV * NSA_DH
    rmask = jnp.concatenate([ones(C_DA_Q), ones(C_DA_K), zeros(C_DA_V), ones(C_NSA_Q)]
                            + [ones(gw), zeros(gw)] * 3)
    cscale = jnp.concatenate([ones(C_DA_Q) * DA_DH ** -0.5, ones(C_DA_K), ones(C_DA_V),
                              ones(C_NSA_Q) * NSA_DH ** -0.5, ones(C_NSA_KV)])
    return rmask[None, :], cscale[None, :]


def _block_expand(width, nkeys):
    blk = jnp.arange(width, dtype=I32)[:, None]
    key = jnp.arange(nkeys, dtype=I32)[None, :]
    return (key // BLOCK == blk).astype(BF16)


def _pad_rows(x, axis, size):
    pad = [(0, 0)] * x.ndim
    pad[axis] = (0, size - x.shape[axis])
    return jnp.pad(x, pad)


def _cmp_layout(kc, b, nc, width):
    x = kc.reshape(b, nc, NSA_KV, NSA_DH).transpose(0, 2, 1, 3)
    return _pad_rows(x, 2, width).astype(BF16)


def kernel(x_prompt, x_sample, cache_da_kv, cache_nsa_kv, cache_win_kv, page_table, p_prompt, p_sample,
           g_mix, w_in, da_lambda, da_subln, nsa_cpos, nsa_cmp, w_da_up, w_nsa_up, w_o, g_ffn,
           w_rg, b_rg, w_re, b_re, w_e_gate, w_e_up, w_e_down, g_ple, w_ple_gate, w_ple, g_final):
    b, s, d = x_prompt.shape
    nb, t4, _ = x_sample.shape
    depth = g_mix.shape[0]
    npr, nsm = b * s, nb * t4
    n = npr + nsm
    page = cache_da_kv.shape[2]
    npages = page_table.shape[1]
    past = npages * page
    wb = cache_win_kv.shape[2]
    nchunk = d // LANES
    rq = t4 * DA_REP
    assert rq % 8 == 0 and page % BLOCK == 0 and s % BLOCK == 0 and t4 <= page and d % LANES == 0

    xa = jnp.concatenate([x_prompt.reshape(npr, d), x_sample.reshape(nsm, d)], axis=0)
    pos = jnp.concatenate([jnp.tile(jnp.arange(s, dtype=I32), b), jnp.tile(past + jnp.arange(t4, dtype=I32), nb)])
    cos, sa, sb = _rope_tables(pos)
    rmask, cscale = _column_tables()

    nc_p, ns_p = s // BLOCK, max(-(-s // BLOCK), N_SELECT)
    tot_s = past + t4
    nc_s, ns_s = tot_s // BLOCK, max(-(-tot_s // BLOCK), N_SELECT)
    wid_p, wid_s = _round_up(ns_p, LANES), _round_up(ns_s, LANES)
    emat_p = _block_expand(wid_p, s)
    pg_s, nsteps_s = _page_steps(npages)
    emat_s = _block_expand(wid_s, (nsteps_s + 1) * pg_s * page)

    col = jnp.arange(LANES, dtype=I32)[:, None]
    lane = jnp.arange(NSA_HEADS * NSA_DH, dtype=I32)[None, :]
    gate_expand = jnp.stack([(col == br * NSA_HEADS + lane // NSA_DH) for br in range(3)]).astype(BF16)

    n_assign = 2 * n
    shift = int(math.log2(EXPERT_ROWS))
    nbt = -(-n_assign // EXPERT_ROWS) + N_EXPERTS
    tm_rank = _pick_tile(n, 640, 16)
    tri = jnp.tril(jnp.ones((tm_rank, tm_rank), BF16), -1)

    outs = {k: [] for k in ("da_p", "da_s", "ns_p", "ns_s", "win_p", "win_s")}
    for i in range(depth):
        lam_init = 0.8 - 0.6 * math.exp(-0.3 * i)
        wi = w_in[i]
        g1 = g_mix[i][None, :]
        zf, zb = _proj_rope(xa, g1, wi[:, :S5].astype(BF16), cos, sa, sb, rmask, cscale)
        wg_cols = wi[:, S5:S6].reshape(d, NSA_HEADS, 3).transpose(0, 2, 1).reshape(d, C_NSA_G)
        gns = _gate_proj(xa, g1, _pad_rows(wg_cols, 1, LANES).astype(BF16))
        gmg = _gate_proj(xa, g1, wi[:, S6:].astype(BF16))
        lam_p = da_lambda[i].astype(F32)
        subln = da_subln[i][None, :].astype(F32)
        wp = jnp.tile(nsa_cpos[i].astype(F32), (1, 1, NSA_KV))
        wcm = nsa_cmp[i].astype(F32)
        zc = jnp.zeros_like(wcm)
        wc = jnp.concatenate([jnp.concatenate([wcm, zc], axis=2), jnp.concatenate([zc, wcm], axis=2)], axis=1)

        oda = _da_flash(zb, n, b, s, lam_p, subln, lam_init)
        kc, vc = _compress_prompt(zf, wp, wc, b, s)
        kc2 = _cmp_layout(kc, b, nc_p, wid_p)
        kc2 = jnp.concatenate([kc2, kc2], axis=-1)
        vcl = _cmp_layout(vc, b, nc_p, wid_p)
        vz = jnp.zeros_like(vcl)
        vc2 = jnp.stack([jnp.concatenate([vcl, vz], axis=-1), jnp.concatenate([vz, vcl], axis=-1)], axis=2)
        ocmp, sel_p = _cmp_select(zb, kc2, vc2, n, b, s, nc=nc_p, ns=ns_p)
        kvp = zb[:npr, S4:S5].reshape(npr, 6, NSA_KV, NSA_DH)
        dup = lambda k: jnp.concatenate([k, k], axis=-1).reshape(npr, NSA_KV * LANES)
        with_ones = lambda v: jnp.concatenate([v, jnp.ones_like(v)], axis=-1).reshape(npr, NSA_KV * LANES)
        oslc = _nsa_flash(zb, dup(kvp[:, 2]), with_ones(kvp[:, 3]), n, b, s, sel_p, emat_p)
        owin = _nsa_flash(zb, dup(kvp[:, 4]), with_ones(kvp[:, 5]), n, b, s)

        zs = zb[npr:].reshape(nb, t4, S5)
        zfs = zf[npr:].reshape(nb, t4, S5)
        q6 = zs[..., :S1].reshape(nb, t4, DA_KV, DA_REP, 2, DA_DH).transpose(0, 2, 4, 1, 3, 5)
        eye_g = jnp.eye(DA_KV, dtype=BF16)
        eye_c = jnp.eye(2, dtype=BF16)
        qbd = (q6[:, :, :, :, :, None, None, :] * eye_g[None, :, None, None, None, :, None, None]
               * eye_c[None, None, :, None, None, None, :, None]).reshape(nb, DA_KV * 2 * rq, DA_KV * 2 * DA_DH)
        cda = cache_da_kv[i].reshape(cache_da_kv.shape[1], page * 2 * DA_KV, DA_DV)
        new_da = _pad_rows(zfs[..., S1:S3], 1, page).reshape(nb, page * 2 * DA_KV, DA_DV)
        oda_s = _da_decode(page_table, qbd, cda, new_da, lam_p, subln, lam_init)
        oda_s = oda_s.reshape(nb, DA_KV, t4, DA_REP, DA_DV).transpose(0, 2, 1, 3, 4).reshape(nsm, DA_HEADS * DA_DV)

        qns = zs[..., S3:S4].reshape(nb, t4, NSA_KV, NSA_REP, NSA_DH).transpose(0, 2, 1, 3, 4)
        eye_n = jnp.eye(NSA_KV, dtype=BF16)
        qnb = (qns[:, :, :, :, None, :] * eye_n[None, :, None, None, :, None]
               ).reshape(nb, NSA_KV * t4 * NSA_REP, LANES)
        cns_t = cache_nsa_kv[i].transpose(0, 2, 3, 4, 1)
        cwin_t = cache_win_kv[i].transpose(0, 2, 3, 4, 1)
        kw = NSA_KV * NSA_DH
        wt = jnp.tile(nsa_cpos[i].astype(F32).transpose(0, 2, 1)[:, None],
                      (1, NSA_KV, 1, page // BLOCK)).reshape(2 * kw, page)
        ocmp_s, sel_s = _cmp_decode(page_table, qns.reshape(nb, NSA_KV, t4 * NSA_REP, NSA_DH), cns_t, wt,
                                    nsa_cmp[i].astype(F32).transpose(0, 2, 1),
                                    past=past, t4=t4, nc=nc_s, ns=ns_s, width=wid_s)
        ocmp_s = ocmp_s.reshape(nb, NSA_KV, t4, NSA_REP, NSA_DH).transpose(0, 2, 1, 3, 4
                                                                             ).reshape(nsm, NSA_HEADS * NSA_DH)
        selrows = jnp.broadcast_to(sel_s[:, :, :, None, :], (nb, NSA_KV, t4, NSA_REP, wid_s)
                                   ).reshape(nb, NSA_KV * t4 * NSA_REP, wid_s)
        new_t = lambda x: _pad_rows(x.reshape(nb, t4, 2, NSA_KV, NSA_DH), 1, page).transpose(0, 2, 3, 4, 1)
        oslc_s = _slc_decode(page_table, qnb, selrows, emat_s, cns_t, new_t(zfs[..., S4 + 2 * kw:S4 + 4 * kw]),
                             past=past, t4=t4)
        owin_s = _win_decode(qnb, cwin_t, new_t(zfs[..., S4 + 4 * kw:S5]), past=past, t4=t4)

        def halves(o):
            o = o.reshape(nb, NSA_KV, t4, NSA_REP, NSA_KV, NSA_DH)
            o = jnp.stack([o[:, g, :, :, g] for g in range(NSA_KV)], axis=2)
            return o.reshape(nsm, NSA_HEADS * NSA_DH)

        oslc_s, owin_s = halves(oslc_s), halves(owin_s)

        oda = oda.at[npr:].set(oda_s.astype(BF16))
        ocmp = ocmp.at[npr:].set(ocmp_s)
        oslc = oslc.at[npr:].set(oslc_s)
        owin = owin.at[npr:].set(owin_s)
        mrg = _merge(oda, ocmp, oslc, owin, gns, gmg, gate_expand,
                     w_da_up[i].astype(BF16), w_nsa_up[i].astype(BF16))
        wr = _pad_rows(jnp.concatenate([w_rg[i], w_re[i]], axis=1).astype(F32), 1, LANES)
        br = _pad_rows(jnp.concatenate([b_rg[i], b_re[i]]).astype(F32)[None, :], 1, LANES)
        x1, hrows, route = _wo_route(mrg, xa, w_o[i].astype(BF16), g_ffn[i][None, :], wr, br)
        rank, sizes = _expert_ranks(route, tri)
        dest, be, nu = _destinations(route, rank, sizes, nbt, shift)
        dest_flat = dest[:, :2].reshape(n_assign)
        ys = _experts(dest_flat, be[:nbt, 0], nu[0, :1], hrows.reshape(n, nchunk, LANES),
                      w_e_gate[i], w_e_up[i], w_e_down[i], EXPERT_ROWS)
        pa = jnp.concatenate([p_prompt[i].reshape(npr, -1), p_sample[i].reshape(nsm, -1)], axis=0)
        xa = _combine(dest_flat, ys.reshape(nbt * EXPERT_ROWS, nchunk, LANES), route, x1, pa,
                      g_ple[i][None, :], w_ple_gate[i].astype(BF16), w_ple[i].astype(BF16),
                      g_final[None, :], i == depth - 1)

        zfp = zf[:npr].reshape(b, s, S5)
        outs["da_p"].append(zfp[..., S1:S3].reshape(b, s, 2, DA_KV, DA_DV))
        outs["da_s"].append(zfs[..., S1:S3].reshape(nb, t4, 2, DA_KV, DA_DV))
        outs["ns_p"].append(zfp[..., S4:S4 + 4 * kw].reshape(b, s, 4, NSA_KV, NSA_DH))
        outs["ns_s"].append(zfs[..., S4:S4 + 4 * kw].reshape(nb, t4, 4, NSA_KV, NSA_DH))
        win_p = zfp[..., S4 + 4 * kw:S5].reshape(b, s, 2, NSA_KV, NSA_DH)
        outs["win_p"].append(win_p[:, s - min(WINDOW, s):])
        win_s = jnp.concatenate([cache_win_kv[i], zfs[..., S4 + 4 * kw:S5].reshape(nb, t4, 2, NSA_KV, NSA_DH)], axis=1)
        outs["win_s"].append(win_s[:, win_s.shape[1] - wb:])

    y_prompt = xa[:npr].reshape(b, s, d)
    y_sample = xa[npr:].reshape(nb, t4, d)
    return (y_prompt, y_sample, jnp.stack(outs["da_p"]), jnp.stack(outs["da_s"]), jnp.stack(outs["ns_p"]),
            jnp.stack(outs["ns_s"]), jnp.stack(outs["win_p"]), jnp.stack(outs["win_s"]))
```

```python
import functools
import math

import jax
import jax.numpy as jnp
from jax import lax
from jax.experimental import pallas as pl
from jax.experimental.pallas import tpu as pltpu

F32 = jnp.float32
BF16 = jnp.bfloat16
I32 = jnp.int32
HIGHEST = lax.Precision.HIGHEST

DA_HEADS, DA_KV, DA_REP, DA_DH = 8, 4, 2, 64
DA_DV = 2 * DA_DH
NSA_HEADS, NSA_KV, NSA_REP, NSA_DH = 16, 2, 8, 64
BLOCK = 64
BLOCK_SHIFT = 6
N_SELECT = 16
WINDOW = 512
FORCE_BONUS = 1000.0
ROPE_THETA = 500000.0
ROPE_HALF = 8
N_GROUPS, EXP_PER_GROUP = 8, 8
N_EXPERTS = N_GROUPS * EXP_PER_GROUP
EPS = 1e-6
NEG = -1e30
LOWEST = -3e38

LANES = 128
VMEM_LIMIT_BYTES = 56 * 1024 * 1024

C_DA_Q = DA_HEADS * 2 * DA_DH
C_DA_K = DA_KV * 2 * DA_DH
C_DA_V = DA_KV * DA_DV
C_NSA_Q = NSA_HEADS * NSA_DH
C_NSA_KV = 6 * NSA_KV * NSA_DH
C_NSA_G = 3 * NSA_HEADS
S1 = C_DA_Q
S2 = S1 + C_DA_K
S3 = S2 + C_DA_V
S4 = S3 + C_NSA_Q
S5 = S4 + C_NSA_KV
S6 = S5 + C_NSA_G

EXPERT_ROWS = 128
DECODE_PAGES = 8
FLASH_TILE = 512
FLASH_ROWS = 128
FLASH_KEYS = 256


def _cparams(sem):
    return pltpu.CompilerParams(dimension_semantics=sem, vmem_limit_bytes=VMEM_LIMIT_BYTES)


def _pick_tile(n, target, mult):
    best = None
    for t in range(mult, min(n, target) + 1, mult):
        if n % t == 0:
            best = t
    return n if best is None else best


def _round_up(x, m):
    return (x + m - 1) // m * m


def _nt(a, b):
    return lax.dot_general(a, b, (((1,), (1,)), ((), ())), preferred_element_type=F32)


def _dot(a, b, **kw):
    return jnp.dot(a, b, preferred_element_type=F32, **kw)


def _rms(x, g):
    ms = jnp.mean(x * x, axis=-1, keepdims=True)
    return x * lax.rsqrt(ms + EPS) * g


def _proj_rope_kernel(x_ref, g_ref, w_ref, cos_ref, sa_ref, sb_ref, rm_ref, cs_ref,
                      of_ref, ob_ref, h_scr, *, n_chunks):
    @pl.when(pl.program_id(1) == 0)
    def _():
        h_scr[...] = _rms(x_ref[...], g_ref[...]).astype(BF16)

    z = _dot(h_scr[...], w_ref[...])
    c, sa, sb = cos_ref[...], sa_ref[...], sb_ref[...]
    for k in range(n_chunks):
        sl = slice(k * LANES, (k + 1) * LANES)
        zc = z[:, sl]
        rot = zc * c + pltpu.roll(zc, LANES - ROPE_HALF, 1) * sa + pltpu.roll(zc, ROPE_HALF, 1) * sb
        out = jnp.where(rm_ref[:, sl] > 0.5, rot, zc) * cs_ref[:, sl]
        of_ref[:, sl] = out
        ob_ref[:, sl] = out.astype(BF16)


def _proj_rope(xa, g, w, cos, sa, sb, rmask, cscale):
    n, d = xa.shape
    ca = w.shape[1]
    tm = _pick_tile(n, 640, 16)
    tn = _pick_tile(ca, 1024, LANES)
    row = lambda i, j: (i, 0)
    col = lambda i, j: (0, j)
    return pl.pallas_call(
        functools.partial(_proj_rope_kernel, n_chunks=tn // LANES),
        out_shape=(jax.ShapeDtypeStruct((n, ca), F32), jax.ShapeDtypeStruct((n, ca), BF16)),
        grid=(n // tm, ca // tn),
        in_specs=[pl.BlockSpec((tm, d), row), pl.BlockSpec((1, d), lambda i, j: (0, 0)),
                  pl.BlockSpec((d, tn), col),
                  pl.BlockSpec((tm, LANES), row), pl.BlockSpec((tm, LANES), row),
                  pl.BlockSpec((tm, LANES), row),
                  pl.BlockSpec((1, tn), col), pl.BlockSpec((1, tn), col)],
        out_specs=(pl.BlockSpec((tm, tn), lambda i, j: (i, j)),
                   pl.BlockSpec((tm, tn), lambda i, j: (i, j))),
        scratch_shapes=[pltpu.VMEM((tm, d), BF16)],
        compiler_params=_cparams(("parallel", "arbitrary")),
        name="proj_rope",
    )(xa, g, w, cos, sa, sb, rmask, cscale)


def _gate_kernel(x_ref, g_ref, w_ref, o_ref, h_scr):
    @pl.when(pl.program_id(1) == 0)
    def _():
        h_scr[...] = _rms(x_ref[...], g_ref[...]).astype(BF16)

    o_ref[...] = jax.nn.sigmoid(_dot(h_scr[...], w_ref[...]))


def _gate_proj(xa, g, w):
    n, d = xa.shape
    ca = w.shape[1]
    tm = _pick_tile(n, 640, 16)
    tn = _pick_tile(ca, 1024, LANES)
    return pl.pallas_call(
        _gate_kernel,
        out_shape=jax.ShapeDtypeStruct((n, ca), F32),
        grid=(n // tm, ca // tn),
        in_specs=[pl.BlockSpec((tm, d), lambda i, j: (i, 0)), pl.BlockSpec((1, d), lambda i, j: (0, 0)),
                  pl.BlockSpec((d, tn), lambda i, j: (0, j))],
        out_specs=pl.BlockSpec((tm, tn), lambda i, j: (i, j)),
        scratch_shapes=[pltpu.VMEM((tm, d), BF16)],
        compiler_params=_cparams(("parallel", "arbitrary")),
        name="gate_proj",
    )(xa, g, w)


def _softmax_block(s, mask, v, m_old, l_old, acc_old, v_is_transposed=False):
    if mask is not None:
        s = jnp.where(mask, s, NEG)
    m_new = jnp.maximum(m_old, jnp.max(s, axis=-1, keepdims=True))
    p = jnp.exp(s - m_new)
    if mask is not None:
        p = jnp.where(mask, p, 0.0)
    alpha = jnp.exp(m_old - m_new)
    l_new = alpha * l_old + jnp.sum(p, axis=-1, keepdims=True)
    pv = _nt(p.astype(v.dtype), v) if v_is_transposed else _dot(p.astype(v.dtype), v)
    return m_new, l_new, alpha * acc_old + pv


def _flash_update(s, mask, v, m_ref, l_ref, acc_ref, idx, v_is_transposed=False):
    m, l, acc = _softmax_block(s, mask, v, m_ref[idx], l_ref[idx], acc_ref[idx], v_is_transposed)
    m_ref[idx] = m
    l_ref[idx] = l
    acc_ref[idx] = acc


def _flash_init(m_ref, l_ref, acc_ref):
    m_ref[...] = jnp.full(m_ref.shape, NEG, F32)
    l_ref[...] = jnp.zeros(l_ref.shape, F32)
    acc_ref[...] = jnp.zeros(acc_ref.shape, F32)


def _da_lambda(lam_ref, lam_init):
    a = jnp.sum(lam_ref[0:1, :] * lam_ref[1:2, :], axis=-1, keepdims=True)
    b = jnp.sum(lam_ref[2:3, :] * lam_ref[3:4, :], axis=-1, keepdims=True)
    return jnp.exp(a) - jnp.exp(b) + lam_init


def _da_finish(a0, a1, lam, sub, lam_init):
    o = a0 - lam * a1
    return _rms(o, sub) * (1.0 - lam_init)


def _flash_rows(s, bias, v1, m_ref, acc_ref, idx):
    if bias is not None:
        s = s + bias
    m_old = m_ref[idx]
    m_new = jnp.maximum(m_old, jnp.max(s, axis=-1, keepdims=True))
    p = jnp.exp(s - jnp.tile(m_new, (1, s.shape[1] // LANES)))
    alpha = jnp.exp(m_old - m_new)
    acc_ref[idx] = jnp.tile(alpha, (1, v1.shape[1] // LANES)) * acc_ref[idx] + _dot(p.astype(v1.dtype), v1)
    m_ref[idx] = m_new


def _tri_tables(nq):
    qi = [q for q in range(nq) for _ in range(q + 1)]
    ki = [k for q in range(nq) for k in range(q + 1)]
    return jnp.asarray(qi, I32), jnp.asarray(ki, I32)


def _da_flash_kernel(qi_ref, ki_ref, lam_ref, sub_ref, q_ref, k_ref, v_ref, base_ref, o_ref, m_scr, acc_scr,
                     *, tq, rb, kb, lam_init):
    del base_ref
    t = pl.program_id(2)
    qi = qi_ref[t]
    ki = ki_ref[t]

    @pl.when(ki == 0)
    def _():
        m_scr[...] = jnp.full(m_scr.shape, NEG, F32)
        acc_scr[...] = jnp.zeros(acc_scr.shape, F32)

    upper = lax.broadcasted_iota(I32, (1, LANES), 1) >= DA_DH

    def step(diag):
        k = k_ref[...]
        v1 = jnp.concatenate([v_ref[...], jnp.ones((tq, DA_DV), BF16)], axis=1)
        for r in range(DA_REP):
            for r0 in range(0, tq, rb):
                nk = r0 + rb if diag else tq
                q = q_ref[pl.ds(r0, rb), r * DA_DV:(r + 1) * DA_DV]
                qc = [jnp.where(upper if c == 1 else jnp.logical_not(upper), q, jnp.zeros_like(q))
                      for c in range(2)]
                for c0 in range(0, nk, kb):
                    c1 = min(c0 + kb, nk)
                    bias = None
                    if diag and c1 > r0:
                        row = r0 - c0 + lax.broadcasted_iota(I32, (rb, c1 - c0), 0)
                        bias = jnp.where(lax.broadcasted_iota(I32, (rb, c1 - c0), 1) <= row, 0.0, NEG)
                    for c in range(2):
                        _flash_rows(_nt(qc[c], k[c0:c1]), bias, v1[c0:c1], m_scr, acc_scr,
                                    (c, pl.ds(r * tq + r0, rb)))

    @pl.when(ki < qi)
    def _():
        step(False)

    @pl.when(ki == qi)
    def _():
        step(True)
        lam = _da_lambda(lam_ref, lam_init)
        a0 = acc_scr[0, :, :DA_DV] / acc_scr[0, :, DA_DV:]
        a1 = acc_scr[1, :, :DA_DV] / acc_scr[1, :, DA_DV:]
        o = _da_finish(a0, a1, lam, sub_ref[...], lam_init)
        for r in range(DA_REP):
            o_ref[:, r * DA_DV:(r + 1) * DA_DV] = o[r * tq:(r + 1) * tq].astype(o_ref.dtype)


def _da_flash(zb, base, b, s, lam_p, subln, lam_init):
    tq = _pick_tile(s, FLASH_TILE, 16)
    rb = _pick_tile(tq, FLASH_ROWS, 16)
    kb = _pick_tile(tq, FLASH_KEYS, LANES)
    nq = s // tq
    qtab, ktab = _tri_tables(nq)
    kcol, vcol = S1 // DA_DV, S2 // DA_DV
    grid_spec = pltpu.PrefetchScalarGridSpec(
        num_scalar_prefetch=2, grid=(b, DA_KV, qtab.shape[0]),
        in_specs=[pl.BlockSpec((4, DA_DH), lambda bi, g, t, qt, kt: (0, 0)),
                  pl.BlockSpec((1, DA_DV), lambda bi, g, t, qt, kt: (0, 0)),
                  pl.BlockSpec((tq, DA_REP * DA_DV), lambda bi, g, t, qt, kt: (bi * nq + qt[t], g)),
                  pl.BlockSpec((tq, DA_DV), lambda bi, g, t, qt, kt: (bi * nq + kt[t], kcol + g)),
                  pl.BlockSpec((tq, DA_DV), lambda bi, g, t, qt, kt: (bi * nq + kt[t], vcol + g)),
                  pl.BlockSpec(memory_space=pl.ANY)],
        out_specs=pl.BlockSpec((tq, DA_REP * DA_DV), lambda bi, g, t, qt, kt: (bi * nq + qt[t], g)),
        scratch_shapes=[pltpu.VMEM((2, DA_REP * tq, LANES), F32),
                        pltpu.VMEM((2, DA_REP * tq, 2 * DA_DV), F32)])
    return pl.pallas_call(
        functools.partial(_da_flash_kernel, tq=tq, rb=rb, kb=kb, lam_init=lam_init),
        out_shape=jax.ShapeDtypeStruct(base.shape, base.dtype),
        grid_spec=grid_spec,
        input_output_aliases={7: 0},
        compiler_params=_cparams(("parallel", "parallel", "arbitrary")),
        name="da_flash",
    )(qtab, ktab, lam_p, subln, zb, zb, zb, base)


def _page_steps(npages):
    pg = _pick_tile(npages, DECODE_PAGES, 1)
    return pg, npages // pg


def _page_specs(block, npages, pg, col_block):
    nd = len(block) - 1

    def spec(u):
        def index(b, p, pt):
            idx = [pt[b, jnp.minimum(p * pg + u, npages - 1)]] + [0] * nd
            if col_block is not None:
                idx[1] = col_block
            return tuple(idx)
        return pl.BlockSpec(block, index)

    return [spec(u) for u in range(pg)]


def _da_decode_kernel(pt_ref, lam_ref, sub_ref, q_ref, *refs, nsteps, pg, page, rq, lam_init):
    del pt_ref
    c_refs = refs[:pg]
    n_ref, o_ref, m_scr, l_scr, acc_scr = refs[pg:]
    p = pl.program_id(1)
    per = 2 * DA_KV

    @pl.when(p == 0)
    def _():
        _flash_init(m_scr, l_scr, acc_scr)

    q = q_ref[0]
    rows = q.shape[0]

    def heads(ref, first):
        return jnp.concatenate([ref[0, pl.ds(first + g, page, stride=per), :] for g in range(DA_KV)],
                               axis=1).astype(BF16)

    def step(blocks, mask):
        k = jnp.concatenate([heads(r, 0) for r in blocks], axis=0)
        v = jnp.concatenate([heads(r, DA_KV) for r in blocks], axis=0)
        _flash_update(_nt(q, k), mask, v, m_scr, l_scr, acc_scr, 0)

    @pl.when(p < nsteps)
    def _():
        step(c_refs, None)

    @pl.when(p == nsteps)
    def _():
        row = lax.broadcasted_iota(I32, (rows, page), 0)
        col = lax.broadcasted_iota(I32, (rows, page), 1)
        step([n_ref], col <= (row % rq) // DA_REP)
        lam = _da_lambda(lam_ref, lam_init)
        a = acc_scr[0] / l_scr[0]
        for g in range(DA_KV):
            cols = slice(g * DA_DV, (g + 1) * DA_DV)
            a0 = a[g * 2 * rq:g * 2 * rq + rq, cols]
            a1 = a[g * 2 * rq + rq:(g + 1) * 2 * rq, cols]
            o_ref[0, g] = _da_finish(a0, a1, lam, sub_ref[...], lam_init)


def _da_decode(page_table, qbd, cache, newkv, lam_p, subln, lam_init):
    nb, rows, ck = qbd.shape
    npages = page_table.shape[1]
    prow = cache.shape[1]
    page = prow // (2 * DA_KV)
    rq = rows // (DA_KV * 2)
    pg, nsteps = _page_steps(npages)
    grid_spec = pltpu.PrefetchScalarGridSpec(
        num_scalar_prefetch=1, grid=(nb, nsteps + 1),
        in_specs=[pl.BlockSpec((4, DA_DH), lambda b, p, pt: (0, 0)),
                  pl.BlockSpec((1, DA_DV), lambda b, p, pt: (0, 0)),
                  pl.BlockSpec((1, rows, ck), lambda b, p, pt: (b, 0, 0))]
        + _page_specs((1, prow, DA_DV), npages, pg, None)
        + [pl.BlockSpec((1, prow, DA_DV), lambda b, p, pt: (b, 0, 0))],
        out_specs=pl.BlockSpec((1, DA_KV, rq, DA_DV), lambda b, p, pt: (b, 0, 0, 0)),
        scratch_shapes=[pltpu.VMEM((1, rows, 1), F32), pltpu.VMEM((1, rows, 1), F32),
                        pltpu.VMEM((1, rows, DA_KV * DA_DV), F32)])
    return pl.pallas_call(
        functools.partial(_da_decode_kernel, nsteps=nsteps, pg=pg, page=page, rq=rq, lam_init=lam_init),
        out_shape=jax.ShapeDtypeStruct((nb, DA_KV, rq, DA_DV), F32),
        grid_spec=grid_spec,
        compiler_params=_cparams(("parallel", "arbitrary")),
        name="da_decode",
    )(page_table, lam_p, subln, qbd, *([cache] * pg), newkv)


def _compress_kernel(k_ref, v_ref, wp_ref, wc_ref, ko_ref, vo_ref):
    for c, (src, dst) in enumerate(((k_ref, ko_ref), (v_ref, vo_ref))):
        x = src[...]
        t = x.shape[0]
        y = jnp.sum(x.reshape(t // BLOCK, BLOCK, LANES) * wp_ref[c][None], axis=1)
        dst[...] = _dot(y, wc_ref[c], precision=HIGHEST)


def _compress_prompt(zf, wp, wc, b, s):
    nc = s // BLOCK
    kc0 = S4 // LANES
    out = jax.ShapeDtypeStruct((b * nc, LANES), F32)
    return pl.pallas_call(
        _compress_kernel,
        out_shape=(out, out),
        grid=(b,),
        in_specs=[pl.BlockSpec((s, LANES), lambda i: (i, kc0)), pl.BlockSpec((s, LANES), lambda i: (i, kc0 + 1)),
                  pl.BlockSpec((2, BLOCK, LANES), lambda i: (0, 0, 0)),
                  pl.BlockSpec((2, LANES, LANES), lambda i: (0, 0, 0))],
        out_specs=(pl.BlockSpec((nc, LANES), lambda i: (i, 0)), pl.BlockSpec((nc, LANES), lambda i: (i, 0))),
        compiler_params=_cparams(("parallel",)),
        name="nsa_compress",
    )(zf, zf, wp, wc)


def _masked_probs(s, ok):
    s = jnp.where(ok, s, NEG)
    e = jnp.where(ok, jnp.exp(s - jnp.max(s, axis=-1, keepdims=True)), 0.0)
    den = jnp.sum(e, axis=-1, keepdims=True)
    return e / jnp.where(den > 0.0, den, 1.0)


def _select_blocks(imp, pos, ns):
    width = imp.shape[1]
    blk = lax.broadcasted_iota(I32, (1, width), 1)
    cur = lax.shift_right_logical(pos, BLOCK_SHIFT)
    forced = (blk == 0) | (blk == cur) | (blk == cur - 1)
    score = jnp.where(forced, imp + FORCE_BONUS, imp)
    score = jnp.where(blk <= cur, score, -1.0)
    score = jnp.where(blk < ns, score, 0.5 * LOWEST)
    sel = jnp.zeros(imp.shape, F32)
    for _ in range(N_SELECT):
        mx = jnp.max(score, axis=-1, keepdims=True)
        idx = jnp.min(jnp.where(score == mx, blk, width), axis=-1, keepdims=True)
        hit = blk == idx
        sel = jnp.where(hit, 1.0, sel)
        score = jnp.where(hit, LOWEST, score)
    return sel


def _half_mask(half):
    upper = lax.broadcasted_iota(I32, (1, LANES), 1) >= NSA_DH
    return upper if half == 1 else jnp.logical_not(upper)


def _head_q(q_ref, rows, r):
    q = q_ref[rows, (r // 2) * LANES:(r // 2 + 1) * LANES]
    return jnp.where(_half_mask(r % 2), q, jnp.zeros_like(q))


def _cmp_select_kernel(q_ref, kc_ref, vc_ref, base_ref, o_ref, sel_ref, *, tq, nc, ns, width):
    del base_ref
    qi = pl.program_id(1)
    pos = qi * tq + lax.broadcasted_iota(I32, (tq, 1), 0)
    blk = lax.broadcasted_iota(I32, (1, width), 1)
    cmp_ok = (blk < nc) & ((blk + 1) * BLOCK - 1 <= pos)
    hw = NSA_REP * NSA_DH
    for g in range(NSA_KV):
        kc = kc_ref[0, g]
        imp = jnp.zeros((tq, width), F32)
        for j in range(NSA_REP // 2):
            out = None
            for a in range(2):
                q = q_ref[:, g * hw + j * LANES:g * hw + (j + 1) * LANES]
                q = jnp.where(_half_mask(a), q, jnp.zeros_like(q))
                p = _masked_probs(_nt(q, kc), cmp_ok)
                imp = imp + p
                term = _dot(p.astype(BF16), vc_ref[0, g, a])
                out = term if out is None else out + term
            o_ref[:, g * hw + j * LANES:g * hw + (j + 1) * LANES] = out
        sel_ref[0, g] = _select_blocks(imp, pos, ns).astype(sel_ref.dtype)


def _cmp_decode_kernel(pt_ref, q_ref, wt_ref, wct_ref, *refs, nsteps, pg, page, past, t4, nc, ns, width):
    del pt_ref
    c_refs = refs[:pg]
    o_ref, sel_ref, acc = refs[pg:]
    p = pl.program_id(1)
    kw = NSA_KV * NSA_DH
    cw = acc.shape[1]

    @pl.when(p == 0)
    def _():
        acc[...] = jnp.zeros(acc.shape, F32)

    z = jnp.concatenate([c_refs[u][0].reshape(2 * kw, page) * wt_ref[...] for u in range(pg)], axis=1)
    tokblk = lax.shift_right_logical(lax.broadcasted_iota(I32, (pg * page, cw), 0), BLOCK_SHIFT)
    lane = lax.broadcasted_iota(I32, (pg * page, cw), 1)
    place = jnp.where(lane == p * (pg * page // BLOCK) + tokblk, 1.0, 0.0).astype(BF16)
    total = None
    for _ in range(3):
        piece = z.astype(BF16)
        z = z - piece.astype(F32)
        term = _dot(piece, place)
        total = term if total is None else total + term
    acc[...] += total

    @pl.when(p == nsteps - 1)
    def _():
        y = acc[...]
        rows = t4 * NSA_REP
        pos = past + lax.broadcasted_iota(I32, (rows, 1), 0) // NSA_REP
        blk = lax.broadcasted_iota(I32, (1, cw), 1)
        cmp_ok = (blk < nc) & ((blk + 1) * BLOCK - 1 <= pos)
        for g in range(NSA_KV):
            kct = _dot(wct_ref[0], y[g * NSA_DH:(g + 1) * NSA_DH], precision=HIGHEST)
            vct = _dot(wct_ref[1], y[kw + g * NSA_DH:kw + (g + 1) * NSA_DH], precision=HIGHEST)
            pr = _masked_probs(_dot(q_ref[0, g], kct.astype(BF16)), cmp_ok)
            o_ref[0, g] = _nt(pr.astype(BF16), vct.astype(BF16))
            imp = jnp.sum(pr.reshape(t4, NSA_REP, cw), axis=1)
            if width > cw:
                imp = jnp.concatenate([imp, jnp.zeros((t4, width - cw), F32)], axis=1)
            tpos = past + lax.broadcasted_iota(I32, (t4, 1), 0)
            sel_ref[0, g] = _select_blocks(imp, tpos, ns).astype(sel_ref.dtype)


def _cmp_decode(page_table, qst, cache_t, wt, wct, *, past, t4, nc, ns, width):
    nb, npages = page_table.shape
    page = cache_t.shape[4]
    pg, nsteps = _page_steps(npages)
    rows = t4 * NSA_REP
    cw = _round_up(nc, LANES)
    kw = NSA_KV * NSA_DH
    grid_spec = pltpu.PrefetchScalarGridSpec(
        num_scalar_prefetch=1, grid=(nb, nsteps),
        in_specs=[pl.BlockSpec((1, NSA_KV, rows, NSA_DH), lambda b, p, pt: (b, 0, 0, 0)),
                  pl.BlockSpec((2 * kw, page), lambda b, p, pt: (0, 0)),
                  pl.BlockSpec((2, NSA_DH, NSA_DH), lambda b, p, pt: (0, 0, 0))]
        + _page_specs((1, 2, NSA_KV, NSA_DH, page), npages, pg, 0),
        out_specs=(pl.BlockSpec((1, NSA_KV, rows, NSA_DH), lambda b, p, pt: (b, 0, 0, 0)),
                   pl.BlockSpec((1, NSA_KV, t4, width), lambda b, p, pt: (b, 0, 0, 0))),
        scratch_shapes=[pltpu.VMEM((2 * kw, cw), F32)])
    return pl.pallas_call(
        functools.partial(_cmp_decode_kernel, nsteps=nsteps, pg=pg, page=page, past=past, t4=t4,
                          nc=nc, ns=ns, width=width),
        out_shape=(jax.ShapeDtypeStruct((nb, NSA_KV, rows, NSA_DH), F32),
                   jax.ShapeDtypeStruct((nb, NSA_KV, t4, width), BF16)),
        grid_spec=grid_spec,
        compiler_params=_cparams(("parallel", "arbitrary")),
        name="nsa_cmp_decode",
    )(page_table, qst, wt, wct, *([cache_t] * pg))


def _cmp_select(zb, kc2, vc2, base, b, s, *, nc, ns):
    width = kc2.shape[2]
    tq = _pick_tile(s, 256, 16)
    nq = s // tq
    qcol = S3 // C_NSA_Q
    return pl.pallas_call(
        functools.partial(_cmp_select_kernel, tq=tq, nc=nc, ns=ns, width=width),
        out_shape=(jax.ShapeDtypeStruct(base.shape, base.dtype),
                   jax.ShapeDtypeStruct((b, NSA_KV, s, width), BF16)),
        grid=(b, nq),
        in_specs=[pl.BlockSpec((tq, C_NSA_Q), lambda bi, qi: (bi * nq + qi, qcol)),
                  pl.BlockSpec((1, NSA_KV, width, LANES), lambda bi, qi: (bi, 0, 0, 0)),
                  pl.BlockSpec((1, NSA_KV, 2, width, LANES), lambda bi, qi: (bi, 0, 0, 0, 0)),
                  pl.BlockSpec(memory_space=pl.ANY)],
        out_specs=(pl.BlockSpec((tq, C_NSA_Q), lambda bi, qi: (bi * nq + qi, 0)),
                   pl.BlockSpec((1, NSA_KV, tq, width), lambda bi, qi: (bi, 0, qi, 0))),
        input_output_aliases={3: 0},
        compiler_params=_cparams(("parallel", "parallel")),
        name="nsa_cmp_select",
    )(zb, kc2, vc2, base)


def _nsa_flash_kernel(qi_ref, ki_ref, *refs, mode, tq, rb, kb, nsteps):
    if mode == "slc":
        q_ref, k_ref, v_ref, sel_ref, e_ref, _, o_ref, m_scr, acc_scr = refs
    else:
        q_ref, k_ref, v_ref, _, o_ref, m_scr, acc_scr = refs
    t = pl.program_id(2)
    qi = qi_ref[t]
    ki = ki_ref[t]
    first = (t == 0) | (qi_ref[jnp.maximum(t - 1, 0)] != qi)
    last = (t == nsteps - 1) | (qi_ref[jnp.minimum(t + 1, nsteps - 1)] != qi)

    @pl.when(first)
    def _():
        m_scr[...] = jnp.full(m_scr.shape, NEG, F32)
        acc_scr[...] = jnp.zeros(acc_scr.shape, F32)

    if mode == "slc":
        picked = _dot(sel_ref[0, 0], e_ref[...])
    k = k_ref[...]
    v1 = v_ref[...]
    diag = qi == ki
    for r0 in range(0, tq, rb):
        rows = pl.ds(r0, rb)
        nk = _round_up(r0 + rb, LANES)

        def update(ncols, r0=r0, rows=rows):
            for c0 in range(0, ncols, kb):
                c1 = min(c0 + kb, ncols)
                d = ((qi - ki) * tq + r0 - c0 + lax.broadcasted_iota(I32, (rb, c1 - c0), 0)
                     - lax.broadcasted_iota(I32, (rb, c1 - c0), 1))
                if mode == "slc":
                    visible = (picked[r0:r0 + rb, c0:c1] > 0.5) & (d >= 0)
                else:
                    visible = (d >= 0) & (d < WINDOW)
                bias = jnp.where(visible, 0.0, NEG)
                for h in range(NSA_REP):
                    s = _nt(_head_q(q_ref, rows, h), k[c0:c1])
                    _flash_rows(s, bias, v1[c0:c1], m_scr, acc_scr, (h, rows))

        if nk < tq:
            pl.when(diag)(functools.partial(update, nk))
            pl.when(jnp.logical_not(diag))(functools.partial(update, tq))
        else:
            update(tq)

    @pl.when(last)
    def _():
        low = _half_mask(0)
        for j in range(NSA_REP // 2):
            a, bq = acc_scr[2 * j], acc_scr[2 * j + 1]
            num = jnp.where(low, a, pltpu.roll(bq, NSA_DH, 1))
            den = jnp.where(low, pltpu.roll(a, NSA_DH, 1), bq)
            o_ref[:, j * LANES:(j + 1) * LANES] = num / den


def _nsa_flash(zb, k2, v1, base, b, s, sel=None, emat=None):
    tq = _pick_tile(s, FLASH_TILE, 16)
    rb = _pick_tile(tq, FLASH_ROWS, 16)
    kb = _pick_tile(tq, FLASH_KEYS, LANES)
    nq = s // tq
    if sel is not None:
        mode = "slc"
        qtab, ktab = _tri_tables(nq)
    else:
        mode = "win"
        back = -(-WINDOW // tq)
        pairs = [(q, kk) for q in range(nq) for kk in range(max(q - back, 0), q + 1)]
        qtab = jnp.asarray([p[0] for p in pairs], I32)
        ktab = jnp.asarray([p[1] for p in pairs], I32)
    nsteps = qtab.shape[0]
    gw = NSA_REP * NSA_DH
    qcol = S3 // gw
    in_specs = [pl.BlockSpec((tq, gw), lambda bi, g, t, qt, kt: (bi * nq + qt[t], qcol + g)),
                pl.BlockSpec((tq, LANES), lambda bi, g, t, qt, kt: (bi * nq + kt[t], g)),
                pl.BlockSpec((tq, LANES), lambda bi, g, t, qt, kt: (bi * nq + kt[t], g))]
    args = [zb, k2, v1]
    if sel is not None:
        width = sel.shape[3]
        in_specs += [pl.BlockSpec((1, 1, tq, width), lambda bi, g, t, qt, kt: (bi, g, qt[t], 0)),
                     pl.BlockSpec((width, tq), lambda bi, g, t, qt, kt: (0, kt[t]))]
        args += [sel, emat]
    in_specs.append(pl.BlockSpec(memory_space=pl.ANY))
    args.append(base)
    grid_spec = pltpu.PrefetchScalarGridSpec(
        num_scalar_prefetch=2, grid=(b, NSA_KV, nsteps),
        in_specs=in_specs,
        out_specs=pl.BlockSpec((tq, gw), lambda bi, g, t, qt, kt: (bi * nq + qt[t], g)),
        scratch_shapes=[pltpu.VMEM((NSA_REP, tq, LANES), F32), pltpu.VMEM((NSA_REP, tq, LANES), F32)])
    return pl.pallas_call(
        functools.partial(_nsa_flash_kernel, mode=mode, tq=tq, rb=rb, kb=kb, nsteps=nsteps),
        out_shape=jax.ShapeDtypeStruct(base.shape, base.dtype),
        grid_spec=grid_spec,
        input_output_aliases={1 + len(args): 0},
        compiler_params=_cparams(("parallel", "parallel", "arbitrary")),
        name="nsa_" + mode,
    )(qtab, ktab, *args)


def _row_token(rows, cols, t4):
    row = lax.broadcasted_iota(I32, (rows, cols), 0)
    return (row // NSA_REP) % t4


def _kv_t(blocks):
    kw = NSA_KV * NSA_DH
    kt = [r[0, 0].reshape(kw, r.shape[4]) for r in blocks]
    vt = [r[0, 1].reshape(kw, r.shape[4]) for r in blocks]
    cat = lambda xs: (xs[0] if len(xs) == 1 else jnp.concatenate(xs, axis=1)).astype(BF16)
    return cat(kt), cat(vt)


def _slc_decode_kernel(pt_ref, q_ref, sel_ref, e_ref, *refs, nsteps, pg, page, past, t4):
    del pt_ref
    c_refs = refs[:pg]
    n_ref, o_ref, m_scr, l_scr, acc_scr = refs[pg:]
    p = pl.program_id(1)

    @pl.when(p == 0)
    def _():
        _flash_init(m_scr, l_scr, acc_scr)

    q = q_ref[0]
    rows = q.shape[0]

    def step(blocks):
        kt, vt = _kv_t(blocks)
        nk = kt.shape[1]
        kpos = p * (pg * page) + lax.broadcasted_iota(I32, (rows, nk), 1)
        picked = _dot(sel_ref[0], e_ref[:, :nk]) > 0.5
        mask = picked & (kpos <= past + _row_token(rows, nk, t4))
        _flash_update(_dot(q, kt), mask, vt, m_scr, l_scr, acc_scr, 0, v_is_transposed=True)

    @pl.when(p < nsteps)
    def _():
        step(c_refs)

    @pl.when(p == nsteps)
    def _():
        step([n_ref])
        o_ref[0] = acc_scr[0] / l_scr[0]


def _slc_decode(page_table, qnb, selrows, emat, cache_t, new_t, *, past, t4):
    nb, rows, _ = qnb.shape
    npages = page_table.shape[1]
    page = cache_t.shape[4]
    width = selrows.shape[2]
    pg, nsteps = _page_steps(npages)
    blk = (1, 2, NSA_KV, NSA_DH, page)
    grid_spec = pltpu.PrefetchScalarGridSpec(
        num_scalar_prefetch=1, grid=(nb, nsteps + 1),
        in_specs=[pl.BlockSpec((1, rows, LANES), lambda b, p, pt: (b, 0, 0)),
                  pl.BlockSpec((1, rows, width), lambda b, p, pt: (b, 0, 0)),
                  pl.BlockSpec((width, pg * page), lambda b, p, pt: (0, p))]
        + _page_specs(blk, npages, pg, 1)
        + [pl.BlockSpec(blk, lambda b, p, pt: (b, 0, 0, 0, 0))],
        out_specs=pl.BlockSpec((1, rows, LANES), lambda b, p, pt: (b, 0, 0)),
        scratch_shapes=[pltpu.VMEM((1, rows, 1), F32), pltpu.VMEM((1, rows, 1), F32),
                        pltpu.VMEM((1, rows, LANES), F32)])
    return pl.pallas_call(
        functools.partial(_slc_decode_kernel, nsteps=nsteps, pg=pg, page=page, past=past, t4=t4),
        out_shape=jax.ShapeDtypeStruct((nb, rows, LANES), F32),
        grid_spec=grid_spec,
        compiler_params=_cparams(("parallel", "arbitrary")),
        name="nsa_slc_decode",
    )(page_table, qnb, selrows, emat, *([cache_t] * pg), new_t)


def _win_decode_kernel(q_ref, c_ref, n_ref, o_ref, *, past, t4):
    q = q_ref[0]
    rows = q.shape[0]
    wb = c_ref.shape[4]
    m = jnp.full((rows, 1), NEG, F32)
    l = jnp.zeros((rows, 1), F32)
    acc = jnp.zeros((rows, LANES), F32)
    for ref, kstart in ((c_ref, past - wb), (n_ref, past)):
        kt, vt = _kv_t([ref])
        nk = kt.shape[1]
        d = past + _row_token(rows, nk, t4) - (kstart + lax.broadcasted_iota(I32, (rows, nk), 1))
        mask = (d >= 0) & (d < WINDOW)
        m, l, acc = _softmax_block(_dot(q, kt), mask, vt, m, l, acc, v_is_transposed=True)
    o_ref[0] = acc / l


def _win_decode(qnb, cwin_t, new_t, *, past, t4):
    nb, rows, _ = qnb.shape
    return pl.pallas_call(
        functools.partial(_win_decode_kernel, past=past, t4=t4),
        out_shape=jax.ShapeDtypeStruct((nb, rows, LANES), F32),
        grid=(nb,),
        in_specs=[pl.BlockSpec((1, rows, LANES), lambda b: (b, 0, 0)),
                  pl.BlockSpec((1,) + cwin_t.shape[1:], lambda b: (b, 0, 0, 0, 0)),
                  pl.BlockSpec((1,) + new_t.shape[1:], lambda b: (b, 0, 0, 0, 0))],
        out_specs=pl.BlockSpec((1, rows, LANES), lambda b: (b, 0, 0)),
        compiler_params=_cparams(("parallel",)),
        name="nsa_win_decode",
    )(qnb, cwin_t, new_t)


def _merge_kernel(oda_ref, oc_ref, os_ref, ow_ref, gn_ref, ga_ref, gb_ref, e_ref, wda_ref, wns_ref, o_ref):
    g = gn_ref[...]
    ghi = g.astype(BF16)
    glo = (g - ghi.astype(F32)).astype(BF16)
    ons = None
    for b, src in enumerate((oc_ref, os_ref, ow_ref)):
        ge = _dot(ghi, e_ref[b]) + _dot(glo, e_ref[b])
        term = ge * src[...]
        ons = term if ons is None else ons + term
    yda = _dot(oda_ref[...], wda_ref[...])
    yns = _dot(ons.astype(BF16), wns_ref[...])
    o_ref[...] = (ga_ref[...] * yda + gb_ref[...] * yns).astype(o_ref.dtype)


def _merge(oda, ocmp, oslc, owin, gns, gmg, emat, wda, wns):
    n, hd = oda.shape
    d = wda.shape[1]
    tm = _pick_tile(n, 320, 16)
    row = lambda i: (i, 0)
    const2 = lambda i: (0, 0)
    return pl.pallas_call(
        _merge_kernel,
        out_shape=jax.ShapeDtypeStruct((n, d), BF16),
        grid=(n // tm,),
        in_specs=[pl.BlockSpec((tm, hd), row), pl.BlockSpec((tm, hd), row), pl.BlockSpec((tm, hd), row),
                  pl.BlockSpec((tm, hd), row), pl.BlockSpec((tm, LANES), row),
                  pl.BlockSpec((tm, d), lambda i: (i, 0)), pl.BlockSpec((tm, d), lambda i: (i, 1)),
                  pl.BlockSpec((3, LANES, hd), lambda i: (0, 0, 0)),
                  pl.BlockSpec((hd, d), const2), pl.BlockSpec((hd, d), const2)],
        out_specs=pl.BlockSpec((tm, d), row),
        compiler_params=_cparams(("parallel",)),
        name="mixer_merge",
    )(oda, ocmp, oslc, owin, gns, gmg, gmg, emat, wda, wns)


def _lane_pick(x, lane, k):
    return jnp.sum(jnp.where(lane == k, x, 0.0), axis=-1, keepdims=True)


def _route(logits):
    lane = lax.broadcasted_iota(I32, logits.shape, 1)
    isg = lane < N_GROUPS
    gmax = jnp.max(jnp.where(isg, logits, LOWEST), axis=-1, keepdims=True)
    gsel = jnp.min(jnp.where(isg & (logits == gmax), lane, LANES), axis=-1, keepdims=True)
    pgrp = 1.0 / jnp.sum(jnp.where(isg, jnp.exp(logits - gmax), 0.0), axis=-1, keepdims=True)
    lo = N_GROUPS + gsel * EXP_PER_GROUP
    ing = (lane >= lo) & (lane < lo + EXP_PER_GROUP)
    v1 = jnp.max(jnp.where(ing, logits, LOWEST), axis=-1, keepdims=True)
    i1 = jnp.min(jnp.where(ing & (logits == v1), lane, LANES), axis=-1, keepdims=True)
    ing2 = ing & (lane != i1)
    v2 = jnp.max(jnp.where(ing2, logits, LOWEST), axis=-1, keepdims=True)
    i2 = jnp.min(jnp.where(ing2 & (logits == v2), lane, LANES), axis=-1, keepdims=True)
    t = jnp.exp(v2 - v1)
    w1 = pgrp / (1.0 + t)
    w2 = pgrp * t / (1.0 + t)
    e1 = (i1 - N_GROUPS).astype(F32)
    e2 = (i2 - N_GROUPS).astype(F32)
    return jnp.where(lane == 0, e1, jnp.where(lane == 1, e2, jnp.where(lane == 2, w1, jnp.where(lane == 3, w2, 0.0))))


def _store_rows(ref, y, rows, nchunk):
    for s in range(nchunk):
        ref[pl.ds(s, rows, stride=nchunk), :] = y[:, s * LANES:(s + 1) * LANES]


def _wo_route_kernel(m_ref, x_ref, wo_ref, gf_ref, wr_ref, br_ref, x1_ref, h_ref, rt_ref, *, nchunk):
    x1 = x_ref[...] + _dot(m_ref[...], wo_ref[...])
    x1_ref[...] = x1
    h = _rms(x1, gf_ref[...])
    _store_rows(h_ref, h, h.shape[0], nchunk)
    hi = h.astype(BF16)
    lo = (h - hi.astype(F32)).astype(BF16)
    w = wr_ref[...]
    w_hi = w.astype(BF16)
    w_lo = (w - w_hi.astype(F32)).astype(BF16)
    logits = _dot(hi, w_hi) + (_dot(hi, w_lo) + _dot(lo, w_hi))
    rt_ref[...] = _route(logits + br_ref[...])


def _wo_route(mrg, xa, wo, gffn, wr, br):
    n, d = xa.shape
    tm = _pick_tile(n, 320, 16)
    nchunk = d // LANES
    row = lambda i: (i, 0)
    const2 = lambda i: (0, 0)
    return pl.pallas_call(
        functools.partial(_wo_route_kernel, nchunk=nchunk),
        out_shape=(jax.ShapeDtypeStruct((n, d), F32), jax.ShapeDtypeStruct((n * nchunk, LANES), F32),
                   jax.ShapeDtypeStruct((n, LANES), F32)),
        grid=(n // tm,),
        in_specs=[pl.BlockSpec((tm, d), row), pl.BlockSpec((tm, d), row), pl.BlockSpec((d, d), const2),
                  pl.BlockSpec((1, d), const2), pl.BlockSpec((d, LANES), const2),
                  pl.BlockSpec((1, LANES), const2)],
        out_specs=(pl.BlockSpec((tm, d), row), pl.BlockSpec((tm * nchunk, LANES), row),
                   pl.BlockSpec((tm, LANES), row)),
        compiler_params=_cparams(("parallel",)),
        name="wo_route",
    )(mrg, xa, wo, gffn, wr, br)


def _onehots(rt):
    lane = lax.broadcasted_iota(I32, rt.shape, 1)
    lane_f = lane.astype(F32)
    oh0 = jnp.where(lane_f == _lane_pick(rt, lane, 0), 1.0, 0.0)
    oh1 = jnp.where(lane_f == _lane_pick(rt, lane, 1), 1.0, 0.0)
    return lane, oh0, oh1


def _rank_kernel(rt_ref, tri_ref, rk_ref, sz_ref, carry):
    @pl.when(pl.program_id(0) == 0)
    def _():
        carry[...] = jnp.zeros(carry.shape, F32)

    lane, oh0, oh1 = _onehots(rt_ref[...])
    oh = oh0 + oh1
    before = _dot(tri_ref[...], oh.astype(BF16)) + carry[0:1, :]
    r0 = jnp.sum(before * oh0, axis=-1, keepdims=True)
    r1 = jnp.sum(before * oh1, axis=-1, keepdims=True)
    rk_ref[...] = jnp.where(lane == 0, r0, jnp.where(lane == 1, r1, 0.0))
    carry[...] = carry[...] + jnp.sum(oh, axis=0, keepdims=True)
    sz_ref[...] = carry[...]


def _expert_ranks(route, tri):
    n = route.shape[0]
    tm = tri.shape[0]
    return pl.pallas_call(
        _rank_kernel,
        out_shape=(jax.ShapeDtypeStruct((n, LANES), F32), jax.ShapeDtypeStruct((8, LANES), F32)),
        grid=(n // tm,),
        in_specs=[pl.BlockSpec((tm, LANES), lambda i: (i, 0)), pl.BlockSpec((tm, tm), lambda i: (0, 0))],
        out_specs=(pl.BlockSpec((tm, LANES), lambda i: (i, 0)), pl.BlockSpec((8, LANES), lambda i: (0, 0))),
        scratch_shapes=[pltpu.VMEM((8, LANES), F32)],
        compiler_params=_cparams(("arbitrary",)),
        name="moe_rank",
    )(route, tri)


def _dest_kernel(rt_ref, rk_ref, sz_ref, dest_ref, be_ref, nu_ref, *, shift):
    lane8 = lax.broadcasted_iota(I32, (8, LANES), 1)
    sizes = sz_ref[...].astype(I32)
    blk = 1 << shift
    padded = jnp.where(lane8 < N_EXPERTS, lax.shift_right_logical(sizes + (blk - 1), shift) * blk, 0)
    ends = padded
    d = 1
    while d < LANES:
        ends = ends + jnp.where(lane8 >= d, pltpu.roll(ends, d, 1), 0)
        d *= 2
    pstart = (ends - padded)[0:1, :].astype(F32)
    rt = rt_ref[...]
    lane, oh0, oh1 = _onehots(rt)
    rk = rk_ref[...]
    d0 = jnp.sum(oh0 * pstart, axis=-1, keepdims=True) + _lane_pick(rk, lane, 0)
    d1 = jnp.sum(oh1 * pstart, axis=-1, keepdims=True) + _lane_pick(rk, lane, 1)
    dest_ref[...] = jnp.where(lane == 0, d0, jnp.where(lane == 1, d1, 0.0)).astype(I32)
    nbt = be_ref.shape[0]
    lane_b = lax.broadcasted_iota(I32, (nbt, LANES), 1)
    first_row = lax.broadcasted_iota(I32, (nbt, LANES), 0) * blk
    cnt = jnp.sum(jnp.where((lane_b < N_EXPERTS) & (ends[0:1, :] <= first_row), 1, 0), axis=-1, keepdims=True)
    be_ref[...] = jnp.broadcast_to(jnp.minimum(cnt, N_EXPERTS - 1), (nbt, LANES))
    total = jnp.sum(jnp.where(lane8 == N_EXPERTS - 1, ends, 0), axis=-1, keepdims=True)
    nu_ref[...] = jnp.broadcast_to(lax.shift_right_logical(total, shift), (8, LANES))


def _destinations(route, rank, sizes, nbt, shift):
    n = route.shape[0]
    tm = _pick_tile(n, 640, 8)
    nbt_p = _round_up(nbt, 8)
    return pl.pallas_call(
        functools.partial(_dest_kernel, shift=shift),
        out_shape=(jax.ShapeDtypeStruct((n, LANES), I32), jax.ShapeDtypeStruct((nbt_p, LANES), I32),
                   jax.ShapeDtypeStruct((8, LANES), I32)),
        grid=(n // tm,),
        in_specs=[pl.BlockSpec((tm, LANES), lambda i: (i, 0)), pl.BlockSpec((tm, LANES), lambda i: (i, 0)),
                  pl.BlockSpec((8, LANES), lambda i: (0, 0))],
        out_specs=(pl.BlockSpec((tm, LANES), lambda i: (i, 0)), pl.BlockSpec((nbt_p, LANES), lambda i: (0, 0)),
                   pl.BlockSpec((8, LANES), lambda i: (0, 0))),
        compiler_params=_cparams(("arbitrary",)),
        name="moe_dest",
    )(route, rank, sizes)


def _expert_kernel(dest_ref, be_ref, nu_ref, h_ref, wg_ref, wu_ref, wd_ref, o_ref,
                   tok, ordinal, xbuf, sem, wgf, wuf, wdf, wsem, wgb, wub, wdb, *, rows, nchunk, n_assign):
    j = pl.program_id(0)
    n_used = nu_ref[0]

    def weight_copies(e, ws):
        return (pltpu.make_async_copy(wg_ref.at[e], wgf.at[ws], wsem.at[ws, 0]),
                pltpu.make_async_copy(wu_ref.at[e], wuf.at[ws], wsem.at[ws, 1]),
                pltpu.make_async_copy(wd_ref.at[e], wdf.at[ws], wsem.at[ws, 2]))

    def row_copy(tile, slot, r):
        return pltpu.make_async_copy(h_ref.at[tok[tile * rows + r]],
                                     xbuf.at[slot, pl.ds(r * nchunk, nchunk)], sem.at[slot])

    def start_tile(tile, slot):
        def body(r, carry):
            row_copy(tile, slot, r).start()
            return carry
        lax.fori_loop(0, rows, body, 0)

    def wait_tile(tile, slot):
        def body(r, carry):
            row_copy(tile, slot, r).wait()
            return carry
        lax.fori_loop(0, rows, body, 0)

    @pl.when(j == 0)
    def _():
        for c in weight_copies(be_ref[0], 0):
            c.start()
        ordinal[0] = 0

        def clear(r, carry):
            tok[r] = 0
            return carry
        lax.fori_loop(0, tok.shape[0], clear, 0, unroll=8)

        def place(a, carry):
            tok[dest_ref[a]] = lax.shift_right_logical(a, 1)
            return carry
        lax.fori_loop(0, n_assign, place, 0, unroll=8)
        start_tile(0, 0)

    slot = lax.rem(j, 2)
    active = j < n_used
    expert = be_ref[j]
    changed = (j == 0) | (expert != be_ref[jnp.maximum(j - 1, 0)])

    @pl.when(active & changed)
    def _():
        ws = lax.rem(ordinal[0], 2)
        for c in weight_copies(expert, ws):
            c.wait()
        wgb[...] = wgf[ws].astype(BF16)
        wub[...] = wuf[ws].astype(BF16)
        wdb[...] = wdf[ws].astype(BF16)
        nxt = lax.while_loop(lambda t: (t < n_used) & (be_ref[jnp.minimum(t, n_used - 1)] == expert),
                             lambda t: t + 1, j + 1)

        @pl.when(nxt < n_used)
        def _():
            for c in weight_copies(be_ref[nxt], 1 - ws):
                c.start()

        ordinal[0] = ordinal[0] + 1

    @pl.when(active)
    def _():
        @pl.when(j + 1 < n_used)
        def _():
            start_tile(j + 1, 1 - slot)

        wait_tile(j, slot)
        x = jnp.concatenate([xbuf[slot, pl.ds(s, rows, stride=nchunk), :] for s in range(nchunk)],
                            axis=1).astype(BF16)
        a = _dot(x, wgb[...])
        u = _dot(x, wub[...])
        y = _dot((jax.nn.silu(a) * u).astype(BF16), wdb[...])
        _store_rows(o_ref, y, rows, nchunk)

    @pl.when(jnp.logical_not(active))
    def _():
        o_ref[...] = jnp.zeros(o_ref.shape, F32)


def _experts(dest, blk_e, n_used, h3, wg, wu, wd, rows):
    d, de = wg.shape[1], wg.shape[2]
    nchunk = d // LANES
    nbt = blk_e.shape[0]
    grid_spec = pltpu.PrefetchScalarGridSpec(
        num_scalar_prefetch=3, grid=(nbt,),
        in_specs=[pl.BlockSpec(memory_space=pl.ANY)] * 4,
        out_specs=pl.BlockSpec((rows * nchunk, LANES), lambda j, dst, be, nu: (j, 0)),
        scratch_shapes=[pltpu.SMEM((nbt * rows,), I32), pltpu.SMEM((1,), I32),
                        pltpu.VMEM((2, rows * nchunk, LANES), F32), pltpu.SemaphoreType.DMA((2,)),
                        pltpu.VMEM((2, d, de), F32), pltpu.VMEM((2, d, de), F32), pltpu.VMEM((2, de, d), F32),
                        pltpu.SemaphoreType.DMA((2, 3)),
                        pltpu.VMEM((d, de), BF16), pltpu.VMEM((d, de), BF16), pltpu.VMEM((de, d), BF16)])
    return pl.pallas_call(
        functools.partial(_expert_kernel, rows=rows, nchunk=nchunk, n_assign=dest.shape[0]),
        out_shape=jax.ShapeDtypeStruct((nbt * rows * nchunk, LANES), F32),
        grid_spec=grid_spec,
        compiler_params=_cparams(("arbitrary",)),
        name="moe_experts",
    )(dest, blk_e, n_used, h3, wg, wu, wd)


def _combine_kernel(dest_ref, ys_ref, rt_ref, x1_ref, p_ref, gp_ref, wpg_ref, wpl_ref, gfin_ref, o_ref,
                    gbuf, sem, *, tm, nchunk, final):
    i = pl.program_id(0)
    slot = lax.rem(i, 2)

    def copy(tile, sl, j):
        return pltpu.make_async_copy(ys_ref.at[dest_ref[tile * (2 * tm) + j]],
                                     gbuf.at[sl, pl.ds(j * nchunk, nchunk)], sem.at[sl])

    def start_tile(tile, sl):
        def body(j, carry):
            copy(tile, sl, j).start()
            return carry
        lax.fori_loop(0, 2 * tm, body, 0)

    @pl.when(i == 0)
    def _():
        start_tile(0, 0)

    @pl.when(i + 1 < pl.num_programs(0))
    def _():
        start_tile(i + 1, 1 - slot)

    def drain(j, carry):
        copy(i, slot, j).wait()
        return carry

    lax.fori_loop(0, 2 * tm, drain, 0)
    rt = rt_ref[...]
    lane = lax.broadcasted_iota(I32, rt.shape, 1)
    y0 = jnp.concatenate([gbuf[slot, pl.ds(s, tm, stride=2 * nchunk), :] for s in range(nchunk)], axis=1)
    y1 = jnp.concatenate([gbuf[slot, pl.ds(nchunk + s, tm, stride=2 * nchunk), :] for s in range(nchunk)], axis=1)
    x2 = x1_ref[...] + (_lane_pick(rt, lane, 2) * y0 + _lane_pick(rt, lane, 3) * y1)
    gate = jax.nn.sigmoid(_dot(_rms(x2, gp_ref[...]).astype(BF16), wpg_ref[...]))
    x3 = x2 + gate * _dot(p_ref[...].astype(BF16), wpl_ref[...])
    o_ref[...] = _rms(x3, gfin_ref[...]) if final else x3


def _combine(dest, ys3, route, x1, pa, gple, wpg, wpl, gfin, final):
    n, d = x1.shape
    nchunk = d // LANES
    tm = _pick_tile(n, 320, 16)
    dp = pa.shape[1]
    row = lambda i, dst: (i, 0)
    const2 = lambda i, dst: (0, 0)
    grid_spec = pltpu.PrefetchScalarGridSpec(
        num_scalar_prefetch=1, grid=(n // tm,),
        in_specs=[pl.BlockSpec(memory_space=pl.ANY),
                  pl.BlockSpec((tm, LANES), row), pl.BlockSpec((tm, d), row), pl.BlockSpec((tm, dp), row),
                  pl.BlockSpec((1, d), const2), pl.BlockSpec((d, d), const2), pl.BlockSpec((dp, d), const2),
                  pl.BlockSpec((1, d), const2)],
        out_specs=pl.BlockSpec((tm, d), row),
        scratch_shapes=[pltpu.VMEM((2, 2 * tm * nchunk, LANES), F32), pltpu.SemaphoreType.DMA((2,))])
    return pl.pallas_call(
        functools.partial(_combine_kernel, tm=tm, nchunk=nchunk, final=final),
        out_shape=jax.ShapeDtypeStruct((n, d), F32),
        grid_spec=grid_spec,
        compiler_params=_cparams(("arbitrary",)),
        name="moe_combine_tail",
    )(dest, ys3, route, x1, pa, gple, wpg, wpl, gfin)


def _rope_tables(pos):
    inv = ROPE_THETA ** (-jnp.arange(ROPE_HALF, dtype=F32) / ROPE_HALF)
    ang = pos.astype(F32)[:, None] * inv[None, :]
    cos, sin = jnp.cos(ang), jnp.sin(ang)
    n = pos.shape[0]
    rest = DA_DH - 2 * ROPE_HALF
    z = jnp.zeros((n, ROPE_HALF), F32)
    zr = jnp.zeros((n, rest), F32)
    c64 = jnp.concatenate([cos, cos, jnp.ones((n, rest), F32)], axis=1)
    a64 = jnp.concatenate([-sin, z, zr], axis=1)
    b64 = jnp.concatenate([z, sin, zr], axis=1)
    rep = LANES // DA_DH
    return jnp.tile(c64, (1, rep)), jnp.tile(a64, (1, rep)), jnp.tile(b64, (1, rep))


def _column_tables():
    ones = lambda n: jnp.ones((n,), F32)
    zeros = lambda n: jnp.zeros((n,), F32)
    gw = NSA_KV * NSA_DH
    rmask = jnp.concatenate([ones(C_DA_Q), ones(C_DA_K), zeros(C_DA_V), ones(C_NSA_Q)]
                            + [ones(gw), zeros(gw)] * 3)
    cscale = jnp.concatenate([ones(C_DA_Q) * DA_DH ** -0.5, ones(C_DA_K), ones(C_DA_V),
                              ones(C_NSA_Q) * NSA_DH ** -0.5, ones(C_NSA_KV)])
    return rmask[None, :], cscale[None, :]


def _block_expand(width, nkeys):
    blk = jnp.arange(width, dtype=I32)[:, None]
    key = jnp.arange(nkeys, dtype=I32)[None, :]
    return (key // BLOCK == blk).astype(BF16)


def _pad_rows(x, axis, size):
    pad = [(0, 0)] * x.ndim
    pad[axis] = (0, size - x.shape[axis])
    return jnp.pad(x, pad)


def _cmp_layout(kc, b, nc, width):
    x = kc.reshape(b, nc, NSA_KV, NSA_DH).transpose(0, 2, 1, 3)
    return _pad_rows(x, 2, width).astype(BF16)


def kernel(x_prompt, x_sample, cache_da_kv, cache_nsa_kv, cache_win_kv, page_table, p_prompt, p_sample,
           g_mix, w_in, da_lambda, da_subln, nsa_cpos, nsa_cmp, w_da_up, w_nsa_up, w_o, g_ffn,
           w_rg, b_rg, w_re, b_re, w_e_gate, w_e_up, w_e_down, g_ple, w_ple_gate, w_ple, g_final):
    b, s, d = x_prompt.shape
    nb, t4, _ = x_sample.shape
    depth = g_mix.shape[0]
    npr, nsm = b * s, nb * t4
    n = npr + nsm
    page = cache_da_kv.shape[2]
    npages = page_table.shape[1]
    past = npages * page
    wb = cache_win_kv.shape[2]
    nchunk = d // LANES
    rq = t4 * DA_REP
    assert rq % 8 == 0 and page % BLOCK == 0 and s % BLOCK == 0 and t4 <= page and d % LANES == 0

    xa = jnp.concatenate([x_prompt.reshape(npr, d), x_sample.reshape(nsm, d)], axis=0)
    pos = jnp.concatenate([jnp.tile(jnp.arange(s, dtype=I32), b), jnp.tile(past + jnp.arange(t4, dtype=I32), nb)])
    cos, sa, sb = _rope_tables(pos)
    rmask, cscale = _column_tables()

    nc_p, ns_p = s // BLOCK, max(-(-s // BLOCK), N_SELECT)
    tot_s = past + t4
    nc_s, ns_s = tot_s // BLOCK, max(-(-tot_s // BLOCK), N_SELECT)
    wid_p, wid_s = _round_up(ns_p, LANES), _round_up(ns_s, LANES)
    emat_p = _block_expand(wid_p, s)
    pg_s, nsteps_s = _page_steps(npages)
    emat_s = _block_expand(wid_s, (nsteps_s + 1) * pg_s * page)

    col = jnp.arange(LANES, dtype=I32)[:, None]
    lane = jnp.arange(NSA_HEADS * NSA_DH, dtype=I32)[None, :]
    gate_expand = jnp.stack([(col == br * NSA_HEADS + lane // NSA_DH) for br in range(3)]).astype(BF16)

    n_assign = 2 * n
    shift = int(math.log2(EXPERT_ROWS))
    nbt = -(-n_assign // EXPERT_ROWS) + N_EXPERTS
    tm_rank = _pick_tile(n, 640, 16)
    tri = jnp.tril(jnp.ones((tm_rank, tm_rank), BF16), -1)

    outs = {k: [] for k in ("da_p", "da_s", "ns_p", "ns_s", "win_p", "win_s")}
    for i in range(depth):
        lam_init = 0.8 - 0.6 * math.exp(-0.3 * i)
        wi = w_in[i]
        g1 = g_mix[i][None, :]
        zf, zb = _proj_rope(xa, g1, wi[:, :S5].astype(BF16), cos, sa, sb, rmask, cscale)
        wg_cols = wi[:, S5:S6].reshape(d, NSA_HEADS, 3).transpose(0, 2, 1).reshape(d, C_NSA_G)
        gns = _gate_proj(xa, g1, _pad_rows(wg_cols, 1, LANES).astype(BF16))
        gmg = _gate_proj(xa, g1, wi[:, S6:].astype(BF16))
        lam_p = da_lambda[i].astype(F32)
        subln = da_subln[i][None, :].astype(F32)
        wp = jnp.tile(nsa_cpos[i].astype(F32), (1, 1, NSA_KV))
        wcm = nsa_cmp[i].astype(F32)
        zc = jnp.zeros_like(wcm)
        wc = jnp.concatenate([jnp.concatenate([wcm, zc], axis=2), jnp.concatenate([zc, wcm], axis=2)], axis=1)

        zs = zb[npr:].reshape(nb, t4, S5)
        zfs = zf[npr:].reshape(nb, t4, S5)
        q6 = zs[..., :S1].reshape(nb, t4, DA_KV, DA_REP, 2, DA_DH).transpose(0, 2, 4, 1, 3, 5)
        eye_g = jnp.eye(DA_KV, dtype=BF16)
        eye_c = jnp.eye(2, dtype=BF16)
        qbd = (q6[:, :, :, :, :, None, None, :] * eye_g[None, :, None, None, None, :, None, None]
               * eye_c[None, None, :, None, None, None, :, None]).reshape(nb, DA_KV * 2 * rq, DA_KV * 2 * DA_DH)
        cda = cache_da_kv[i].reshape(cache_da_kv.shape[1], page * 2 * DA_KV, DA_DV)
        new_da = _pad_rows(zfs[..., S1:S3], 1, page).reshape(nb, page * 2 * DA_KV, DA_DV)
        oda_s = _da_decode(page_table, qbd, cda, new_da, lam_p, subln, lam_init)
        oda_s = oda_s.reshape(nb, DA_KV, t4, DA_REP, DA_DV).transpose(0, 2, 1, 3, 4).reshape(nsm, DA_HEADS * DA_DV)

        qns = zs[..., S3:S4].reshape(nb, t4, NSA_KV, NSA_REP, NSA_DH).transpose(0, 2, 1, 3, 4)
        eye_n = jnp.eye(NSA_KV, dtype=BF16)
        qnb = (qns[:, :, :, :, None, :] * eye_n[None, :, None, None, :, None]
               ).reshape(nb, NSA_KV * t4 * NSA_REP, LANES)
        cns_t = cache_nsa_kv[i].transpose(0, 2, 3, 4, 1)
        cwin_t = cache_win_kv[i].transpose(0, 2, 3, 4, 1)
        kw = NSA_KV * NSA_DH
        wt = jnp.tile(nsa_cpos[i].astype(F32).transpose(0, 2, 1)[:, None],
                      (1, NSA_KV, 1, page // BLOCK)).reshape(2 * kw, page)
        ocmp_s, sel_s = _cmp_decode(page_table, qns.reshape(nb, NSA_KV, t4 * NSA_REP, NSA_DH), cns_t, wt,
                                    nsa_cmp[i].astype(F32).transpose(0, 2, 1),
                                    past=past, t4=t4, nc=nc_s, ns=ns_s, width=wid_s)
        ocmp_s = ocmp_s.reshape(nb, NSA_KV, t4, NSA_REP, NSA_DH).transpose(0, 2, 1, 3, 4
                                                                             ).reshape(nsm, NSA_HEADS * NSA_DH)
        selrows = jnp.broadcast_to(sel_s[:, :, :, None, :], (nb, NSA_KV, t4, NSA_REP, wid_s)
                                   ).reshape(nb, NSA_KV * t4 * NSA_REP, wid_s)
        new_t = lambda x: _pad_rows(x.reshape(nb, t4, 2, NSA_KV, NSA_DH), 1, page).transpose(0, 2, 3, 4, 1)
        oslc_s = _slc_decode(page_table, qnb, selrows, emat_s, cns_t, new_t(zfs[..., S4 + 2 * kw:S4 + 4 * kw]),
                             past=past, t4=t4)
        owin_s = _win_decode(qnb, cwin_t, new_t(zfs[..., S4 + 4 * kw:S5]), past=past, t4=t4)

        def halves(o):
            o = o.reshape(nb, NSA_KV, t4, NSA_REP, NSA_KV, NSA_DH)
            o = jnp.stack([o[:, g, :, :, g] for g in range(NSA_KV)], axis=2)
            return o.reshape(nsm, NSA_HEADS * NSA_DH)

        oslc_s, owin_s = halves(oslc_s), halves(owin_s)

        seeded = lambda rows: jnp.concatenate([jnp.zeros((npr, rows.shape[1]), rows.dtype), rows], axis=0)
        oda = _da_flash(zb, seeded(oda_s.astype(BF16)), b, s, lam_p, subln, lam_init)
        kc, vc = _compress_prompt(zf, wp, wc, b, s)
        kc2 = _cmp_layout(kc, b, nc_p, wid_p)
        kc2 = jnp.concatenate([kc2, kc2], axis=-1)
        vcl = _cmp_layout(vc, b, nc_p, wid_p)
        vz = jnp.zeros_like(vcl)
        vc2 = jnp.stack([jnp.concatenate([vcl, vz], axis=-1), jnp.concatenate([vz, vcl], axis=-1)], axis=2)
        ocmp, sel_p = _cmp_select(zb, kc2, vc2, seeded(ocmp_s), b, s, nc=nc_p, ns=ns_p)
        kvp = zb[:npr, S4:S5].reshape(npr, 6, NSA_KV, NSA_DH)
        dup = lambda k: jnp.concatenate([k, k], axis=-1).reshape(npr, NSA_KV * LANES)
        with_ones = lambda v: jnp.concatenate([v, jnp.ones_like(v)], axis=-1).reshape(npr, NSA_KV * LANES)
        oslc = _nsa_flash(zb, dup(kvp[:, 2]), with_ones(kvp[:, 3]), seeded(oslc_s), b, s, sel_p, emat_p)
        owin = _nsa_flash(zb, dup(kvp[:, 4]), with_ones(kvp[:, 5]), seeded(owin_s), b, s)

        mrg =_merge(oda, ocmp, oslc, owin, gns, gmg, gate_expand,
                     w_da_up[i].astype(BF16), w_nsa_up[i].astype(BF16))
        wr = _pad_rows(jnp.concatenate([w_rg[i], w_re[i]], axis=1).astype(F32), 1, LANES)
        br = _pad_rows(jnp.concatenate([b_rg[i], b_re[i]]).astype(F32)[None, :], 1, LANES)
        x1, hrows, route = _wo_route(mrg, xa, w_o[i].astype(BF16), g_ffn[i][None, :], wr, br)
        rank, sizes = _expert_ranks(route, tri)
        dest, be, nu = _destinations(route, rank, sizes, nbt, shift)
        dest_flat = dest[:, :2].reshape(n_assign)
        ys = _experts(dest_flat, be[:nbt, 0], nu[0, :1], hrows.reshape(n, nchunk, LANES),
                      w_e_gate[i], w_e_up[i], w_e_down[i], EXPERT_ROWS)
        pa = jnp.concatenate([p_prompt[i].reshape(npr, -1), p_sample[i].reshape(nsm, -1)], axis=0)
        xa = _combine(dest_flat, ys.reshape(nbt * EXPERT_ROWS, nchunk, LANES), route, x1, pa,
                      g_ple[i][None, :], w_ple_gate[i].astype(BF16), w_ple[i].astype(BF16),
                      g_final[None, :], i == depth - 1)

        zfp = zf[:npr].reshape(b, s, S5)
        outs["da_p"].append(zfp[..., S1:S3].reshape(b, s, 2, DA_KV, DA_DV))
        outs["da_s"].append(zfs[..., S1:S3].reshape(nb, t4, 2, DA_KV, DA_DV))
        outs["ns_p"].append(zfp[..., S4:S4 + 4 * kw].reshape(b, s, 4, NSA_KV, NSA_DH))
        outs["ns_s"].append(zfs[..., S4:S4 + 4 * kw].reshape(nb, t4, 4, NSA_KV, NSA_DH))
        win_p = zfp[..., S4 + 4 * kw:S5].reshape(b, s, 2, NSA_KV, NSA_DH)
        outs["win_p"].append(win_p[:, s - min(WINDOW, s):])
        win_s = jnp.concatenate([cache_win_kv[i], zfs[..., S4 + 4 * kw:S5].reshape(nb, t4, 2, NSA_KV, NSA_DH)], axis=1)
        outs["win_s"].append(win_s[:, win_s.shape[1] - wb:])

    y_prompt = xa[:npr].reshape(b, s, d)
    y_sample = xa[npr:].reshape(nb, t4, d)
    return (y_prompt, y_sample, jnp.stack(outs["da_p"]), jnp.stack(outs["da_s"]), jnp.stack(outs["ns_p"]),
            jnp.stack(outs["ns_s"]), jnp.stack(outs["win_p"]), jnp.stack(outs["win_s"]))
```

```python
import functools
import math

import jax
import jax.numpy as jnp
from jax import lax
from jax.experimental import pallas as pl
from jax.experimental.pallas import tpu as pltpu

F32 = jnp.float32
BF16 = jnp.bfloat16
I32 = jnp.int32
HIGHEST = lax.Precision.HIGHEST

DA_HEADS, DA_KV, DA_REP, DA_DH = 8, 4, 2, 64
DA_DV = 2 * DA_DH
NSA_HEADS, NSA_KV, NSA_REP, NSA_DH = 16, 2, 8, 64
BLOCK = 64
BLOCK_SHIFT = 6
N_SELECT = 16
WINDOW = 512
FORCE_BONUS = 1000.0
ROPE_THETA = 500000.0
ROPE_HALF = 8
N_GROUPS, EXP_PER_GROUP = 8, 8
N_EXPERTS = N_GROUPS * EXP_PER_GROUP
EPS = 1e-6
NEG = -1e30
LOWEST = -3e38

LANES = 128
VMEM_LIMIT_BYTES = 56 * 1024 * 1024

C_DA_Q = DA_HEADS * 2 * DA_DH
C_DA_K = DA_KV * 2 * DA_DH
C_DA_V = DA_KV * DA_DV
C_NSA_Q = NSA_HEADS * NSA_DH
C_NSA_KV = 6 * NSA_KV * NSA_DH
C_NSA_G = 3 * NSA_HEADS
S1 = C_DA_Q
S2 = S1 + C_DA_K
S3 = S2 + C_DA_V
S4 = S3 + C_NSA_Q
S5 = S4 + C_NSA_KV
S6 = S5 + C_NSA_G

EXPERT_ROWS = 128
DECODE_PAGES = 16
PROJ_TILE = 256
FLASH_TILE = 512
FLASH_ROWS = 128
FLASH_KEYS = 256


def _cparams(sem):
    return pltpu.CompilerParams(dimension_semantics=sem, vmem_limit_bytes=VMEM_LIMIT_BYTES)


def _pick_tile(n, target, mult):
    best = None
    for t in range(mult, min(n, target) + 1, mult):
        if n % t == 0:
            best = t
    return n if best is None else best


def _round_up(x, m):
    return (x + m - 1) // m * m


def _nt(a, b):
    return lax.dot_general(a, b, (((1,), (1,)), ((), ())), preferred_element_type=F32)


def _dot(a, b, **kw):
    return jnp.dot(a, b, preferred_element_type=F32, **kw)


def _rms(x, g):
    ms = jnp.mean(x * x, axis=-1, keepdims=True)
    return x * lax.rsqrt(ms + EPS) * g


def _proj_rope_kernel(x_ref, g_ref, w_ref, cos_ref, sa_ref, sb_ref, rm_ref, cs_ref,
                      of_ref, ob_ref, h_scr, *, n_chunks, kv_tiles):
    @pl.when(pl.program_id(1) == 0)
    def _():
        h_scr[...] = _rms(x_ref[...], g_ref[...]).astype(BF16)

    j = pl.program_id(1)
    z = _dot(h_scr[...], w_ref[...])
    c, sa, sb = cos_ref[...], sa_ref[...], sb_ref[...]
    outs = []
    for k in range(n_chunks):
        sl = slice(k * LANES, (k + 1) * LANES)
        zc = z[:, sl]
        rot = zc * c + pltpu.roll(zc, LANES - ROPE_HALF, 1) * sa + pltpu.roll(zc, ROPE_HALF, 1) * sb
        out = jnp.where(rm_ref[:, sl] > 0.5, rot, zc) * cs_ref[:, sl]
        ob_ref[:, sl] = out.astype(BF16)
        outs.append(out)

    @pl.when(((j >= kv_tiles[0]) & (j < kv_tiles[1])) | (j >= kv_tiles[2]))
    def _():
        for k, out in enumerate(outs):
            of_ref[:, k * LANES:(k + 1) * LANES] = out


def _proj_rope(xa, g, w, cos, sa, sb, rmask, cscale):
    n, d = xa.shape
    ca = w.shape[1]
    tm = _pick_tile(n, 640, 16)
    tn = PROJ_TILE
    a, bq, c = S1 // tn, S3 // tn, S4 // tn
    row = lambda i, j: (i, 0)
    col = lambda i, j: (0, j)
    fcol = lambda i, j: (i, jnp.where(j < c, jnp.clip(j - a, 0, bq - a), bq - a + j - c))
    return pl.pallas_call(
        functools.partial(_proj_rope_kernel, n_chunks=tn // LANES, kv_tiles=(a, bq, c)),
        out_shape=(jax.ShapeDtypeStruct((n, S3 - S1 + S5 - S4), F32), jax.ShapeDtypeStruct((n, ca), BF16)),
        grid=(n // tm, ca // tn),
        in_specs=[pl.BlockSpec((tm, d), row), pl.BlockSpec((1, d), lambda i, j: (0, 0)),
                  pl.BlockSpec((d, tn), col),
                  pl.BlockSpec((tm, LANES), row), pl.BlockSpec((tm, LANES), row),
                  pl.BlockSpec((tm, LANES), row),
                  pl.BlockSpec((1, tn), col), pl.BlockSpec((1, tn), col)],
        out_specs=(pl.BlockSpec((tm, tn), fcol),
                   pl.BlockSpec((tm, tn), lambda i, j: (i, j))),
        scratch_shapes=[pltpu.VMEM((tm, d), BF16)],
        compiler_params=_cparams(("parallel", "arbitrary")),
        name="proj_rope",
    )(xa, g, w, cos, sa, sb, rmask, cscale)


def _gate_kernel(x_ref, g_ref, w_ref, o_ref, h_scr):
    @pl.when(pl.program_id(1) == 0)
    def _():
        h_scr[...] = _rms(x_ref[...], g_ref[...]).astype(BF16)

    o_ref[...] = jax.nn.sigmoid(_dot(h_scr[...], w_ref[...]))


def _gate_proj(xa, g, w):
    n, d = xa.shape
    ca = w.shape[1]
    tm = _pick_tile(n, 640, 16)
    tn = _pick_tile(ca, 1024, LANES)
    return pl.pallas_call(
        _gate_kernel,
        out_shape=jax.ShapeDtypeStruct((n, ca), F32),
        grid=(n // tm, ca // tn),
        in_specs=[pl.BlockSpec((tm, d), lambda i, j: (i, 0)), pl.BlockSpec((1, d), lambda i, j: (0, 0)),
                  pl.BlockSpec((d, tn), lambda i, j: (0, j))],
        out_specs=pl.BlockSpec((tm, tn), lambda i, j: (i, j)),
        scratch_shapes=[pltpu.VMEM((tm, d), BF16)],
        compiler_params=_cparams(("parallel", "arbitrary")),
        name="gate_proj",
    )(xa, g, w)


def _softmax_block(s, mask, v, m_old, l_old, acc_old, v_is_transposed=False):
    if mask is not None:
        s = jnp.where(mask, s, NEG)
    m_new = jnp.maximum(m_old, jnp.max(s, axis=-1, keepdims=True))
    p = jnp.exp(s - m_new)
    if mask is not None:
        p = jnp.where(mask, p, 0.0)
    alpha = jnp.exp(m_old - m_new)
    l_new = alpha * l_old + jnp.sum(p, axis=-1, keepdims=True)
    pv = _nt(p.astype(v.dtype), v) if v_is_transposed else _dot(p.astype(v.dtype), v)
    return m_new, l_new, alpha * acc_old + pv


def _flash_update(s, mask, v, m_ref, l_ref, acc_ref, idx, v_is_transposed=False):
    m, l, acc = _softmax_block(s, mask, v, m_ref[idx], l_ref[idx], acc_ref[idx], v_is_transposed)
    m_ref[idx] = m
    l_ref[idx] = l
    acc_ref[idx] = acc


def _flash_init(m_ref, l_ref, acc_ref):
    m_ref[...] = jnp.full(m_ref.shape, NEG, F32)
    l_ref[...] = jnp.zeros(l_ref.shape, F32)
    acc_ref[...] = jnp.zeros(acc_ref.shape, F32)


def _da_lambda(lam_ref, lam_init):
    a = jnp.sum(lam_ref[0:1, :] * lam_ref[1:2, :], axis=-1, keepdims=True)
    b = jnp.sum(lam_ref[2:3, :] * lam_ref[3:4, :], axis=-1, keepdims=True)
    return jnp.exp(a) - jnp.exp(b) + lam_init


def _da_finish(a0, a1, lam, sub, lam_init):
    o = a0 - lam * a1
    return _rms(o, sub) * (1.0 - lam_init)


def _flash_rows(s, bias, v1, m_ref, acc_ref, idx):
    if bias is not None:
        s = s + bias
    m_old = m_ref[idx]
    m_new = jnp.maximum(m_old, jnp.max(s, axis=-1, keepdims=True))
    p = jnp.exp(s - jnp.tile(m_new, (1, s.shape[1] // LANES)))
    alpha = jnp.exp(m_old - m_new)
    acc_ref[idx] = jnp.tile(alpha, (1, v1.shape[1] // LANES)) * acc_ref[idx] + _dot(p.astype(v1.dtype), v1)
    m_ref[idx] = m_new


def _tri_tables(nq):
    qi = [q for q in range(nq) for _ in range(q + 1)]
    ki = [k for q in range(nq) for k in range(q + 1)]
    return jnp.asarray(qi, I32), jnp.asarray(ki, I32)


def _da_flash_kernel(qi_ref, ki_ref, lam_ref, sub_ref, q_ref, k_ref, v_ref, base_ref, o_ref, m_scr, acc_scr,
                     *, tq, rb, kb, lam_init):
    del base_ref
    t = pl.program_id(2)
    qi = qi_ref[t]
    ki = ki_ref[t]

    @pl.when(ki == 0)
    def _():
        m_scr[...] = jnp.full(m_scr.shape, NEG, F32)
        acc_scr[...] = jnp.zeros(acc_scr.shape, F32)

    upper = lax.broadcasted_iota(I32, (1, LANES), 1) >= DA_DH

    def step(diag):
        k = k_ref[...]
        v1 = jnp.concatenate([v_ref[...], jnp.ones((tq, DA_DV), BF16)], axis=1)
        for r in range(DA_REP):
            for r0 in range(0, tq, rb):
                nk = r0 + rb if diag else tq
                q = q_ref[pl.ds(r0, rb), r * DA_DV:(r + 1) * DA_DV]
                qc = [jnp.where(upper if c == 1 else jnp.logical_not(upper), q, jnp.zeros_like(q))
                      for c in range(2)]
                for c0 in range(0, nk, kb):
                    c1 = min(c0 + kb, nk)
                    bias = None
                    if diag and c1 > r0:
                        row = r0 - c0 + lax.broadcasted_iota(I32, (rb, c1 - c0), 0)
                        bias = jnp.where(lax.broadcasted_iota(I32, (rb, c1 - c0), 1) <= row, 0.0, NEG)
                    for c in range(2):
                        _flash_rows(_nt(qc[c], k[c0:c1]), bias, v1[c0:c1], m_scr, acc_scr,
                                    (c, pl.ds(r * tq + r0, rb)))

    @pl.when(ki < qi)
    def _():
        step(False)

    @pl.when(ki == qi)
    def _():
        step(True)
        lam = _da_lambda(lam_ref, lam_init)
        a0 = acc_scr[0, :, :DA_DV] / acc_scr[0, :, DA_DV:]
        a1 = acc_scr[1, :, :DA_DV] / acc_scr[1, :, DA_DV:]
        o = _da_finish(a0, a1, lam, sub_ref[...], lam_init)
        for r in range(DA_REP):
            o_ref[:, r * DA_DV:(r + 1) * DA_DV] = o[r * tq:(r + 1) * tq].astype(o_ref.dtype)


def _da_flash(zb, base, b, s, lam_p, subln, lam_init):
    tq = _pick_tile(s, FLASH_TILE, 16)
    rb = _pick_tile(tq, FLASH_ROWS, 16)
    kb = _pick_tile(tq, FLASH_KEYS, LANES)
    nq = s // tq
    qtab, ktab = _tri_tables(nq)
    kcol, vcol = S1 // DA_DV, S2 // DA_DV
    grid_spec = pltpu.PrefetchScalarGridSpec(
        num_scalar_prefetch=2, grid=(b, DA_KV, qtab.shape[0]),
        in_specs=[pl.BlockSpec((4, DA_DH), lambda bi, g, t, qt, kt: (0, 0)),
                  pl.BlockSpec((1, DA_DV), lambda bi, g, t, qt, kt: (0, 0)),
                  pl.BlockSpec((tq, DA_REP * DA_DV), lambda bi, g, t, qt, kt: (bi * nq + qt[t], g)),
                  pl.BlockSpec((tq, DA_DV), lambda bi, g, t, qt, kt: (bi * nq + kt[t], kcol + g)),
                  pl.BlockSpec((tq, DA_DV), lambda bi, g, t, qt, kt: (bi * nq + kt[t], vcol + g)),
                  pl.BlockSpec(memory_space=pl.ANY)],
        out_specs=pl.BlockSpec((tq, DA_REP * DA_DV), lambda bi, g, t, qt, kt: (bi * nq + qt[t], g)),
        scratch_shapes=[pltpu.VMEM((2, DA_REP * tq, LANES), F32),
                        pltpu.VMEM((2, DA_REP * tq, 2 * DA_DV), F32)])
    return pl.pallas_call(
        functools.partial(_da_flash_kernel, tq=tq, rb=rb, kb=kb, lam_init=lam_init),
        out_shape=jax.ShapeDtypeStruct(base.shape, base.dtype),
        grid_spec=grid_spec,
        input_output_aliases={7: 0},
        compiler_params=_cparams(("parallel", "parallel", "arbitrary")),
        name="da_flash",
    )(qtab, ktab, lam_p, subln, zb, zb, zb, base)


def _page_steps(npages):
    pg = _pick_tile(npages, DECODE_PAGES, 1)
    return pg, npages // pg


def _page_specs(block, npages, pg, col_block):
    nd = len(block) - 1

    def spec(u):
        def index(b, p, pt):
            idx = [pt[b, jnp.minimum(p * pg + u, npages - 1)]] + [0] * nd
            if col_block is not None:
                idx[1] = col_block
            return tuple(idx)
        return pl.BlockSpec(block, index)

    return [spec(u) for u in range(pg)]


def _da_decode_kernel(pt_ref, lam_ref, sub_ref, q_ref, *refs, nsteps, pg, page, rq, lam_init):
    del pt_ref
    c_refs = refs[:pg]
    n_ref, o_ref, m_scr, l_scr, acc_scr = refs[pg:]
    p = pl.program_id(1)
    per = 2 * DA_KV

    @pl.when(p == 0)
    def _():
        _flash_init(m_scr, l_scr, acc_scr)

    q = q_ref[0]
    rows = q.shape[0]

    def heads(ref, first):
        return jnp.concatenate([ref[0, pl.ds(first + g, page, stride=per), :] for g in range(DA_KV)],
                               axis=1).astype(BF16)

    def step(blocks, mask):
        k = jnp.concatenate([heads(r, 0) for r in blocks], axis=0)
        v = jnp.concatenate([heads(r, DA_KV) for r in blocks], axis=0)
        _flash_update(_nt(q, k), mask, v, m_scr, l_scr, acc_scr, 0)

    @pl.when(p < nsteps)
    def _():
        step(c_refs, None)

    @pl.when(p == nsteps)
    def _():
        row = lax.broadcasted_iota(I32, (rows, page), 0)
        col = lax.broadcasted_iota(I32, (rows, page), 1)
        step([n_ref], col <= (row % rq) // DA_REP)
        lam = _da_lambda(lam_ref, lam_init)
        a = acc_scr[0] / l_scr[0]
        for g in range(DA_KV):
            cols = slice(g * DA_DV, (g + 1) * DA_DV)
            a0 = a[g * 2 * rq:g * 2 * rq + rq, cols]
            a1 = a[g * 2 * rq + rq:(g + 1) * 2 * rq, cols]
            o_ref[0, g] = _da_finish(a0, a1, lam, sub_ref[...], lam_init)


def _da_decode(page_table, qbd, cache, newkv, lam_p, subln, lam_init):
    nb, rows, ck = qbd.shape
    npages = page_table.shape[1]
    prow = cache.shape[1]
    page = prow // (2 * DA_KV)
    rq = rows // (DA_KV * 2)
    pg, nsteps = _page_steps(npages)
    grid_spec = pltpu.PrefetchScalarGridSpec(
        num_scalar_prefetch=1, grid=(nb, nsteps + 1),
        in_specs=[pl.BlockSpec((4, DA_DH), lambda b, p, pt: (0, 0)),
                  pl.BlockSpec((1, DA_DV), lambda b, p, pt: (0, 0)),
                  pl.BlockSpec((1, rows, ck), lambda b, p, pt: (b, 0, 0))]
        + _page_specs((1, prow, DA_DV), npages, pg, None)
        + [pl.BlockSpec((1, prow, DA_DV), lambda b, p, pt: (b, 0, 0))],
        out_specs=pl.BlockSpec((1, DA_KV, rq, DA_DV), lambda b, p, pt: (b, 0, 0, 0)),
        scratch_shapes=[pltpu.VMEM((1, rows, 1), F32), pltpu.VMEM((1, rows, 1), F32),
                        pltpu.VMEM((1, rows, DA_KV * DA_DV), F32)])
    return pl.pallas_call(
        functools.partial(_da_decode_kernel, nsteps=nsteps, pg=pg, page=page, rq=rq, lam_init=lam_init),
        out_shape=jax.ShapeDtypeStruct((nb, DA_KV, rq, DA_DV), F32),
        grid_spec=grid_spec,
        compiler_params=_cparams(("parallel", "arbitrary")),
        name="da_decode",
    )(page_table, lam_p, subln, qbd, *([cache] * pg), newkv)


def _compress_kernel(k_ref, v_ref, wp_ref, wc_ref, ko_ref, vo_ref):
    for c, (src, dst) in enumerate(((k_ref, ko_ref), (v_ref, vo_ref))):
        x = src[...]
        t = x.shape[0]
        y = jnp.sum(x.reshape(t // BLOCK, BLOCK, LANES) * wp_ref[c][None], axis=1)
        dst[...] = _dot(y, wc_ref[c], precision=HIGHEST)


def _compress_prompt(zf, wp, wc, b, s):
    nc = s // BLOCK
    kc0 = (S3 - S1) // LANES
    out = jax.ShapeDtypeStruct((b * nc, LANES), F32)
    return pl.pallas_call(
        _compress_kernel,
        out_shape=(out, out),
        grid=(b,),
        in_specs=[pl.BlockSpec((s, LANES), lambda i: (i, kc0)), pl.BlockSpec((s, LANES), lambda i: (i, kc0 + 1)),
                  pl.BlockSpec((2, BLOCK, LANES), lambda i: (0, 0, 0)),
                  pl.BlockSpec((2, LANES, LANES), lambda i: (0, 0, 0))],
        out_specs=(pl.BlockSpec((nc, LANES), lambda i: (i, 0)), pl.BlockSpec((nc, LANES), lambda i: (i, 0))),
        compiler_params=_cparams(("parallel",)),
        name="nsa_compress",
    )(zf, zf, wp, wc)


def _masked_probs(s, ok):
    s = jnp.where(ok, s, NEG)
    e = jnp.where(ok, jnp.exp(s - jnp.max(s, axis=-1, keepdims=True)), 0.0)
    den = jnp.sum(e, axis=-1, keepdims=True)
    return e / jnp.where(den > 0.0, den, 1.0)


def _select_blocks(imp, pos, ns):
    width = imp.shape[1]
    blk = lax.broadcasted_iota(I32, (1, width), 1)
    cur = lax.shift_right_logical(pos, BLOCK_SHIFT)
    forced = (blk == 0) | (blk == cur) | (blk == cur - 1)
    score = jnp.where(forced, imp + FORCE_BONUS, imp)
    score = jnp.where(blk <= cur, score, -1.0)
    score = jnp.where(blk < ns, score, 0.5 * LOWEST)
    sel = jnp.zeros(imp.shape, F32)
    for _ in range(N_SELECT):
        mx = jnp.max(score, axis=-1, keepdims=True)
        idx = jnp.min(jnp.where(score == mx, blk, width), axis=-1, keepdims=True)
        hit = blk == idx
        sel = jnp.where(hit, 1.0, sel)
        score = jnp.where(hit, LOWEST, score)
    return sel


def _half_mask(half):
    upper = lax.broadcasted_iota(I32, (1, LANES), 1) >= NSA_DH
    return upper if half == 1 else jnp.logical_not(upper)


def _head_q(q_ref, rows, r):
    q = q_ref[rows, (r // 2) * LANES:(r // 2 + 1) * LANES]
    return jnp.where(_half_mask(r % 2), q, jnp.zeros_like(q))


def _cmp_select_kernel(q_ref, kc_ref, vc_ref, base_ref, o_ref, sel_ref, *, tq, nc, ns, width):
    del base_ref
    qi = pl.program_id(1)
    pos = qi * tq + lax.broadcasted_iota(I32, (tq, 1), 0)
    blk = lax.broadcasted_iota(I32, (1, width), 1)
    cmp_ok = (blk < nc) & ((blk + 1) * BLOCK - 1 <= pos)
    hw = NSA_REP * NSA_DH
    for g in range(NSA_KV):
        kc = kc_ref[0, g]
        imp = jnp.zeros((tq, width), F32)
        for j in range(NSA_REP // 2):
            out = None
            for a in range(2):
                q = q_ref[:, g * hw + j * LANES:g * hw + (j + 1) * LANES]
                q = jnp.where(_half_mask(a), q, jnp.zeros_like(q))
                p = _masked_probs(_nt(q, kc), cmp_ok)
                imp = imp + p
                term = _dot(p.astype(BF16), vc_ref[0, g, a])
                out = term if out is None else out + term
            o_ref[:, g * hw + j * LANES:g * hw + (j + 1) * LANES] = out
        sel_ref[0, g] = _select_blocks(imp, pos, ns).astype(sel_ref.dtype)


def _cmp_decode_kernel(pt_ref, q_ref, wt_ref, wct_ref, *refs, nsteps, pg, page, past, t4, nc, width):
    del pt_ref
    c_refs = refs[:pg]
    o_ref, imp_ref, acc = refs[pg:]
    p = pl.program_id(1)
    kw = NSA_KV * NSA_DH
    cw = acc.shape[1]

    @pl.when(p == 0)
    def _():
        acc[...] = jnp.zeros(acc.shape, F32)

    z = jnp.concatenate([c_refs[u][0].reshape(2 * kw, page) * wt_ref[...] for u in range(pg)], axis=1)
    tokblk = lax.shift_right_logical(lax.broadcasted_iota(I32, (pg * page, cw), 0), BLOCK_SHIFT)
    lane = lax.broadcasted_iota(I32, (pg * page, cw), 1)
    place = jnp.where(lane == p * (pg * page // BLOCK) + tokblk, 1.0, 0.0).astype(BF16)
    total = None
    for _ in range(3):
        piece = z.astype(BF16)
        z = z - piece.astype(F32)
        term = _dot(piece, place)
        total = term if total is None else total + term
    acc[...] += total

    @pl.when(p == nsteps - 1)
    def _():
        y = acc[...]
        rows = t4 * NSA_REP
        pos = past + lax.broadcasted_iota(I32, (rows, 1), 0) // NSA_REP
        blk = lax.broadcasted_iota(I32, (1, cw), 1)
        cmp_ok = (blk < nc) & ((blk + 1) * BLOCK - 1 <= pos)
        for g in range(NSA_KV):
            kct = _dot(wct_ref[0], y[g * NSA_DH:(g + 1) * NSA_DH], precision=HIGHEST)
            vct = _dot(wct_ref[1], y[kw + g * NSA_DH:kw + (g + 1) * NSA_DH], precision=HIGHEST)
            pr = _masked_probs(_dot(q_ref[0, g], kct.astype(BF16)), cmp_ok)
            o_ref[0, g] = _nt(pr.astype(BF16), vct.astype(BF16))
            imp = jnp.sum(pr.reshape(t4, NSA_REP, cw), axis=1)
            if width > cw:
                imp = jnp.concatenate([imp, jnp.zeros((t4, width - cw), F32)], axis=1)
            imp_ref[0, g] = imp


def _select_decode_kernel(imp_ref, sel_ref, *, past, t4, ns):
    rows = imp_ref.shape[0]
    pos = past + lax.broadcasted_iota(I32, (rows, 1), 0) % t4
    sel_ref[...] = _select_blocks(imp_ref[...], pos, ns).astype(sel_ref.dtype)


def _cmp_decode(page_table, qst, cache_t, wt, wct, *, past, t4, nc, ns, width):
    nb, npages = page_table.shape
    page = cache_t.shape[4]
    pg, nsteps = _page_steps(npages)
    rows = t4 * NSA_REP
    cw = _round_up(nc, LANES)
    kw = NSA_KV * NSA_DH
    grid_spec = pltpu.PrefetchScalarGridSpec(
        num_scalar_prefetch=1, grid=(nb, nsteps),
        in_specs=[pl.BlockSpec((1, NSA_KV, rows, NSA_DH), lambda b, p, pt: (b, 0, 0, 0)),
                  pl.BlockSpec((2 * kw, page), lambda b, p, pt: (0, 0)),
                  pl.BlockSpec((2, NSA_DH, NSA_DH), lambda b, p, pt: (0, 0, 0))]
        + _page_specs((1, 2, NSA_KV, NSA_DH, page), npages, pg, 0),
        out_specs=(pl.BlockSpec((1, NSA_KV, rows, NSA_DH), lambda b, p, pt: (b, 0, 0, 0)),
                   pl.BlockSpec((1, NSA_KV, t4, width), lambda b, p, pt: (b, 0, 0, 0))),
        scratch_shapes=[pltpu.VMEM((2 * kw, cw), F32)])
    ocmp, imp = pl.pallas_call(
        functools.partial(_cmp_decode_kernel, nsteps=nsteps, pg=pg, page=page, past=past, t4=t4,
                          nc=nc, width=width),
        out_shape=(jax.ShapeDtypeStruct((nb, NSA_KV, rows, NSA_DH), F32),
                   jax.ShapeDtypeStruct((nb, NSA_KV, t4, width), F32)),
        grid_spec=grid_spec,
        compiler_params=_cparams(("parallel", "arbitrary")),
        name="nsa_cmp_decode",
    )(page_table, qst, wt, wct, *([cache_t] * pg))
    nrow = nb * NSA_KV * t4
    sel = pl.pallas_call(
        functools.partial(_select_decode_kernel, past=past, t4=t4, ns=ns),
        out_shape=jax.ShapeDtypeStruct((nrow, width), BF16),
        grid=(1,),
        in_specs=[pl.BlockSpec((nrow, width), lambda i: (0, 0))],
        out_specs=pl.BlockSpec((nrow, width), lambda i: (0, 0)),
        compiler_params=_cparams(("arbitrary",)),
        name="nsa_select_decode",
    )(imp.reshape(nrow, width))
    return ocmp, sel.reshape(nb, NSA_KV, t4, width)


def _cmp_select(zb, kc2, vc2, base, b, s, *, nc, ns):
    width = kc2.shape[2]
    tq = _pick_tile(s, 256, 16)
    nq = s // tq
    qcol = S3 // C_NSA_Q
    return pl.pallas_call(
        functools.partial(_cmp_select_kernel, tq=tq, nc=nc, ns=ns, width=width),
        out_shape=(jax.ShapeDtypeStruct(base.shape, base.dtype),
                   jax.ShapeDtypeStruct((b, NSA_KV, s, width), BF16)),
        grid=(b, nq),
        in_specs=[pl.BlockSpec((tq, C_NSA_Q), lambda bi, qi: (bi * nq + qi, qcol)),
                  pl.BlockSpec((1, NSA_KV, width, LANES), lambda bi, qi: (bi, 0, 0, 0)),
                  pl.BlockSpec((1, NSA_KV, 2, width, LANES), lambda bi, qi: (bi, 0, 0, 0, 0)),
                  pl.BlockSpec(memory_space=pl.ANY)],
        out_specs=(pl.BlockSpec((tq, C_NSA_Q), lambda bi, qi: (bi * nq + qi, 0)),
                   pl.BlockSpec((1, NSA_KV, tq, width), lambda bi, qi: (bi, 0, qi, 0))),
        input_output_aliases={3: 0},
        compiler_params=_cparams(("parallel", "parallel")),
        name="nsa_cmp_select",
    )(zb, kc2, vc2, base)


def _nsa_flash_kernel(qi_ref, ki_ref, *refs, mode, tq, rb, kb, nsteps):
    if mode == "slc":
        q_ref, k_ref, v_ref, sel_ref, e_ref, _, o_ref, m_scr, acc_scr = refs
    else:
        q_ref, k_ref, v_ref, _, o_ref, m_scr, acc_scr = refs
    t = pl.program_id(2)
    qi = qi_ref[t]
    ki = ki_ref[t]
    first = (t == 0) | (qi_ref[jnp.maximum(t - 1, 0)] != qi)
    last = (t == nsteps - 1) | (qi_ref[jnp.minimum(t + 1, nsteps - 1)] != qi)

    @pl.when(first)
    def _():
        m_scr[...] = jnp.full(m_scr.shape, NEG, F32)
        acc_scr[...] = jnp.zeros(acc_scr.shape, F32)

    if mode == "slc":
        picked = _dot(sel_ref[0, 0], e_ref[...])
    k = k_ref[...]
    v1 = v_ref[...]
    diag = qi == ki
    for r0 in range(0, tq, rb):
        rows = pl.ds(r0, rb)
        nk = _round_up(r0 + rb, LANES)

        def update(ncols, r0=r0, rows=rows):
            for c0 in range(0, ncols, kb):
                c1 = min(c0 + kb, ncols)
                d = ((qi - ki) * tq + r0 - c0 + lax.broadcasted_iota(I32, (rb, c1 - c0), 0)
                     - lax.broadcasted_iota(I32, (rb, c1 - c0), 1))
                if mode == "slc":
                    visible = (picked[r0:r0 + rb, c0:c1] > 0.5) & (d >= 0)
                else:
                    visible = (d >= 0) & (d < WINDOW)
                bias = jnp.where(visible, 0.0, NEG)
                for h in range(NSA_REP):
                    s = _nt(_head_q(q_ref, rows, h), k[c0:c1])
                    _flash_rows(s, bias, v1[c0:c1], m_scr, acc_scr, (h, rows))

        if nk < tq:
            pl.when(diag)(functools.partial(update, nk))
            pl.when(jnp.logical_not(diag))(functools.partial(update, tq))
        else:
            update(tq)

    @pl.when(last)
    def _():
        low = _half_mask(0)
        for j in range(NSA_REP // 2):
            a, bq = acc_scr[2 * j], acc_scr[2 * j + 1]
            num = jnp.where(low, a, pltpu.roll(bq, NSA_DH, 1))
            den = jnp.where(low, pltpu.roll(a, NSA_DH, 1), bq)
            o_ref[:, j * LANES:(j + 1) * LANES] = num / den


def _nsa_flash(zb, k2, v1, base, b, s, sel=None, emat=None):
    tq = _pick_tile(s, FLASH_TILE, 16)
    rb = _pick_tile(tq, FLASH_ROWS, 16)
    kb = _pick_tile(tq, FLASH_KEYS, LANES)
    nq = s // tq
    if sel is not None:
        mode = "slc"
        qtab, ktab = _tri_tables(nq)
    else:
        mode = "win"
        back = -(-WINDOW // tq)
        pairs = [(q, kk) for q in range(nq) for kk in range(max(q - back, 0), q + 1)]
        qtab = jnp.asarray([p[0] for p in pairs], I32)
        ktab = jnp.asarray([p[1] for p in pairs], I32)
    nsteps = qtab.shape[0]
    gw = NSA_REP * NSA_DH
    qcol = S3 // gw
    in_specs = [pl.BlockSpec((tq, gw), lambda bi, g, t, qt, kt: (bi * nq + qt[t], qcol + g)),
                pl.BlockSpec((tq, LANES), lambda bi, g, t, qt, kt: (bi * nq + kt[t], g)),
                pl.BlockSpec((tq, LANES), lambda bi, g, t, qt, kt: (bi * nq + kt[t], g))]
    args = [zb, k2, v1]
    if sel is not None:
        width = sel.shape[3]
        in_specs += [pl.BlockSpec((1, 1, tq, width), lambda bi, g, t, qt, kt: (bi, g, qt[t], 0)),
                     pl.BlockSpec((width, tq), lambda bi, g, t, qt, kt: (0, kt[t]))]
        args += [sel, emat]
    in_specs.append(pl.BlockSpec(memory_space=pl.ANY))
    args.append(base)
    grid_spec = pltpu.PrefetchScalarGridSpec(
        num_scalar_prefetch=2, grid=(b, NSA_KV, nsteps),
        in_specs=in_specs,
        out_specs=pl.BlockSpec((tq, gw), lambda bi, g, t, qt, kt: (bi * nq + qt[t], g)),
        scratch_shapes=[pltpu.VMEM((NSA_REP, tq, LANES), F32), pltpu.VMEM((NSA_REP, tq, LANES), F32)])
    return pl.pallas_call(
        functools.partial(_nsa_flash_kernel, mode=mode, tq=tq, rb=rb, kb=kb, nsteps=nsteps),
        out_shape=jax.ShapeDtypeStruct(base.shape, base.dtype),
        grid_spec=grid_spec,
        input_output_aliases={1 + len(args): 0},
        compiler_params=_cparams(("parallel", "parallel", "arbitrary")),
        name="nsa_" + mode,
    )(qtab, ktab, *args)


def _row_token(rows, cols, t4):
    row = lax.broadcasted_iota(I32, (rows, cols), 0)
    return (row // NSA_REP) % t4


def _kv_t(blocks):
    kw = NSA_KV * NSA_DH
    kt = [r[0, 0].reshape(kw, r.shape[4]) for r in blocks]
    vt = [r[0, 1].reshape(kw, r.shape[4]) for r in blocks]
    cat = lambda xs: (xs[0] if len(xs) == 1 else jnp.concatenate(xs, axis=1)).astype(BF16)
    return cat(kt), cat(vt)


def _slc_decode_kernel(pt_ref, q_ref, sel_ref, e_ref, *refs, nsteps, pg, page, past, t4):
    del pt_ref
    c_refs = refs[:pg]
    n_ref, o_ref, m_scr, l_scr, acc_scr = refs[pg:]
    p = pl.program_id(1)

    @pl.when(p == 0)
    def _():
        _flash_init(m_scr, l_scr, acc_scr)

    q = q_ref[0]
    rows = q.shape[0]

    def step(blocks):
        kt, vt = _kv_t(blocks)
        nk = kt.shape[1]
        kpos = p * (pg * page) + lax.broadcasted_iota(I32, (rows, nk), 1)
        picked = _dot(sel_ref[0], e_ref[:, :nk]) > 0.5
        mask = picked & (kpos <= past + _row_token(rows, nk, t4))
        _flash_update(_dot(q, kt), mask, vt, m_scr, l_scr, acc_scr, 0, v_is_transposed=True)

    @pl.when(p < nsteps)
    def _():
        step(c_refs)

    @pl.when(p == nsteps)
    def _():
        step([n_ref])
        o_ref[0] = acc_scr[0] / l_scr[0]


def _slc_decode(page_table, qnb, selrows, emat, cache_t, new_t, *, past, t4):
    nb, rows, _ = qnb.shape
    npages = page_table.shape[1]
    page = cache_t.shape[4]
    width = selrows.shape[2]
    pg, nsteps = _page_steps(npages)
    blk = (1, 2, NSA_KV, NSA_DH, page)
    grid_spec = pltpu.PrefetchScalarGridSpec(
        num_scalar_prefetch=1, grid=(nb, nsteps + 1),
        in_specs=[pl.BlockSpec((1, rows, LANES), lambda b, p, pt: (b, 0, 0)),
                  pl.BlockSpec((1, rows, width), lambda b, p, pt: (b, 0, 0)),
                  pl.BlockSpec((width, pg * page), lambda b, p, pt: (0, p))]
        + _page_specs(blk, npages, pg, 1)
        + [pl.BlockSpec(blk, lambda b, p, pt: (b, 0, 0, 0, 0))],
        out_specs=pl.BlockSpec((1, rows, LANES), lambda b, p, pt: (b, 0, 0)),
        scratch_shapes=[pltpu.VMEM((1, rows, 1), F32), pltpu.VMEM((1, rows, 1), F32),
                        pltpu.VMEM((1, rows, LANES), F32)])
    return pl.pallas_call(
        functools.partial(_slc_decode_kernel, nsteps=nsteps, pg=pg, page=page, past=past, t4=t4),
        out_shape=jax.ShapeDtypeStruct((nb, rows, LANES), F32),
        grid_spec=grid_spec,
        compiler_params=_cparams(("parallel", "arbitrary")),
        name="nsa_slc_decode",
    )(page_table, qnb, selrows, emat, *([cache_t] * pg), new_t)


def _win_decode_kernel(q_ref, c_ref, n_ref, o_ref, *, past, t4):
    q = q_ref[0]
    rows = q.shape[0]
    wb = c_ref.shape[4]
    m = jnp.full((rows, 1), NEG, F32)
    l = jnp.zeros((rows, 1), F32)
    acc = jnp.zeros((rows, LANES), F32)
    for ref, kstart in ((c_ref, past - wb), (n_ref, past)):
        kt, vt = _kv_t([ref])
        nk = kt.shape[1]
        d = past + _row_token(rows, nk, t4) - (kstart + lax.broadcasted_iota(I32, (rows, nk), 1))
        mask = (d >= 0) & (d < WINDOW)
        m, l, acc = _softmax_block(_dot(q, kt), mask, vt, m, l, acc, v_is_transposed=True)
    o_ref[0] = acc / l


def _win_decode(qnb, cwin_t, new_t, *, past, t4):
    nb, rows, _ = qnb.shape
    return pl.pallas_call(
        functools.partial(_win_decode_kernel, past=past, t4=t4),
        out_shape=jax.ShapeDtypeStruct((nb, rows, LANES), F32),
        grid=(nb,),
        in_specs=[pl.BlockSpec((1, rows, LANES), lambda b: (b, 0, 0)),
                  pl.BlockSpec((1,) + cwin_t.shape[1:], lambda b: (b, 0, 0, 0, 0)),
                  pl.BlockSpec((1,) + new_t.shape[1:], lambda b: (b, 0, 0, 0, 0))],
        out_specs=pl.BlockSpec((1, rows, LANES), lambda b: (b, 0, 0)),
        compiler_params=_cparams(("parallel",)),
        name="nsa_win_decode",
    )(qnb, cwin_t, new_t)


def _merge_kernel(oda_ref, oc_ref, os_ref, ow_ref, gn_ref, ga_ref, gb_ref, e_ref, wda_ref, wns_ref, o_ref):
    g = gn_ref[...]
    ghi = g.astype(BF16)
    glo = (g - ghi.astype(F32)).astype(BF16)
    ons = None
    for b, src in enumerate((oc_ref, os_ref, ow_ref)):
        ge = _dot(ghi, e_ref[b]) + _dot(glo, e_ref[b])
        term = ge * src[...]
        ons = term if ons is None else ons + term
    yda = _dot(oda_ref[...], wda_ref[...])
    yns = _dot(ons.astype(BF16), wns_ref[...])
    o_ref[...] = (ga_ref[...] * yda + gb_ref[...] * yns).astype(o_ref.dtype)


def _merge(oda, ocmp, oslc, owin, gns, gmg, emat, wda, wns):
    n, hd = oda.shape
    d = wda.shape[1]
    tm = _pick_tile(n, 320, 16)
    row = lambda i: (i, 0)
    const2 = lambda i: (0, 0)
    return pl.pallas_call(
        _merge_kernel,
        out_shape=jax.ShapeDtypeStruct((n, d), BF16),
        grid=(n // tm,),
        in_specs=[pl.BlockSpec((tm, hd), row), pl.BlockSpec((tm, hd), row), pl.BlockSpec((tm, hd), row),
                  pl.BlockSpec((tm, hd), row), pl.BlockSpec((tm, LANES), row),
                  pl.BlockSpec((tm, d), lambda i: (i, 0)), pl.BlockSpec((tm, d), lambda i: (i, 1)),
                  pl.BlockSpec((3, LANES, hd), lambda i: (0, 0, 0)),
                  pl.BlockSpec((hd, d), const2), pl.BlockSpec((hd, d), const2)],
        out_specs=pl.BlockSpec((tm, d), row),
        compiler_params=_cparams(("parallel",)),
        name="mixer_merge",
    )(oda, ocmp, oslc, owin, gns, gmg, gmg, emat, wda, wns)


def _lane_pick(x, lane, k):
    return jnp.sum(jnp.where(lane == k, x, 0.0), axis=-1, keepdims=True)


def _route(logits):
    lane = lax.broadcasted_iota(I32, logits.shape, 1)
    isg = lane < N_GROUPS
    gmax = jnp.max(jnp.where(isg, logits, LOWEST), axis=-1, keepdims=True)
    gsel = jnp.min(jnp.where(isg & (logits == gmax), lane, LANES), axis=-1, keepdims=True)
    pgrp = 1.0 / jnp.sum(jnp.where(isg, jnp.exp(logits - gmax), 0.0), axis=-1, keepdims=True)
    lo = N_GROUPS + gsel * EXP_PER_GROUP
    ing = (lane >= lo) & (lane < lo + EXP_PER_GROUP)
    v1 = jnp.max(jnp.where(ing, logits, LOWEST), axis=-1, keepdims=True)
    i1 = jnp.min(jnp.where(ing & (logits == v1), lane, LANES), axis=-1, keepdims=True)
    ing2 = ing & (lane != i1)
    v2 = jnp.max(jnp.where(ing2, logits, LOWEST), axis=-1, keepdims=True)
    i2 = jnp.min(jnp.where(ing2 & (logits == v2), lane, LANES), axis=-1, keepdims=True)
    t = jnp.exp(v2 - v1)
    w1 = pgrp / (1.0 + t)
    w2 = pgrp * t / (1.0 + t)
    e1 = (i1 - N_GROUPS).astype(F32)
    e2 = (i2 - N_GROUPS).astype(F32)
    return jnp.where(lane == 0, e1, jnp.where(lane == 1, e2, jnp.where(lane == 2, w1, jnp.where(lane == 3, w2, 0.0))))


def _store_rows(ref, y, rows, nchunk):
    for s in range(nchunk):
        ref[pl.ds(s, rows, stride=nchunk), :] = y[:, s * LANES:(s + 1) * LANES]


def _wo_route_kernel(m_ref, x_ref, wo_ref, gf_ref, wr_ref, br_ref, x1_ref, h_ref, rt_ref, *, nchunk):
    x1 = x_ref[...] + _dot(m_ref[...], wo_ref[...])
    x1_ref[...] = x1
    h = _rms(x1, gf_ref[...])
    _store_rows(h_ref, h, h.shape[0], nchunk)
    hi = h.astype(BF16)
    lo = (h - hi.astype(F32)).astype(BF16)
    w = wr_ref[...]
    w_hi = w.astype(BF16)
    w_lo = (w - w_hi.astype(F32)).astype(BF16)
    logits = _dot(hi, w_hi) + (_dot(hi, w_lo) + _dot(lo, w_hi))
    rt_ref[...] = _route(logits + br_ref[...])


def _wo_route(mrg, xa, wo, gffn, wr, br):
    n, d = xa.shape
    tm = _pick_tile(n, 320, 16)
    nchunk = d // LANES
    row = lambda i: (i, 0)
    const2 = lambda i: (0, 0)
    return pl.pallas_call(
        functools.partial(_wo_route_kernel, nchunk=nchunk),
        out_shape=(jax.ShapeDtypeStruct((n, d), F32), jax.ShapeDtypeStruct((n * nchunk, LANES), F32),
                   jax.ShapeDtypeStruct((n, LANES), F32)),
        grid=(n // tm,),
        in_specs=[pl.BlockSpec((tm, d), row), pl.BlockSpec((tm, d), row), pl.BlockSpec((d, d), const2),
                  pl.BlockSpec((1, d), const2), pl.BlockSpec((d, LANES), const2),
                  pl.BlockSpec((1, LANES), const2)],
        out_specs=(pl.BlockSpec((tm, d), row), pl.BlockSpec((tm * nchunk, LANES), row),
                   pl.BlockSpec((tm, LANES), row)),
        compiler_params=_cparams(("parallel",)),
        name="wo_route",
    )(mrg, xa, wo, gffn, wr, br)


def _onehots(rt):
    lane = lax.broadcasted_iota(I32, rt.shape, 1)
    lane_f = lane.astype(F32)
    oh0 = jnp.where(lane_f == _lane_pick(rt, lane, 0), 1.0, 0.0)
    oh1 = jnp.where(lane_f == _lane_pick(rt, lane, 1), 1.0, 0.0)
    return lane, oh0, oh1


def _rank_kernel(rt_ref, tri_ref, rk_ref, sz_ref, carry):
    @pl.when(pl.program_id(0) == 0)
    def _():
        carry[...] = jnp.zeros(carry.shape, F32)

    lane, oh0, oh1 = _onehots(rt_ref[...])
    oh = oh0 + oh1
    before = _dot(tri_ref[...], oh.astype(BF16)) + carry[0:1, :]
    r0 = jnp.sum(before * oh0, axis=-1, keepdims=True)
    r1 = jnp.sum(before * oh1, axis=-1, keepdims=True)
    rk_ref[...] = jnp.where(lane == 0, r0, jnp.where(lane == 1, r1, 0.0))
    carry[...] = carry[...] + jnp.sum(oh, axis=0, keepdims=True)
    sz_ref[...] = carry[...]


def _expert_ranks(route, tri):
    n = route.shape[0]
    tm = tri.shape[0]
    return pl.pallas_call(
        _rank_kernel,
        out_shape=(jax.ShapeDtypeStruct((n, LANES), F32), jax.ShapeDtypeStruct((8, LANES), F32)),
        grid=(n // tm,),
        in_specs=[pl.BlockSpec((tm, LANES), lambda i: (i, 0)), pl.BlockSpec((tm, tm), lambda i: (0, 0))],
        out_specs=(pl.BlockSpec((tm, LANES), lambda i: (i, 0)), pl.BlockSpec((8, LANES), lambda i: (0, 0))),
        scratch_shapes=[pltpu.VMEM((8, LANES), F32)],
        compiler_params=_cparams(("arbitrary",)),
        name="moe_rank",
    )(route, tri)


def _dest_kernel(rt_ref, rk_ref, sz_ref, dest_ref, be_ref, nu_ref, *, shift):
    lane8 = lax.broadcasted_iota(I32, (8, LANES), 1)
    sizes = sz_ref[...].astype(I32)
    blk = 1 << shift
    padded = jnp.where(lane8 < N_EXPERTS, lax.shift_right_logical(sizes + (blk - 1), shift) * blk, 0)
    ends = padded
    d = 1
    while d < LANES:
        ends = ends + jnp.where(lane8 >= d, pltpu.roll(ends, d, 1), 0)
        d *= 2
    pstart = (ends - padded)[0:1, :].astype(F32)
    rt = rt_ref[...]
    lane, oh0, oh1 = _onehots(rt)
    rk = rk_ref[...]
    d0 = jnp.sum(oh0 * pstart, axis=-1, keepdims=True) + _lane_pick(rk, lane, 0)
    d1 = jnp.sum(oh1 * pstart, axis=-1, keepdims=True) + _lane_pick(rk, lane, 1)
    dest_ref[...] = jnp.where(lane == 0, d0, jnp.where(lane == 1, d1, 0.0)).astype(I32)
    nbt = be_ref.shape[0]
    lane_b = lax.broadcasted_iota(I32, (nbt, LANES), 1)
    first_row = lax.broadcasted_iota(I32, (nbt, LANES), 0) * blk
    cnt = jnp.sum(jnp.where((lane_b < N_EXPERTS) & (ends[0:1, :] <= first_row), 1, 0), axis=-1, keepdims=True)
    be_ref[...] = jnp.broadcast_to(jnp.minimum(cnt, N_EXPERTS - 1), (nbt, LANES))
    total = jnp.sum(jnp.where(lane8 == N_EXPERTS - 1, ends, 0), axis=-1, keepdims=True)
    nu_ref[...] = jnp.broadcast_to(lax.shift_right_logical(total, shift), (8, LANES))


def _destinations(route, rank, sizes, nbt, shift):
    n = route.shape[0]
    tm = _pick_tile(n, 640, 8)
    nbt_p = _round_up(nbt, 8)
    return pl.pallas_call(
        functools.partial(_dest_kernel, shift=shift),
        out_shape=(jax.ShapeDtypeStruct((n, LANES), I32), jax.ShapeDtypeStruct((nbt_p, LANES), I32),
                   jax.ShapeDtypeStruct((8, LANES), I32)),
        grid=(n // tm,),
        in_specs=[pl.BlockSpec((tm, LANES), lambda i: (i, 0)), pl.BlockSpec((tm, LANES), lambda i: (i, 0)),
                  pl.BlockSpec((8, LANES), lambda i: (0, 0))],
        out_specs=(pl.BlockSpec((tm, LANES), lambda i: (i, 0)), pl.BlockSpec((nbt_p, LANES), lambda i: (0, 0)),
                   pl.BlockSpec((8, LANES), lambda i: (0, 0))),
        compiler_params=_cparams(("arbitrary",)),
        name="moe_dest",
    )(route, rank, sizes)


def _expert_kernel(dest_ref, be_ref, nu_ref, h_ref, wg_ref, wu_ref, wd_ref, o_ref,
                   tok, ordinal, xbuf, sem, wgf, wuf, wdf, wsem, wgb, wub, wdb, *, rows, nchunk, n_assign):
    j = pl.program_id(0)
    n_used = nu_ref[0]

    def weight_copies(e, ws):
        return (pltpu.make_async_copy(wg_ref.at[e], wgf.at[ws], wsem.at[ws, 0]),
                pltpu.make_async_copy(wu_ref.at[e], wuf.at[ws], wsem.at[ws, 1]),
                pltpu.make_async_copy(wd_ref.at[e], wdf.at[ws], wsem.at[ws, 2]))

    def row_copy(tile, slot, r):
        return pltpu.make_async_copy(h_ref.at[tok[tile * rows + r]],
                                     xbuf.at[slot, pl.ds(r * nchunk, nchunk)], sem.at[slot])

    def start_tile(tile, slot):
        def body(r, carry):
            row_copy(tile, slot, 2 * r).start(priority=0)
            row_copy(tile, slot, 2 * r + 1).start(priority=1)
            return carry
        lax.fori_loop(0, rows // 2, body, 0, unroll=4)

    def wait_tile(tile, slot):
        def body(r, carry):
            row_copy(tile, slot, r).wait()
            return carry
        lax.fori_loop(0, rows, body, 0, unroll=8)

    @pl.when(j == 0)
    def _():
        for c in weight_copies(be_ref[0], 0):
            c.start()
        ordinal[0] = 0

        def clear(r, carry):
            tok[r] = 0
            return carry
        lax.fori_loop(0, tok.shape[0], clear, 0, unroll=8)

        def place(a, carry):
            tok[dest_ref[a]] = lax.shift_right_logical(a, 1)
            return carry
        lax.fori_loop(0, n_assign, place, 0, unroll=8)
        start_tile(0, 0)

    slot = lax.rem(j, 2)
    active = j < n_used
    expert = be_ref[j]
    changed = (j == 0) | (expert != be_ref[jnp.maximum(j - 1, 0)])

    @pl.when(active & changed)
    def _():
        ws = lax.rem(ordinal[0], 2)
        for c in weight_copies(expert, ws):
            c.wait()
        wgb[...] = wgf[ws].astype(BF16)
        wub[...] = wuf[ws].astype(BF16)
        wdb[...] = wdf[ws].astype(BF16)
        nxt = lax.while_loop(lambda t: (t < n_used) & (be_ref[jnp.minimum(t, n_used - 1)] == expert),
                             lambda t: t + 1, j + 1)

        @pl.when(nxt < n_used)
        def _():
            for c in weight_copies(be_ref[nxt], 1 - ws):
                c.start()

        ordinal[0] = ordinal[0] + 1

    @pl.when(active)
    def _():
        @pl.when(j + 1 < n_used)
        def _():
            start_tile(j + 1, 1 - slot)

        wait_tile(j, slot)
        x = jnp.concatenate([xbuf[slot, pl.ds(s, rows, stride=nchunk), :] for s in range(nchunk)],
                            axis=1).astype(BF16)
        a = _dot(x, wgb[...])
        u = _dot(x, wub[...])
        y = _dot((jax.nn.silu(a) * u).astype(BF16), wdb[...])
        _store_rows(o_ref, y, rows, nchunk)

    @pl.when(jnp.logical_not(active))
    def _():
        o_ref[...] = jnp.zeros(o_ref.shape, F32)


def _experts(dest, blk_e, n_used, h3, wg, wu, wd, rows):
    d, de = wg.shape[1], wg.shape[2]
    nchunk = d // LANES
    nbt = blk_e.shape[0]
    grid_spec = pltpu.PrefetchScalarGridSpec(
        num_scalar_prefetch=3, grid=(nbt,),
        in_specs=[pl.BlockSpec(memory_space=pl.ANY)] * 4,
        out_specs=pl.BlockSpec((rows * nchunk, LANES), lambda j, dst, be, nu: (j, 0)),
        scratch_shapes=[pltpu.SMEM((nbt * rows,), I32), pltpu.SMEM((1,), I32),
                        pltpu.VMEM((2, rows * nchunk, LANES), F32), pltpu.SemaphoreType.DMA((2,)),
                        pltpu.VMEM((2, d, de), F32), pltpu.VMEM((2, d, de), F32), pltpu.VMEM((2, de, d), F32),
                        pltpu.SemaphoreType.DMA((2, 3)),
                        pltpu.VMEM((d, de), BF16), pltpu.VMEM((d, de), BF16), pltpu.VMEM((de, d), BF16)])
    return pl.pallas_call(
        functools.partial(_expert_kernel, rows=rows, nchunk=nchunk, n_assign=dest.shape[0]),
        out_shape=jax.ShapeDtypeStruct((nbt * rows * nchunk, LANES), F32),
        grid_spec=grid_spec,
        compiler_params=_cparams(("arbitrary",)),
        name="moe_experts",
    )(dest, blk_e, n_used, h3, wg, wu, wd)


def _combine_kernel(dest_ref, ys_ref, rt_ref, x1_ref, p_ref, gp_ref, wpg_ref, wpl_ref, gfin_ref, o_ref,
                    gbuf, sem, *, tm, nchunk, final):
    i = pl.program_id(0)
    slot = lax.rem(i, 2)

    def copy(tile, sl, j):
        return pltpu.make_async_copy(ys_ref.at[dest_ref[tile * (2 * tm) + j]],
                                     gbuf.at[sl, pl.ds(j * nchunk, nchunk)], sem.at[sl])

    def start_tile(tile, sl):
        def body(j, carry):
            copy(tile, sl, 2 * j).start(priority=0)
            copy(tile, sl, 2 * j + 1).start(priority=1)
            return carry
        lax.fori_loop(0, tm, body, 0, unroll=4)

    @pl.when(i == 0)
    def _():
        start_tile(0, 0)

    @pl.when(i + 1 < pl.num_programs(0))
    def _():
        start_tile(i + 1, 1 - slot)

    def drain(j, carry):
        copy(i, slot, j).wait()
        return carry

    lax.fori_loop(0, 2 * tm, drain, 0, unroll=8)
    rt = rt_ref[...]
    lane = lax.broadcasted_iota(I32, rt.shape, 1)
    y0 = jnp.concatenate([gbuf[slot, pl.ds(s, tm, stride=2 * nchunk), :] for s in range(nchunk)], axis=1)
    y1 = jnp.concatenate([gbuf[slot, pl.ds(nchunk + s, tm, stride=2 * nchunk), :] for s in range(nchunk)], axis=1)
    x2 = x1_ref[...] + (_lane_pick(rt, lane, 2) * y0 + _lane_pick(rt, lane, 3) * y1)
    gate = jax.nn.sigmoid(_dot(_rms(x2, gp_ref[...]).astype(BF16), wpg_ref[...]))
    x3 = x2 + gate * _dot(p_ref[...].astype(BF16), wpl_ref[...])
    o_ref[...] = _rms(x3, gfin_ref[...]) if final else x3


def _combine(dest, ys3, route, x1, pa, gple, wpg, wpl, gfin, final):
    n, d = x1.shape
    nchunk = d // LANES
    tm = _pick_tile(n, 320, 16)
    dp = pa.shape[1]
    row = lambda i, dst: (i, 0)
    const2 = lambda i, dst: (0, 0)
    grid_spec = pltpu.PrefetchScalarGridSpec(
        num_scalar_prefetch=1, grid=(n // tm,),
        in_specs=[pl.BlockSpec(memory_space=pl.ANY),
                  pl.BlockSpec((tm, LANES), row), pl.BlockSpec((tm, d), row), pl.BlockSpec((tm, dp), row),
                  pl.BlockSpec((1, d), const2), pl.BlockSpec((d, d), const2), pl.BlockSpec((dp, d), const2),
                  pl.BlockSpec((1, d), const2)],
        out_specs=pl.BlockSpec((tm, d), row),
        scratch_shapes=[pltpu.VMEM((2, 2 * tm * nchunk, LANES), F32), pltpu.SemaphoreType.DMA((2,))])
    return pl.pallas_call(
        functools.partial(_combine_kernel, tm=tm, nchunk=nchunk, final=final),
        out_shape=jax.ShapeDtypeStruct((n, d), F32),
        grid_spec=grid_spec,
        compiler_params=_cparams(("arbitrary",)),
        name="moe_combine_tail",
    )(dest, ys3, route, x1, pa, gple, wpg, wpl, gfin)


def _rope_tables(pos):
    inv = ROPE_THETA ** (-jnp.arange(ROPE_HALF, dtype=F32) / ROPE_HALF)
    ang = pos.astype(F32)[:, None] * inv[None, :]
    cos, sin = jnp.cos(ang), jnp.sin(ang)
    n = pos.shape[0]
    rest = DA_DH - 2 * ROPE_HALF
    z = jnp.zeros((n, ROPE_HALF), F32)
    zr = jnp.zeros((n, rest), F32)
    c64 = jnp.concatenate([cos, cos, jnp.ones((n, rest), F32)], axis=1)
    a64 = jnp.concatenate([-sin, z, zr], axis=1)
    b64 = jnp.concatenate([z, sin, zr], axis=1)
    rep = LANES // DA_DH
    return jnp.tile(c64, (1, rep)), jnp.tile(a64, (1, rep)), jnp.tile(b64, (1, rep))


def _column_tables():
    ones = lambda n: jnp.ones((n,), F32)
    zeros = lambda n: jnp.zeros((n,), F32)
    gw = NSA_KV * NSA_DH
    rmask = jnp.concatenate([ones(C_DA_Q), ones(C_DA_K), zeros(C_DA_V), ones(C_NSA_Q)]
                            + [ones(gw), zeros(gw)] * 3)
    cscale = jnp.concatenate([ones(C_DA_Q) * DA_DH ** -0.5, ones(C_DA_K), ones(C_DA_V),
                              ones(C_NSA_Q) * NSA_DH ** -0.5, ones(C_NSA_KV)])
    return rmask[None, :], cscale[None, :]


def _block_expand(width, nkeys):
    blk = jnp.arange(width, dtype=I32)[:, None]
    key = jnp.arange(nkeys, dtype=I32)[None, :]
    return (key // BLOCK == blk).astype(BF16)


def _pad_rows(x, axis, size):
    pad = [(0, 0)] * x.ndim
    pad[axis] = (0, size - x.shape[axis])
    return jnp.pad(x, pad)


def _cmp_layout(kc, b, nc, width):
    x = kc.reshape(b, nc, NSA_KV, NSA_DH).transpose(0, 2, 1, 3)
    return _pad_rows(x, 2, width).astype(BF16)


def kernel(x_prompt, x_sample, cache_da_kv, cache_nsa_kv, cache_win_kv, page_table, p_prompt, p_sample,
           g_mix, w_in, da_lambda, da_subln, nsa_cpos, nsa_cmp, w_da_up, w_nsa_up, w_o, g_ffn,
           w_rg, b_rg, w_re, b_re, w_e_gate, w_e_up, w_e_down, g_ple, w_ple_gate, w_ple, g_final):
    b, s, d = x_prompt.shape
    nb, t4, _ = x_sample.shape
    depth = g_mix.shape[0]
    npr, nsm = b * s, nb * t4
    n = npr + nsm
    page = cache_da_kv.shape[2]
    npages = page_table.shape[1]
    past = npages * page
    wb = cache_win_kv.shape[2]
    nchunk = d // LANES
    rq = t4 * DA_REP
    assert rq % 8 == 0 and page % BLOCK == 0 and s % BLOCK == 0 and t4 <= page and d % LANES == 0

    xa = jnp.concatenate([x_prompt.reshape(npr, d), x_sample.reshape(nsm, d)], axis=0)
    pos = jnp.concatenate([jnp.tile(jnp.arange(s, dtype=I32), b), jnp.tile(past + jnp.arange(t4, dtype=I32), nb)])
    cos, sa, sb = _rope_tables(pos)
    rmask, cscale = _column_tables()

    nc_p, ns_p = s // BLOCK, max(-(-s // BLOCK), N_SELECT)
    tot_s = past + t4
    nc_s, ns_s = tot_s // BLOCK, max(-(-tot_s // BLOCK), N_SELECT)
    wid_p, wid_s = _round_up(ns_p, LANES), _round_up(ns_s, LANES)
    emat_p = _block_expand(wid_p, s)
    pg_s, nsteps_s = _page_steps(npages)
    emat_s = _block_expand(wid_s, (nsteps_s + 1) * pg_s * page)

    col = jnp.arange(LANES, dtype=I32)[:, None]
    lane = jnp.arange(NSA_HEADS * NSA_DH, dtype=I32)[None, :]
    gate_expand = jnp.stack([(col == br * NSA_HEADS + lane // NSA_DH) for br in range(3)]).astype(BF16)

    n_assign = 2 * n
    shift = int(math.log2(EXPERT_ROWS))
    nbt = -(-n_assign // EXPERT_ROWS) + N_EXPERTS
    tm_rank = _pick_tile(n, 640, 16)
    tri = jnp.tril(jnp.ones((tm_rank, tm_rank), BF16), -1)

    outs = {k: [] for k in ("da_p", "da_s", "ns_p", "ns_s", "win_p", "win_s")}
    for i in range(depth):
        lam_init = 0.8 - 0.6 * math.exp(-0.3 * i)
        wi = w_in[i]
        g1 = g_mix[i][None, :]
        zf, zb = _proj_rope(xa, g1, wi[:, :S5].astype(BF16), cos, sa, sb, rmask, cscale)
        wg_cols = wi[:, S5:S6].reshape(d, NSA_HEADS, 3).transpose(0, 2, 1).reshape(d, C_NSA_G)
        gns = _gate_proj(xa, g1, _pad_rows(wg_cols, 1, LANES).astype(BF16))
        gmg = _gate_proj(xa, g1, wi[:, S6:].astype(BF16))
        lam_p = da_lambda[i].astype(F32)
        subln = da_subln[i][None, :].astype(F32)
        wp = jnp.tile(nsa_cpos[i].astype(F32), (1, 1, NSA_KV))
        wcm = nsa_cmp[i].astype(F32)
        zc = jnp.zeros_like(wcm)
        wc = jnp.concatenate([jnp.concatenate([wcm, zc], axis=2), jnp.concatenate([zc, wcm], axis=2)], axis=1)

        zs = zb[npr:].reshape(nb, t4, S5)
        kn0, kn1 = S3 - S1, S3 - S1 + S5 - S4
        zfs = zf[npr:].reshape(nb, t4, kn1)
        q6 = zs[..., :S1].reshape(nb, t4, DA_KV, DA_REP, 2, DA_DH).transpose(0, 2, 4, 1, 3, 5)
        eye_g = jnp.eye(DA_KV, dtype=BF16)
        eye_c = jnp.eye(2, dtype=BF16)
        qbd = (q6[:, :, :, :, :, None, None, :] * eye_g[None, :, None, None, None, :, None, None]
               * eye_c[None, None, :, None, None, None, :, None]).reshape(nb, DA_KV * 2 * rq, DA_KV * 2 * DA_DH)
        cda = cache_da_kv[i].reshape(cache_da_kv.shape[1], page * 2 * DA_KV, DA_DV)
        new_da = _pad_rows(zfs[..., :kn0], 1, page).reshape(nb, page * 2 * DA_KV, DA_DV)
        oda_s = _da_decode(page_table, qbd, cda, new_da, lam_p, subln, lam_init)
        oda_s = oda_s.reshape(nb, DA_KV, t4, DA_REP, DA_DV).transpose(0, 2, 1, 3, 4).reshape(nsm, DA_HEADS * DA_DV)

        qns = zs[..., S3:S4].reshape(nb, t4, NSA_KV, NSA_REP, NSA_DH).transpose(0, 2, 1, 3, 4)
        eye_n = jnp.eye(NSA_KV, dtype=BF16)
        qnb = (qns[:, :, :, :, None, :] * eye_n[None, :, None, None, :, None]
               ).reshape(nb, NSA_KV * t4 * NSA_REP, LANES)
        cns_t = cache_nsa_kv[i].transpose(0, 2, 3, 4, 1)
        cwin_t = cache_win_kv[i].transpose(0, 2, 3, 4, 1)
        kw = NSA_KV * NSA_DH
        wt = jnp.tile(nsa_cpos[i].astype(F32).transpose(0, 2, 1)[:, None],
                      (1, NSA_KV, 1, page // BLOCK)).reshape(2 * kw, page)
        ocmp_s, sel_s = _cmp_decode(page_table, qns.reshape(nb, NSA_KV, t4 * NSA_REP, NSA_DH), cns_t, wt,
                                    nsa_cmp[i].astype(F32).transpose(0, 2, 1),
                                    past=past, t4=t4, nc=nc_s, ns=ns_s, width=wid_s)
        ocmp_s = ocmp_s.reshape(nb, NSA_KV, t4, NSA_REP, NSA_DH).transpose(0, 2, 1, 3, 4
                                                                             ).reshape(nsm, NSA_HEADS * NSA_DH)
        selrows = jnp.broadcast_to(sel_s[:, :, :, None, :], (nb, NSA_KV, t4, NSA_REP, wid_s)
                                   ).reshape(nb, NSA_KV * t4 * NSA_REP, wid_s)
        new_t = lambda x: _pad_rows(x.reshape(nb, t4, 2, NSA_KV, NSA_DH), 1, page).transpose(0, 2, 3, 4, 1)
        oslc_s = _slc_decode(page_table, qnb, selrows, emat_s, cns_t, new_t(zfs[..., kn0 + 2 * kw:kn0 + 4 * kw]),
                             past=past, t4=t4)
        owin_s = _win_decode(qnb, cwin_t, new_t(zfs[..., kn0 + 4 * kw:]), past=past, t4=t4)

        def halves(o):
            o = o.reshape(nb, NSA_KV, t4, NSA_REP, NSA_KV, NSA_DH)
            o = jnp.stack([o[:, g, :, :, g] for g in range(NSA_KV)], axis=2)
            return o.reshape(nsm, NSA_HEADS * NSA_DH)

        oslc_s, owin_s = halves(oslc_s), halves(owin_s)

        seeded = lambda rows: jnp.concatenate([jnp.zeros((npr, rows.shape[1]), rows.dtype), rows], axis=0)
        oda = _da_flash(zb, seeded(oda_s.astype(BF16)), b, s, lam_p, subln, lam_init)
        kc, vc = _compress_prompt(zf, wp, wc, b, s)
        kc2 = _cmp_layout(kc, b, nc_p, wid_p)
        kc2 = jnp.concatenate([kc2, kc2], axis=-1)
        vcl = _cmp_layout(vc, b, nc_p, wid_p)
        vz = jnp.zeros_like(vcl)
        vc2 = jnp.stack([jnp.concatenate([vcl, vz], axis=-1), jnp.concatenate([vz, vcl], axis=-1)], axis=2)
        ocmp, sel_p = _cmp_select(zb, kc2, vc2, seeded(ocmp_s), b, s, nc=nc_p, ns=ns_p)
        kvp = zb[:npr, S4:S5].reshape(npr, 6, NSA_KV, NSA_DH)
        dup = lambda k: jnp.concatenate([k, k], axis=-1).reshape(npr, NSA_KV * LANES)
        with_ones = lambda v: jnp.concatenate([v, jnp.ones_like(v)], axis=-1).reshape(npr, NSA_KV * LANES)
        oslc = _nsa_flash(zb, dup(kvp[:, 2]), with_ones(kvp[:, 3]), seeded(oslc_s), b, s, sel_p, emat_p)
        owin = _nsa_flash(zb, dup(kvp[:, 4]), with_ones(kvp[:, 5]), seeded(owin_s), b, s)

        mrg =_merge(oda, ocmp, oslc, owin, gns, gmg, gate_expand,
                     w_da_up[i].astype(BF16), w_nsa_up[i].astype(BF16))
        wr = _pad_rows(jnp.concatenate([w_rg[i], w_re[i]], axis=1).astype(F32), 1, LANES)
        br = _pad_rows(jnp.concatenate([b_rg[i], b_re[i]]).astype(F32)[None, :], 1, LANES)
        x1, hrows, route = _wo_route(mrg, xa, w_o[i].astype(BF16), g_ffn[i][None, :], wr, br)
        rank, sizes = _expert_ranks(route, tri)
        dest, be, nu = _destinations(route, rank, sizes, nbt, shift)
        dest_flat = dest[:, :2].reshape(n_assign)
        ys = _experts(dest_flat, be[:nbt, 0], nu[0, :1], hrows.reshape(n, nchunk, LANES),
                      w_e_gate[i], w_e_up[i], w_e_down[i], EXPERT_ROWS)
        pa = jnp.concatenate([p_prompt[i].reshape(npr, -1), p_sample[i].reshape(nsm, -1)], axis=0)
        xa = _combine(dest_flat, ys.reshape(nbt * EXPERT_ROWS, nchunk, LANES), route, x1, pa,
                      g_ple[i][None, :], w_ple_gate[i].astype(BF16), w_ple[i].astype(BF16),
                      g_final[None, :], i == depth - 1)

        zfp = zf[:npr].reshape(b, s, kn1)
        outs["da_p"].append(zfp[..., :kn0].reshape(b, s, 2, DA_KV, DA_DV))
        outs["da_s"].append(zfs[..., :kn0].reshape(nb, t4, 2, DA_KV, DA_DV))
        outs["ns_p"].append(zfp[..., kn0:kn0 + 4 * kw].reshape(b, s, 4, NSA_KV, NSA_DH))
        outs["ns_s"].append(zfs[..., kn0:kn0 + 4 * kw].reshape(nb, t4, 4, NSA_KV, NSA_DH))
        win_p = zfp[..., kn0 + 4 * kw:].reshape(b, s, 2, NSA_KV, NSA_DH)
        outs["win_p"].append(win_p[:, s - min(WINDOW, s):])
        win_s = jnp.concatenate([cache_win_kv[i], zfs[..., kn0 + 4 * kw:].reshape(nb, t4, 2, NSA_KV, NSA_DH)], axis=1)
        outs["win_s"].append(win_s[:, win_s.shape[1] - wb:])

    y_prompt = xa[:npr].reshape(b, s, d)
    y_sample = xa[npr:].reshape(nb, t4, d)
    return (y_prompt, y_sample, jnp.stack(outs["da_p"]), jnp.stack(outs["da_s"]), jnp.stack(outs["ns_p"]),
            jnp.stack(outs["ns_s"]), jnp.stack(outs["win_p"]), jnp.stack(outs["win_s"]))
```

```python
import functools
import math

import jax
import jax.numpy as jnp
from jax import lax
from jax.experimental import pallas as pl
from jax.experimental.pallas import tpu as pltpu

F32 = jnp.float32
BF16 = jnp.bfloat16
I32 = jnp.int32
HIGHEST = lax.Precision.HIGHEST

DA_HEADS, DA_KV, DA_REP, DA_DH = 8, 4, 2, 64
DA_DV = 2 * DA_DH
NSA_HEADS, NSA_KV, NSA_REP, NSA_DH = 16, 2, 8, 64
BLOCK = 64
BLOCK_SHIFT = 6
N_SELECT = 16
WINDOW = 512
FORCE_BONUS = 1000.0
ROPE_THETA = 500000.0
ROPE_HALF = 8
N_GROUPS, EXP_PER_GROUP = 8, 8
N_EXPERTS = N_GROUPS * EXP_PER_GROUP
EPS = 1e-6
NEG = -1e30
LOWEST = -3e38

LANES = 128
VMEM_LIMIT_BYTES = 56 * 1024 * 1024

C_DA_Q = DA_HEADS * 2 * DA_DH
C_DA_K = DA_KV * 2 * DA_DH
C_DA_V = DA_KV * DA_DV
C_NSA_Q = NSA_HEADS * NSA_DH
C_NSA_KV = 6 * NSA_KV * NSA_DH
C_NSA_G = 3 * NSA_HEADS
S1 = C_DA_Q
S2 = S1 + C_DA_K
S3 = S2 + C_DA_V
S4 = S3 + C_NSA_Q
S5 = S4 + C_NSA_KV
S6 = S5 + C_NSA_G

EXPERT_ROWS = 128
DECODE_PAGES = 16
PROJ_TILE = 768
PROJ_F32_FROM = 768
FLASH_TILE = 512
FLASH_ROWS = 128
FLASH_KEYS = 256


def _cparams(sem):
    return pltpu.CompilerParams(dimension_semantics=sem, vmem_limit_bytes=VMEM_LIMIT_BYTES)


def _pick_tile(n, target, mult):
    best = None
    for t in range(mult, min(n, target) + 1, mult):
        if n % t == 0:
            best = t
    return n if best is None else best


def _round_up(x, m):
    return (x + m - 1) // m * m


def _nt(a, b):
    return lax.dot_general(a, b, (((1,), (1,)), ((), ())), preferred_element_type=F32)


def _dot(a, b, **kw):
    return jnp.dot(a, b, preferred_element_type=F32, **kw)


def _rms(x, g):
    ms = jnp.mean(x * x, axis=-1, keepdims=True)
    return x * lax.rsqrt(ms + EPS) * g


def _proj_rope_kernel(x_ref, g_ref, w_ref, cos_ref, sa_ref, sb_ref, rm_ref, cs_ref,
                      of_ref, ob_ref, h_scr, *, n_chunks, skip):
    @pl.when(pl.program_id(1) == 0)
    def _():
        h_scr[...] = _rms(x_ref[...], g_ref[...]).astype(BF16)

    j = pl.program_id(1)
    z = _dot(h_scr[...], w_ref[...])
    c, sa, sb = cos_ref[...], sa_ref[...], sb_ref[...]
    outs = []
    for k in range(n_chunks):
        sl = slice(k * LANES, (k + 1) * LANES)
        zc = z[:, sl]
        rot = zc * c + pltpu.roll(zc, LANES - ROPE_HALF, 1) * sa + pltpu.roll(zc, ROPE_HALF, 1) * sb
        out = jnp.where(rm_ref[:, sl] > 0.5, rot, zc) * cs_ref[:, sl]
        ob_ref[:, sl] = out.astype(BF16)
        outs.append(out)

    @pl.when(j >= skip)
    def _():
        for k, out in enumerate(outs):
            of_ref[:, k * LANES:(k + 1) * LANES] = out


def _proj_rope(xa, g, w, cos, sa, sb, rmask, cscale):
    n, d = xa.shape
    ca = w.shape[1]
    tm = _pick_tile(n, 640, 16)
    tn = PROJ_TILE
    skip = PROJ_F32_FROM // tn
    row = lambda i, j: (i, 0)
    col = lambda i, j: (0, j)
    fcol = lambda i, j: (i, jnp.maximum(j - skip, 0))
    return pl.pallas_call(
        functools.partial(_proj_rope_kernel, n_chunks=tn // LANES, skip=skip),
        out_shape=(jax.ShapeDtypeStruct((n, ca - PROJ_F32_FROM), F32), jax.ShapeDtypeStruct((n, ca), BF16)),
        grid=(n // tm, ca // tn),
        in_specs=[pl.BlockSpec((tm, d), row), pl.BlockSpec((1, d), lambda i, j: (0, 0)),
                  pl.BlockSpec((d, tn), col),
                  pl.BlockSpec((tm, LANES), row), pl.BlockSpec((tm, LANES), row),
                  pl.BlockSpec((tm, LANES), row),
                  pl.BlockSpec((1, tn), col), pl.BlockSpec((1, tn), col)],
        out_specs=(pl.BlockSpec((tm, tn), fcol),
                   pl.BlockSpec((tm, tn), lambda i, j: (i, j))),
        scratch_shapes=[pltpu.VMEM((tm, d), BF16)],
        compiler_params=_cparams(("parallel", "arbitrary")),
        name="proj_rope",
    )(xa, g, w, cos, sa, sb, rmask, cscale)


def _gate_kernel(x_ref, g_ref, w_ref, o_ref, h_scr):
    @pl.when(pl.program_id(1) == 0)
    def _():
        h_scr[...] = _rms(x_ref[...], g_ref[...]).astype(BF16)

    o_ref[...] = jax.nn.sigmoid(_dot(h_scr[...], w_ref[...]))


def _gate_proj(xa, g, w):
    n, d = xa.shape
    ca = w.shape[1]
    tm = _pick_tile(n, 640, 16)
    tn = _pick_tile(ca, 1024, LANES)
    return pl.pallas_call(
        _gate_kernel,
        out_shape=jax.ShapeDtypeStruct((n, ca), F32),
        grid=(n // tm, ca // tn),
        in_specs=[pl.BlockSpec((tm, d), lambda i, j: (i, 0)), pl.BlockSpec((1, d), lambda i, j: (0, 0)),
                  pl.BlockSpec((d, tn), lambda i, j: (0, j))],
        out_specs=pl.BlockSpec((tm, tn), lambda i, j: (i, j)),
        scratch_shapes=[pltpu.VMEM((tm, d), BF16)],
        compiler_params=_cparams(("parallel", "arbitrary")),
        name="gate_proj",
    )(xa, g, w)


def _softmax_block(s, mask, v, m_old, l_old, acc_old, v_is_transposed=False):
    if mask is not None:
        s = jnp.where(mask, s, NEG)
    m_new = jnp.maximum(m_old, jnp.max(s, axis=-1, keepdims=True))
    p = jnp.exp(s - m_new)
    if mask is not None:
        p = jnp.where(mask, p, 0.0)
    alpha = jnp.exp(m_old - m_new)
    l_new = alpha * l_old + jnp.sum(p, axis=-1, keepdims=True)
    pv = _nt(p.astype(v.dtype), v) if v_is_transposed else _dot(p.astype(v.dtype), v)
    return m_new, l_new, alpha * acc_old + pv


def _flash_update(s, mask, v, m_ref, l_ref, acc_ref, idx, v_is_transposed=False):
    m, l, acc = _softmax_block(s, mask, v, m_ref[idx], l_ref[idx], acc_ref[idx], v_is_transposed)
    m_ref[idx] = m
    l_ref[idx] = l
    acc_ref[idx] = acc


def _flash_init(m_ref, l_ref, acc_ref):
    m_ref[...] = jnp.full(m_ref.shape, NEG, F32)
    l_ref[...] = jnp.zeros(l_ref.shape, F32)
    acc_ref[...] = jnp.zeros(acc_ref.shape, F32)


def _da_lambda(lam_ref, lam_init):
    a = jnp.sum(lam_ref[0:1, :] * lam_ref[1:2, :], axis=-1, keepdims=True)
    b = jnp.sum(lam_ref[2:3, :] * lam_ref[3:4, :], axis=-1, keepdims=True)
    return jnp.exp(a) - jnp.exp(b) + lam_init


def _da_finish(a0, a1, lam, sub, lam_init):
    o = a0 - lam * a1
    return _rms(o, sub) * (1.0 - lam_init)


def _flash_rows(s, bias, v1, m_ref, acc_ref, idx):
    if bias is not None:
        s = s + bias
    m_old = m_ref[idx]
    m_new = jnp.maximum(m_old, jnp.max(s, axis=-1, keepdims=True))
    p = jnp.exp(s - jnp.tile(m_new, (1, s.shape[1] // LANES)))
    alpha = jnp.exp(m_old - m_new)
    acc_ref[idx] = jnp.tile(alpha, (1, v1.shape[1] // LANES)) * acc_ref[idx] + _dot(p.astype(v1.dtype), v1)
    m_ref[idx] = m_new


def _tri_tables(nq):
    qi = [q for q in range(nq) for _ in range(q + 1)]
    ki = [k for q in range(nq) for k in range(q + 1)]
    return jnp.asarray(qi, I32), jnp.asarray(ki, I32)


def _da_flash_kernel(qi_ref, ki_ref, lam_ref, sub_ref, q_ref, k_ref, v_ref, base_ref, o_ref, m_scr, acc_scr,
                     *, tq, rb, kb, lam_init):
    del base_ref
    t = pl.program_id(2)
    qi = qi_ref[t]
    ki = ki_ref[t]

    @pl.when(ki == 0)
    def _():
        m_scr[...] = jnp.full(m_scr.shape, NEG, F32)
        acc_scr[...] = jnp.zeros(acc_scr.shape, F32)

    upper = lax.broadcasted_iota(I32, (1, LANES), 1) >= DA_DH

    def step(diag):
        k = k_ref[...]
        v1 = jnp.concatenate([v_ref[...], jnp.ones((tq, DA_DV), BF16)], axis=1)
        for r in range(DA_REP):
            for r0 in range(0, tq, rb):
                nk = r0 + rb if diag else tq
                q = q_ref[pl.ds(r0, rb), r * DA_DV:(r + 1) * DA_DV]
                qc = [jnp.where(upper if c == 1 else jnp.logical_not(upper), q, jnp.zeros_like(q))
                      for c in range(2)]
                for c0 in range(0, nk, kb):
                    c1 = min(c0 + kb, nk)
                    bias = None
                    if diag and c1 > r0:
                        row = r0 - c0 + lax.broadcasted_iota(I32, (rb, c1 - c0), 0)
                        bias = jnp.where(lax.broadcasted_iota(I32, (rb, c1 - c0), 1) <= row, 0.0, NEG)
                    for c in range(2):
                        _flash_rows(_nt(qc[c], k[c0:c1]), bias, v1[c0:c1], m_scr, acc_scr,
                                    (c, pl.ds(r * tq + r0, rb)))

    @pl.when(ki < qi)
    def _():
        step(False)

    @pl.when(ki == qi)
    def _():
        step(True)
        lam = _da_lambda(lam_ref, lam_init)
        a0 = acc_scr[0, :, :DA_DV] / acc_scr[0, :, DA_DV:]
        a1 = acc_scr[1, :, :DA_DV] / acc_scr[1, :, DA_DV:]
        o = _da_finish(a0, a1, lam, sub_ref[...], lam_init)
        for r in range(DA_REP):
            o_ref[:, r * DA_DV:(r + 1) * DA_DV] = o[r * tq:(r + 1) * tq].astype(o_ref.dtype)


def _da_flash(zb, base, b, s, lam_p, subln, lam_init):
    tq = _pick_tile(s, FLASH_TILE, 16)
    rb = _pick_tile(tq, FLASH_ROWS, 16)
    kb = _pick_tile(tq, FLASH_KEYS, LANES)
    nq = s // tq
    qtab, ktab = _tri_tables(nq)
    kcol, vcol = S1 // DA_DV, S2 // DA_DV
    grid_spec = pltpu.PrefetchScalarGridSpec(
        num_scalar_prefetch=2, grid=(b, DA_KV, qtab.shape[0]),
        in_specs=[pl.BlockSpec((4, DA_DH), lambda bi, g, t, qt, kt: (0, 0)),
                  pl.BlockSpec((1, DA_DV), lambda bi, g, t, qt, kt: (0, 0)),
                  pl.BlockSpec((tq, DA_REP * DA_DV), lambda bi, g, t, qt, kt: (bi * nq + qt[t], g)),
                  pl.BlockSpec((tq, DA_DV), lambda bi, g, t, qt, kt: (bi * nq + kt[t], kcol + g)),
                  pl.BlockSpec((tq, DA_DV), lambda bi, g, t, qt, kt: (bi * nq + kt[t], vcol + g)),
                  pl.BlockSpec(memory_space=pl.ANY)],
        out_specs=pl.BlockSpec((tq, DA_REP * DA_DV), lambda bi, g, t, qt, kt: (bi * nq + qt[t], g)),
        scratch_shapes=[pltpu.VMEM((2, DA_REP * tq, LANES), F32),
                        pltpu.VMEM((2, DA_REP * tq, 2 * DA_DV), F32)])
    return pl.pallas_call(
        functools.partial(_da_flash_kernel, tq=tq, rb=rb, kb=kb, lam_init=lam_init),
        out_shape=jax.ShapeDtypeStruct(base.shape, base.dtype),
        grid_spec=grid_spec,
        input_output_aliases={7: 0},
        compiler_params=_cparams(("parallel", "parallel", "arbitrary")),
        name="da_flash",
    )(qtab, ktab, lam_p, subln, zb, zb, zb, base)


def _page_steps(npages):
    pg = _pick_tile(npages, DECODE_PAGES, 1)
    return pg, npages // pg


def _page_specs(block, npages, pg, col_block):
    nd = len(block) - 1

    def spec(u):
        def index(b, p, pt):
            idx = [pt[b, jnp.minimum(p * pg + u, npages - 1)]] + [0] * nd
            if col_block is not None:
                idx[1] = col_block
            return tuple(idx)
        return pl.BlockSpec(block, index)

    return [spec(u) for u in range(pg)]


def _da_decode_kernel(pt_ref, lam_ref, sub_ref, q_ref, *refs, nsteps, pg, page, rq, lam_init):
    del pt_ref
    c_refs = refs[:pg]
    n_ref, o_ref, m_scr, l_scr, acc_scr = refs[pg:]
    p = pl.program_id(1)
    per = 2 * DA_KV

    @pl.when(p == 0)
    def _():
        _flash_init(m_scr, l_scr, acc_scr)

    q = q_ref[0]
    rows = q.shape[0]

    def heads(ref, first):
        return jnp.concatenate([ref[0, pl.ds(first + g, page, stride=per), :] for g in range(DA_KV)],
                               axis=1).astype(BF16)

    def step(blocks, mask):
        k = jnp.concatenate([heads(r, 0) for r in blocks], axis=0)
        v = jnp.concatenate([heads(r, DA_KV) for r in blocks], axis=0)
        _flash_update(_nt(q, k), mask, v, m_scr, l_scr, acc_scr, 0)

    @pl.when(p < nsteps)
    def _():
        step(c_refs, None)

    @pl.when(p == nsteps)
    def _():
        row = lax.broadcasted_iota(I32, (rows, page), 0)
        col = lax.broadcasted_iota(I32, (rows, page), 1)
        step([n_ref], col <= (row % rq) // DA_REP)
        lam = _da_lambda(lam_ref, lam_init)
        a = acc_scr[0] / l_scr[0]
        for g in range(DA_KV):
            cols = slice(g * DA_DV, (g + 1) * DA_DV)
            a0 = a[g * 2 * rq:g * 2 * rq + rq, cols]
            a1 = a[g * 2 * rq + rq:(g + 1) * 2 * rq, cols]
            o_ref[0, g] = _da_finish(a0, a1, lam, sub_ref[...], lam_init)


def _da_decode(page_table, qbd, cache, newkv, lam_p, subln, lam_init):
    nb, rows, ck = qbd.shape
    npages = page_table.shape[1]
    prow = cache.shape[1]
    page = prow // (2 * DA_KV)
    rq = rows // (DA_KV * 2)
    pg, nsteps = _page_steps(npages)
    grid_spec = pltpu.PrefetchScalarGridSpec(
        num_scalar_prefetch=1, grid=(nb, nsteps + 1),
        in_specs=[pl.BlockSpec((4, DA_DH), lambda b, p, pt: (0, 0)),
                  pl.BlockSpec((1, DA_DV), lambda b, p, pt: (0, 0)),
                  pl.BlockSpec((1, rows, ck), lambda b, p, pt: (b, 0, 0))]
        + _page_specs((1, prow, DA_DV), npages, pg, None)
        + [pl.BlockSpec((1, prow, DA_DV), lambda b, p, pt: (b, 0, 0))],
        out_specs=pl.BlockSpec((1, DA_KV, rq, DA_DV), lambda b, p, pt: (b, 0, 0, 0)),
        scratch_shapes=[pltpu.VMEM((1, rows, 1), F32), pltpu.VMEM((1, rows, 1), F32),
                        pltpu.VMEM((1, rows, DA_KV * DA_DV), F32)])
    return pl.pallas_call(
        functools.partial(_da_decode_kernel, nsteps=nsteps, pg=pg, page=page, rq=rq, lam_init=lam_init),
        out_shape=jax.ShapeDtypeStruct((nb, DA_KV, rq, DA_DV), F32),
        grid_spec=grid_spec,
        compiler_params=_cparams(("parallel", "arbitrary")),
        name="da_decode",
    )(page_table, lam_p, subln, qbd, *([cache] * pg), newkv)


def _compress_kernel(k_ref, v_ref, wp_ref, wc_ref, ko_ref, vo_ref):
    for c, (src, dst) in enumerate(((k_ref, ko_ref), (v_ref, vo_ref))):
        x = src[...]
        t = x.shape[0]
        y = jnp.sum(x.reshape(t // BLOCK, BLOCK, LANES) * wp_ref[c][None], axis=1)
        dst[...] = _dot(y, wc_ref[c], precision=HIGHEST)


def _compress_prompt(zf, wp, wc, b, s):
    nc = s // BLOCK
    kc0 = (S4 - PROJ_F32_FROM) // LANES
    out = jax.ShapeDtypeStruct((b * nc, LANES), F32)
    return pl.pallas_call(
        _compress_kernel,
        out_shape=(out, out),
        grid=(b,),
        in_specs=[pl.BlockSpec((s, LANES), lambda i: (i, kc0)), pl.BlockSpec((s, LANES), lambda i: (i, kc0 + 1)),
                  pl.BlockSpec((2, BLOCK, LANES), lambda i: (0, 0, 0)),
                  pl.BlockSpec((2, LANES, LANES), lambda i: (0, 0, 0))],
        out_specs=(pl.BlockSpec((nc, LANES), lambda i: (i, 0)), pl.BlockSpec((nc, LANES), lambda i: (i, 0))),
        compiler_params=_cparams(("parallel",)),
        name="nsa_compress",
    )(zf, zf, wp, wc)


def _masked_probs(s, ok):
    s = jnp.where(ok, s, NEG)
    e = jnp.where(ok, jnp.exp(s - jnp.max(s, axis=-1, keepdims=True)), 0.0)
    den = jnp.sum(e, axis=-1, keepdims=True)
    return e / jnp.where(den > 0.0, den, 1.0)


def _select_blocks(imp, pos, ns):
    width = imp.shape[1]
    blk = lax.broadcasted_iota(I32, (1, width), 1)
    cur = lax.shift_right_logical(pos, BLOCK_SHIFT)
    forced = (blk == 0) | (blk == cur) | (blk == cur - 1)
    score = jnp.where(forced, imp + FORCE_BONUS, imp)
    score = jnp.where(blk <= cur, score, -1.0)
    score = jnp.where(blk < ns, score, 0.5 * LOWEST)
    sel = jnp.zeros(imp.shape, F32)
    for _ in range(N_SELECT):
        mx = jnp.max(score, axis=-1, keepdims=True)
        idx = jnp.min(jnp.where(score == mx, blk, width), axis=-1, keepdims=True)
        hit = blk == idx
        sel = jnp.where(hit, 1.0, sel)
        score = jnp.where(hit, LOWEST, score)
    return sel


def _half_mask(half):
    upper = lax.broadcasted_iota(I32, (1, LANES), 1) >= NSA_DH
    return upper if half == 1 else jnp.logical_not(upper)


def _head_q(q_ref, rows, r):
    q = q_ref[rows, (r // 2) * LANES:(r // 2 + 1) * LANES]
    return jnp.where(_half_mask(r % 2), q, jnp.zeros_like(q))


def _cmp_select_kernel(q_ref, kc_ref, vc_ref, base_ref, o_ref, sel_ref, *, tq, nc, ns, width):
    del base_ref
    qi = pl.program_id(1)
    pos = qi * tq + lax.broadcasted_iota(I32, (tq, 1), 0)
    blk = lax.broadcasted_iota(I32, (1, width), 1)
    cmp_ok = (blk < nc) & ((blk + 1) * BLOCK - 1 <= pos)
    hw = NSA_REP * NSA_DH
    for g in range(NSA_KV):
        kc = kc_ref[0, g]
        imp = jnp.zeros((tq, width), F32)
        for j in range(NSA_REP // 2):
            out = None
            for a in range(2):
                q = q_ref[:, g * hw + j * LANES:g * hw + (j + 1) * LANES]
                q = jnp.where(_half_mask(a), q, jnp.zeros_like(q))
                p = _masked_probs(_nt(q, kc), cmp_ok)
                imp = imp + p
                term = _dot(p.astype(BF16), vc_ref[0, g, a])
                out = term if out is None else out + term
            o_ref[:, g * hw + j * LANES:g * hw + (j + 1) * LANES] = out
        sel_ref[0, g] = _select_blocks(imp, pos, ns).astype(sel_ref.dtype)


def _cmp_decode_kernel(pt_ref, q_ref, wt_ref, wct_ref, *refs, nsteps, pg, page, past, t4, nc, width):
    del pt_ref
    c_refs = refs[:pg]
    o_ref, imp_ref, acc = refs[pg:]
    p = pl.program_id(1)
    kw = NSA_KV * NSA_DH
    cw = acc.shape[1]

    @pl.when(p == 0)
    def _():
        acc[...] = jnp.zeros(acc.shape, F32)

    z = jnp.concatenate([c_refs[u][0].reshape(2 * kw, page) * wt_ref[...] for u in range(pg)], axis=1)
    tokblk = lax.shift_right_logical(lax.broadcasted_iota(I32, (pg * page, cw), 0), BLOCK_SHIFT)
    lane = lax.broadcasted_iota(I32, (pg * page, cw), 1)
    place = jnp.where(lane == p * (pg * page // BLOCK) + tokblk, 1.0, 0.0).astype(BF16)
    total = None
    for _ in range(3):
        piece = z.astype(BF16)
        z = z - piece.astype(F32)
        term = _dot(piece, place)
        total = term if total is None else total + term
    acc[...] += total

    @pl.when(p == nsteps - 1)
    def _():
        y = acc[...]
        rows = t4 * NSA_REP
        pos = past + lax.broadcasted_iota(I32, (rows, 1), 0) // NSA_REP
        blk = lax.broadcasted_iota(I32, (1, cw), 1)
        cmp_ok = (blk < nc) & ((blk + 1) * BLOCK - 1 <= pos)
        for g in range(NSA_KV):
            kct = _dot(wct_ref[0], y[g * NSA_DH:(g + 1) * NSA_DH], precision=HIGHEST)
            vct = _dot(wct_ref[1], y[kw + g * NSA_DH:kw + (g + 1) * NSA_DH], precision=HIGHEST)
            pr = _masked_probs(_dot(q_ref[0, g], kct.astype(BF16)), cmp_ok)
            o_ref[0, g] = _nt(pr.astype(BF16), vct.astype(BF16))
            imp = jnp.sum(pr.reshape(t4, NSA_REP, cw), axis=1)
            if width > cw:
                imp = jnp.concatenate([imp, jnp.zeros((t4, width - cw), F32)], axis=1)
            imp_ref[0, g] = imp


def _select_decode_kernel(imp_ref, sel_ref, *, past, t4, ns):
    rows = imp_ref.shape[0]
    pos = past + lax.broadcasted_iota(I32, (rows, 1), 0) % t4
    sel_ref[...] = _select_blocks(imp_ref[...], pos, ns).astype(sel_ref.dtype)


def _cmp_decode(page_table, qst, cache_t, wt, wct, *, past, t4, nc, ns, width):
    nb, npages = page_table.shape
    page = cache_t.shape[4]
    pg, nsteps = _page_steps(npages)
    rows = t4 * NSA_REP
    cw = _round_up(nc, LANES)
    kw = NSA_KV * NSA_DH
    grid_spec = pltpu.PrefetchScalarGridSpec(
        num_scalar_prefetch=1, grid=(nb, nsteps),
        in_specs=[pl.BlockSpec((1, NSA_KV, rows, NSA_DH), lambda b, p, pt: (b, 0, 0, 0)),
                  pl.BlockSpec((2 * kw, page), lambda b, p, pt: (0, 0)),
                  pl.BlockSpec((2, NSA_DH, NSA_DH), lambda b, p, pt: (0, 0, 0))]
        + _page_specs((1, 2, NSA_KV, NSA_DH, page), npages, pg, 0),
        out_specs=(pl.BlockSpec((1, NSA_KV, rows, NSA_DH), lambda b, p, pt: (b, 0, 0, 0)),
                   pl.BlockSpec((1, NSA_KV, t4, width), lambda b, p, pt: (b, 0, 0, 0))),
        scratch_shapes=[pltpu.VMEM((2 * kw, cw), F32)])
    ocmp, imp = pl.pallas_call(
        functools.partial(_cmp_decode_kernel, nsteps=nsteps, pg=pg, page=page, past=past, t4=t4,
                          nc=nc, width=width),
        out_shape=(jax.ShapeDtypeStruct((nb, NSA_KV, rows, NSA_DH), F32),
                   jax.ShapeDtypeStruct((nb, NSA_KV, t4, width), F32)),
        grid_spec=grid_spec,
        compiler_params=_cparams(("parallel", "arbitrary")),
        name="nsa_cmp_decode",
    )(page_table, qst, wt, wct, *([cache_t] * pg))
    nrow = nb * NSA_KV * t4
    sel = pl.pallas_call(
        functools.partial(_select_decode_kernel, past=past, t4=t4, ns=ns),
        out_shape=jax.ShapeDtypeStruct((nrow, width), BF16),
        grid=(1,),
        in_specs=[pl.BlockSpec((nrow, width), lambda i: (0, 0))],
        out_specs=pl.BlockSpec((nrow, width), lambda i: (0, 0)),
        compiler_params=_cparams(("arbitrary",)),
        name="nsa_select_decode",
    )(imp.reshape(nrow, width))
    return ocmp, sel.reshape(nb, NSA_KV, t4, width)


def _cmp_select(zb, kc2, vc2, base, b, s, *, nc, ns):
    width = kc2.shape[2]
    tq = _pick_tile(s, 256, 16)
    nq = s // tq
    qcol = S3 // C_NSA_Q
    return pl.pallas_call(
        functools.partial(_cmp_select_kernel, tq=tq, nc=nc, ns=ns, width=width),
        out_shape=(jax.ShapeDtypeStruct(base.shape, base.dtype),
                   jax.ShapeDtypeStruct((b, NSA_KV, s, width), BF16)),
        grid=(b, nq),
        in_specs=[pl.BlockSpec((tq, C_NSA_Q), lambda bi, qi: (bi * nq + qi, qcol)),
                  pl.BlockSpec((1, NSA_KV, width, LANES), lambda bi, qi: (bi, 0, 0, 0)),
                  pl.BlockSpec((1, NSA_KV, 2, width, LANES), lambda bi, qi: (bi, 0, 0, 0, 0)),
                  pl.BlockSpec(memory_space=pl.ANY)],
        out_specs=(pl.BlockSpec((tq, C_NSA_Q), lambda bi, qi: (bi * nq + qi, 0)),
                   pl.BlockSpec((1, NSA_KV, tq, width), lambda bi, qi: (bi, 0, qi, 0))),
        input_output_aliases={3: 0},
        compiler_params=_cparams(("parallel", "parallel")),
        name="nsa_cmp_select",
    )(zb, kc2, vc2, base)


def _nsa_flash_kernel(qi_ref, ki_ref, *refs, mode, tq, rb, kb, nsteps, band):
    if mode == "slc":
        q_ref, k_ref, v_ref, sel_ref, e_ref, _, o_ref, m_scr, acc_scr = refs
    else:
        q_ref, k_ref, v_ref, _, o_ref, m_scr, acc_scr = refs
    t = pl.program_id(2)
    qi = qi_ref[t]
    ki = ki_ref[t]
    first = (t == 0) | (qi_ref[jnp.maximum(t - 1, 0)] != qi)
    last = (t == nsteps - 1) | (qi_ref[jnp.minimum(t + 1, nsteps - 1)] != qi)

    @pl.when(first)
    def _():
        m_scr[...] = jnp.full(m_scr.shape, NEG, F32)
        acc_scr[...] = jnp.zeros(acc_scr.shape, F32)

    if mode == "slc":
        picked = _dot(sel_ref[0, 0], e_ref[...])
    k = k_ref[...]
    v1 = v_ref[...]
    diag = qi == ki
    for r0 in range(0, tq, rb):
        rows = pl.ds(r0, rb)
        nk = _round_up(r0 + rb, LANES)

        def update(ncols, off_diag=False, r0=r0, rows=rows):
            for c0 in range(0, ncols, kb):
                c1 = min(c0 + kb, ncols)
                if off_diag and band == 1 and tq + r0 - (c1 - 1) >= WINDOW:
                    continue
                d = ((qi - ki) * tq + r0 - c0 + lax.broadcasted_iota(I32, (rb, c1 - c0), 0)
                     - lax.broadcasted_iota(I32, (rb, c1 - c0), 1))
                if mode == "slc":
                    visible = (picked[r0:r0 + rb, c0:c1] > 0.5) & (d >= 0)
                else:
                    visible = (d >= 0) & (d < WINDOW)
                bias = jnp.where(visible, 0.0, NEG)
                for h in range(NSA_REP):
                    s = _nt(_head_q(q_ref, rows, h), k[c0:c1])
                    _flash_rows(s, bias, v1[c0:c1], m_scr, acc_scr, (h, rows))

        if nk < tq or band == 1:
            pl.when(diag)(functools.partial(update, nk))
            pl.when(jnp.logical_not(diag))(functools.partial(update, tq, True))
        else:
            update(tq)

    @pl.when(last)
    def _():
        low = _half_mask(0)
        for j in range(NSA_REP // 2):
            a, bq = acc_scr[2 * j], acc_scr[2 * j + 1]
            num = jnp.where(low, a, pltpu.roll(bq, NSA_DH, 1))
            den = jnp.where(low, pltpu.roll(a, NSA_DH, 1), bq)
            o_ref[:, j * LANES:(j + 1) * LANES] = num / den


def _nsa_flash(zb, k2, v1, base, b, s, sel=None, emat=None):
    tq = _pick_tile(s, FLASH_TILE, 16)
    rb = _pick_tile(tq, FLASH_ROWS, 16)
    kb = _pick_tile(tq, FLASH_KEYS, LANES)
    nq = s // tq
    back = 0
    if sel is not None:
        mode = "slc"
        qtab, ktab = _tri_tables(nq)
    else:
        mode = "win"
        back = -(-WINDOW // tq)
        pairs = [(q, kk) for q in range(nq) for kk in range(max(q - back, 0), q + 1)]
        qtab = jnp.asarray([p[0] for p in pairs], I32)
        ktab = jnp.asarray([p[1] for p in pairs], I32)
    nsteps = qtab.shape[0]
    gw = NSA_REP * NSA_DH
    qcol = S3 // gw
    in_specs = [pl.BlockSpec((tq, gw), lambda bi, g, t, qt, kt: (bi * nq + qt[t], qcol + g)),
                pl.BlockSpec((tq, LANES), lambda bi, g, t, qt, kt: (bi * nq + kt[t], g)),
                pl.BlockSpec((tq, LANES), lambda bi, g, t, qt, kt: (bi * nq + kt[t], g))]
    args = [zb, k2, v1]
    if sel is not None:
        width = sel.shape[3]
        in_specs += [pl.BlockSpec((1, 1, tq, width), lambda bi, g, t, qt, kt: (bi, g, qt[t], 0)),
                     pl.BlockSpec((width, tq), lambda bi, g, t, qt, kt: (0, kt[t]))]
        args += [sel, emat]
    in_specs.append(pl.BlockSpec(memory_space=pl.ANY))
    args.append(base)
    grid_spec = pltpu.PrefetchScalarGridSpec(
        num_scalar_prefetch=2, grid=(b, NSA_KV, nsteps),
        in_specs=in_specs,
        out_specs=pl.BlockSpec((tq, gw), lambda bi, g, t, qt, kt: (bi * nq + qt[t], g)),
        scratch_shapes=[pltpu.VMEM((NSA_REP, tq, LANES), F32), pltpu.VMEM((NSA_REP, tq, LANES), F32)])
    return pl.pallas_call(
        functools.partial(_nsa_flash_kernel, mode=mode, tq=tq, rb=rb, kb=kb, nsteps=nsteps, band=back),
        out_shape=jax.ShapeDtypeStruct(base.shape, base.dtype),
        grid_spec=grid_spec,
        input_output_aliases={1 + len(args): 0},
        compiler_params=_cparams(("parallel", "parallel", "arbitrary")),
        name="nsa_" + mode,
    )(qtab, ktab, *args)


def _row_token(rows, cols, t4):
    row = lax.broadcasted_iota(I32, (rows, cols), 0)
    return (row // NSA_REP) % t4


def _kv_t(blocks):
    kw = NSA_KV * NSA_DH
    kt = [r[0, 0].reshape(kw, r.shape[4]) for r in blocks]
    vt = [r[0, 1].reshape(kw, r.shape[4]) for r in blocks]
    cat = lambda xs: (xs[0] if len(xs) == 1 else jnp.concatenate(xs, axis=1)).astype(BF16)
    return cat(kt), cat(vt)


def _slc_decode_kernel(pt_ref, q_ref, sel_ref, e_ref, *refs, nsteps, pg, page, past, t4):
    del pt_ref
    c_refs = refs[:pg]
    n_ref, o_ref, m_scr, l_scr, acc_scr = refs[pg:]
    p = pl.program_id(1)

    @pl.when(p == 0)
    def _():
        _flash_init(m_scr, l_scr, acc_scr)

    q = q_ref[0]
    rows = q.shape[0]

    def step(blocks):
        kt, vt = _kv_t(blocks)
        nk = kt.shape[1]
        kpos = p * (pg * page) + lax.broadcasted_iota(I32, (rows, nk), 1)
        picked = _dot(sel_ref[0], e_ref[:, :nk]) > 0.5
        mask = picked & (kpos <= past + _row_token(rows, nk, t4))
        _flash_update(_dot(q, kt), mask, vt, m_scr, l_scr, acc_scr, 0, v_is_transposed=True)

    @pl.when(p < nsteps)
    def _():
        step(c_refs)

    @pl.when(p == nsteps)
    def _():
        step([n_ref])
        o_ref[0] = acc_scr[0] / l_scr[0]


def _slc_decode(page_table, qnb, selrows, emat, cache_t, new_t, *, past, t4):
    nb, rows, _ = qnb.shape
    npages = page_table.shape[1]
    page = cache_t.shape[4]
    width = selrows.shape[2]
    pg, nsteps = _page_steps(npages)
    blk = (1, 2, NSA_KV, NSA_DH, page)
    grid_spec = pltpu.PrefetchScalarGridSpec(
        num_scalar_prefetch=1, grid=(nb, nsteps + 1),
        in_specs=[pl.BlockSpec((1, rows, LANES), lambda b, p, pt: (b, 0, 0)),
                  pl.BlockSpec((1, rows, width), lambda b, p, pt: (b, 0, 0)),
                  pl.BlockSpec((width, pg * page), lambda b, p, pt: (0, p))]
        + _page_specs(blk, npages, pg, 1)
        + [pl.BlockSpec(blk, lambda b, p, pt: (b, 0, 0, 0, 0))],
        out_specs=pl.BlockSpec((1, rows, LANES), lambda b, p, pt: (b, 0, 0)),
        scratch_shapes=[pltpu.VMEM((1, rows, 1), F32), pltpu.VMEM((1, rows, 1), F32),
                        pltpu.VMEM((1, rows, LANES), F32)])
    return pl.pallas_call(
        functools.partial(_slc_decode_kernel, nsteps=nsteps, pg=pg, page=page, past=past, t4=t4),
        out_shape=jax.ShapeDtypeStruct((nb, rows, LANES), F32),
        grid_spec=grid_spec,
        compiler_params=_cparams(("parallel", "arbitrary")),
        name="nsa_slc_decode",
    )(page_table, qnb, selrows, emat, *([cache_t] * pg), new_t)


def _win_decode_kernel(q_ref, c_ref, n_ref, o_ref, *, past, t4):
    q = q_ref[0]
    rows = q.shape[0]
    wb = c_ref.shape[4]
    m = jnp.full((rows, 1), NEG, F32)
    l = jnp.zeros((rows, 1), F32)
    acc = jnp.zeros((rows, LANES), F32)
    for ref, kstart in ((c_ref, past - wb), (n_ref, past)):
        kt, vt = _kv_t([ref])
        nk = kt.shape[1]
        d = past + _row_token(rows, nk, t4) - (kstart + lax.broadcasted_iota(I32, (rows, nk), 1))
        mask = (d >= 0) & (d < WINDOW)
        m, l, acc = _softmax_block(_dot(q, kt), mask, vt, m, l, acc, v_is_transposed=True)
    o_ref[0] = acc / l


def _win_decode(qnb, cwin_t, new_t, *, past, t4):
    nb, rows, _ = qnb.shape
    return pl.pallas_call(
        functools.partial(_win_decode_kernel, past=past, t4=t4),
        out_shape=jax.ShapeDtypeStruct((nb, rows, LANES), F32),
        grid=(nb,),
        in_specs=[pl.BlockSpec((1, rows, LANES), lambda b: (b, 0, 0)),
                  pl.BlockSpec((1,) + cwin_t.shape[1:], lambda b: (b, 0, 0, 0, 0)),
                  pl.BlockSpec((1,) + new_t.shape[1:], lambda b: (b, 0, 0, 0, 0))],
        out_specs=pl.BlockSpec((1, rows, LANES), lambda b: (b, 0, 0)),
        compiler_params=_cparams(("parallel",)),
        name="nsa_win_decode",
    )(qnb, cwin_t, new_t)


def _merge_kernel(oda_ref, oc_ref, os_ref, ow_ref, gn_ref, ga_ref, gb_ref, e_ref, wda_ref, wns_ref, o_ref):
    g = gn_ref[...]
    ghi = g.astype(BF16)
    glo = (g - ghi.astype(F32)).astype(BF16)
    ons = None
    for b, src in enumerate((oc_ref, os_ref, ow_ref)):
        ge = _dot(ghi, e_ref[b]) + _dot(glo, e_ref[b])
        term = ge * src[...]
        ons = term if ons is None else ons + term
    yda = _dot(oda_ref[...], wda_ref[...])
    yns = _dot(ons.astype(BF16), wns_ref[...])
    o_ref[...] = (ga_ref[...] * yda + gb_ref[...] * yns).astype(o_ref.dtype)


def _merge(oda, ocmp, oslc, owin, gns, gmg, emat, wda, wns):
    n, hd = oda.shape
    d = wda.shape[1]
    tm = _pick_tile(n, 320, 16)
    row = lambda i: (i, 0)
    const2 = lambda i: (0, 0)
    return pl.pallas_call(
        _merge_kernel,
        out_shape=jax.ShapeDtypeStruct((n, d), BF16),
        grid=(n // tm,),
        in_specs=[pl.BlockSpec((tm, hd), row), pl.BlockSpec((tm, hd), row), pl.BlockSpec((tm, hd), row),
                  pl.BlockSpec((tm, hd), row), pl.BlockSpec((tm, LANES), row),
                  pl.BlockSpec((tm, d), lambda i: (i, 0)), pl.BlockSpec((tm, d), lambda i: (i, 1)),
                  pl.BlockSpec((3, LANES, hd), lambda i: (0, 0, 0)),
                  pl.BlockSpec((hd, d), const2), pl.BlockSpec((hd, d), const2)],
        out_specs=pl.BlockSpec((tm, d), row),
        compiler_params=_cparams(("parallel",)),
        name="mixer_merge",
    )(oda, ocmp, oslc, owin, gns, gmg, gmg, emat, wda, wns)


def _lane_pick(x, lane, k):
    return jnp.sum(jnp.where(lane == k, x, 0.0), axis=-1, keepdims=True)


def _route(logits):
    lane = lax.broadcasted_iota(I32, logits.shape, 1)
    isg = lane < N_GROUPS
    gmax = jnp.max(jnp.where(isg, logits, LOWEST), axis=-1, keepdims=True)
    gsel = jnp.min(jnp.where(isg & (logits == gmax), lane, LANES), axis=-1, keepdims=True)
    pgrp = 1.0 / jnp.sum(jnp.where(isg, jnp.exp(logits - gmax), 0.0), axis=-1, keepdims=True)
    lo = N_GROUPS + gsel * EXP_PER_GROUP
    ing = (lane >= lo) & (lane < lo + EXP_PER_GROUP)
    v1 = jnp.max(jnp.where(ing, logits, LOWEST), axis=-1, keepdims=True)
    i1 = jnp.min(jnp.where(ing & (logits == v1), lane, LANES), axis=-1, keepdims=True)
    ing2 = ing & (lane != i1)
    v2 = jnp.max(jnp.where(ing2, logits, LOWEST), axis=-1, keepdims=True)
    i2 = jnp.min(jnp.where(ing2 & (logits == v2), lane, LANES), axis=-1, keepdims=True)
    t = jnp.exp(v2 - v1)
    w1 = pgrp / (1.0 + t)
    w2 = pgrp * t / (1.0 + t)
    e1 = (i1 - N_GROUPS).astype(F32)
    e2 = (i2 - N_GROUPS).astype(F32)
    return jnp.where(lane == 0, e1, jnp.where(lane == 1, e2, jnp.where(lane == 2, w1, jnp.where(lane == 3, w2, 0.0))))


def _store_rows(ref, y, rows, nchunk):
    for s in range(nchunk):
        ref[pl.ds(s, rows, stride=nchunk), :] = y[:, s * LANES:(s + 1) * LANES]


def _wo_route_kernel(m_ref, x_ref, wo_ref, gf_ref, wr_ref, br_ref, x1_ref, h_ref, rt_ref, *, nchunk):
    x1 = x_ref[...] + _dot(m_ref[...], wo_ref[...])
    x1_ref[...] = x1
    h = _rms(x1, gf_ref[...])
    _store_rows(h_ref, h, h.shape[0], nchunk)
    hi = h.astype(BF16)
    lo = (h - hi.astype(F32)).astype(BF16)
    w = wr_ref[...]
    w_hi = w.astype(BF16)
    w_lo = (w - w_hi.astype(F32)).astype(BF16)
    logits = _dot(hi, w_hi) + (_dot(hi, w_lo) + _dot(lo, w_hi))
    rt_ref[...] = _route(logits + br_ref[...])


def _wo_route(mrg, xa, wo, gffn, wr, br):
    n, d = xa.shape
    tm = _pick_tile(n, 320, 16)
    nchunk = d // LANES
    row = lambda i: (i, 0)
    const2 = lambda i: (0, 0)
    return pl.pallas_call(
        functools.partial(_wo_route_kernel, nchunk=nchunk),
        out_shape=(jax.ShapeDtypeStruct((n, d), F32), jax.ShapeDtypeStruct((n * nchunk, LANES), F32),
                   jax.ShapeDtypeStruct((n, LANES), F32)),
        grid=(n // tm,),
        in_specs=[pl.BlockSpec((tm, d), row), pl.BlockSpec((tm, d), row), pl.BlockSpec((d, d), const2),
                  pl.BlockSpec((1, d), const2), pl.BlockSpec((d, LANES), const2),
                  pl.BlockSpec((1, LANES), const2)],
        out_specs=(pl.BlockSpec((tm, d), row), pl.BlockSpec((tm * nchunk, LANES), row),
                   pl.BlockSpec((tm, LANES), row)),
        compiler_params=_cparams(("parallel",)),
        name="wo_route",
    )(mrg, xa, wo, gffn, wr, br)


def _onehots(rt):
    lane = lax.broadcasted_iota(I32, rt.shape, 1)
    lane_f = lane.astype(F32)
    oh0 = jnp.where(lane_f == _lane_pick(rt, lane, 0), 1.0, 0.0)
    oh1 = jnp.where(lane_f == _lane_pick(rt, lane, 1), 1.0, 0.0)
    return lane, oh0, oh1


def _rank_kernel(rt_ref, tri_ref, rk_ref, sz_ref, carry):
    @pl.when(pl.program_id(0) == 0)
    def _():
        carry[...] = jnp.zeros(carry.shape, F32)

    lane, oh0, oh1 = _onehots(rt_ref[...])
    oh = oh0 + oh1
    before = _dot(tri_ref[...], oh.astype(BF16)) + carry[0:1, :]
    r0 = jnp.sum(before * oh0, axis=-1, keepdims=True)
    r1 = jnp.sum(before * oh1, axis=-1, keepdims=True)
    rk_ref[...] = jnp.where(lane == 0, r0, jnp.where(lane == 1, r1, 0.0))
    carry[...] = carry[...] + jnp.sum(oh, axis=0, keepdims=True)
    sz_ref[...] = carry[...]


def _expert_ranks(route, tri):
    n = route.shape[0]
    tm = tri.shape[0]
    return pl.pallas_call(
        _rank_kernel,
        out_shape=(jax.ShapeDtypeStruct((n, LANES), F32), jax.ShapeDtypeStruct((8, LANES), F32)),
        grid=(n // tm,),
        in_specs=[pl.BlockSpec((tm, LANES), lambda i: (i, 0)), pl.BlockSpec((tm, tm), lambda i: (0, 0))],
        out_specs=(pl.BlockSpec((tm, LANES), lambda i: (i, 0)), pl.BlockSpec((8, LANES), lambda i: (0, 0))),
        scratch_shapes=[pltpu.VMEM((8, LANES), F32)],
        compiler_params=_cparams(("arbitrary",)),
        name="moe_rank",
    )(route, tri)


def _dest_kernel(rt_ref, rk_ref, sz_ref, dest_ref, be_ref, nu_ref, *, shift):
    lane8 = lax.broadcasted_iota(I32, (8, LANES), 1)
    sizes = sz_ref[...].astype(I32)
    blk = 1 << shift
    padded = jnp.where(lane8 < N_EXPERTS, lax.shift_right_logical(sizes + (blk - 1), shift) * blk, 0)
    ends = padded
    d = 1
    while d < LANES:
        ends = ends + jnp.where(lane8 >= d, pltpu.roll(ends, d, 1), 0)
        d *= 2
    pstart = (ends - padded)[0:1, :].astype(F32)
    rt = rt_ref[...]
    lane, oh0, oh1 = _onehots(rt)
    rk = rk_ref[...]
    d0 = jnp.sum(oh0 * pstart, axis=-1, keepdims=True) + _lane_pick(rk, lane, 0)
    d1 = jnp.sum(oh1 * pstart, axis=-1, keepdims=True) + _lane_pick(rk, lane, 1)
    dest_ref[...] = jnp.where(lane == 0, d0, jnp.where(lane == 1, d1, 0.0)).astype(I32)
    nbt = be_ref.shape[0]
    lane_b = lax.broadcasted_iota(I32, (nbt, LANES), 1)
    first_row = lax.broadcasted_iota(I32, (nbt, LANES), 0) * blk
    cnt = jnp.sum(jnp.where((lane_b < N_EXPERTS) & (ends[0:1, :] <= first_row), 1, 0), axis=-1, keepdims=True)
    be_ref[...] = jnp.broadcast_to(jnp.minimum(cnt, N_EXPERTS - 1), (nbt, LANES))
    total = jnp.sum(jnp.where(lane8 == N_EXPERTS - 1, ends, 0), axis=-1, keepdims=True)
    nu_ref[...] = jnp.broadcast_to(lax.shift_right_logical(total, shift), (8, LANES))


def _destinations(route, rank, sizes, nbt, shift):
    n = route.shape[0]
    tm = _pick_tile(n, 640, 8)
    nbt_p = _round_up(nbt, 8)
    return pl.pallas_call(
        functools.partial(_dest_kernel, shift=shift),
        out_shape=(jax.ShapeDtypeStruct((n, LANES), I32), jax.ShapeDtypeStruct((nbt_p, LANES), I32),
                   jax.ShapeDtypeStruct((8, LANES), I32)),
        grid=(n // tm,),
        in_specs=[pl.BlockSpec((tm, LANES), lambda i: (i, 0)), pl.BlockSpec((tm, LANES), lambda i: (i, 0)),
                  pl.BlockSpec((8, LANES), lambda i: (0, 0))],
        out_specs=(pl.BlockSpec((tm, LANES), lambda i: (i, 0)), pl.BlockSpec((nbt_p, LANES), lambda i: (0, 0)),
                   pl.BlockSpec((8, LANES), lambda i: (0, 0))),
        compiler_params=_cparams(("arbitrary",)),
        name="moe_dest",
    )(route, rank, sizes)


def _expert_kernel(dest_ref, be_ref, nu_ref, h_ref, wg_ref, wu_ref, wd_ref, o_ref,
                   tok, ordinal, xbuf, sem, wgf, wuf, wdf, wsem, wgb, wub, wdb, *, rows, nchunk, n_assign):
    j = pl.program_id(0)
    n_used = nu_ref[0]

    def weight_copies(e, ws):
        return (pltpu.make_async_copy(wg_ref.at[e], wgf.at[ws], wsem.at[ws, 0]),
                pltpu.make_async_copy(wu_ref.at[e], wuf.at[ws], wsem.at[ws, 1]),
                pltpu.make_async_copy(wd_ref.at[e], wdf.at[ws], wsem.at[ws, 2]))

    def row_copy(tile, slot, r):
        return pltpu.make_async_copy(h_ref.at[tok[tile * rows + r]],
                                     xbuf.at[slot, pl.ds(r * nchunk, nchunk)], sem.at[slot])

    def start_tile(tile, slot):
        def body(r, carry):
            row_copy(tile, slot, 2 * r).start(priority=0)
            row_copy(tile, slot, 2 * r + 1).start(priority=1)
            return carry
        lax.fori_loop(0, rows // 2, body, 0, unroll=4)

    def wait_tile(tile, slot):
        def body(r, carry):
            row_copy(tile, slot, r).wait()
            return carry
        lax.fori_loop(0, rows, body, 0, unroll=8)

    @pl.when(j == 0)
    def _():
        for c in weight_copies(be_ref[0], 0):
            c.start()
        ordinal[0] = 0

        def clear(r, carry):
            tok[r] = 0
            return carry
        lax.fori_loop(0, tok.shape[0], clear, 0, unroll=8)

        def place(a, carry):
            tok[dest_ref[a]] = lax.shift_right_logical(a, 1)
            return carry
        lax.fori_loop(0, n_assign, place, 0, unroll=8)
        start_tile(0, 0)

    slot = lax.rem(j, 2)
    active = j < n_used
    expert = be_ref[j]
    changed = (j == 0) | (expert != be_ref[jnp.maximum(j - 1, 0)])

    @pl.when(active & changed)
    def _():
        ws = lax.rem(ordinal[0], 2)
        for c in weight_copies(expert, ws):
            c.wait()
        wgb[...] = wgf[ws].astype(BF16)
        wub[...] = wuf[ws].astype(BF16)
        wdb[...] = wdf[ws].astype(BF16)
        nxt = lax.while_loop(lambda t: (t < n_used) & (be_ref[jnp.minimum(t, n_used - 1)] == expert),
                             lambda t: t + 1, j + 1)

        @pl.when(nxt < n_used)
        def _():
            for c in weight_copies(be_ref[nxt], 1 - ws):
                c.start()

        ordinal[0] = ordinal[0] + 1

    @pl.when(active)
    def _():
        @pl.when(j + 1 < n_used)
        def _():
            start_tile(j + 1, 1 - slot)

        wait_tile(j, slot)
        x = jnp.concatenate([xbuf[slot, pl.ds(s, rows, stride=nchunk), :] for s in range(nchunk)],
                            axis=1).astype(BF16)
        a = _dot(x, wgb[...])
        u = _dot(x, wub[...])
        y = _dot((jax.nn.silu(a) * u).astype(BF16), wdb[...])
        _store_rows(o_ref, y, rows, nchunk)

    @pl.when(jnp.logical_not(active))
    def _():
        o_ref[...] = jnp.zeros(o_ref.shape, F32)


def _experts(dest, blk_e, n_used, h3, wg, wu, wd, rows):
    d, de = wg.shape[1], wg.shape[2]
    nchunk = d // LANES
    nbt = blk_e.shape[0]
    grid_spec = pltpu.PrefetchScalarGridSpec(
        num_scalar_prefetch=3, grid=(nbt,),
        in_specs=[pl.BlockSpec(memory_space=pl.ANY)] * 4,
        out_specs=pl.BlockSpec((rows * nchunk, LANES), lambda j, dst, be, nu: (j, 0)),
        scratch_shapes=[pltpu.SMEM((nbt * rows,), I32), pltpu.SMEM((1,), I32),
                        pltpu.VMEM((2, rows * nchunk, LANES), F32), pltpu.SemaphoreType.DMA((2,)),
                        pltpu.VMEM((2, d, de), F32), pltpu.VMEM((2, d, de), F32), pltpu.VMEM((2, de, d), F32),
                        pltpu.SemaphoreType.DMA((2, 3)),
                        pltpu.VMEM((d, de), BF16), pltpu.VMEM((d, de), BF16), pltpu.VMEM((de, d), BF16)])
    return pl.pallas_call(
        functools.partial(_expert_kernel, rows=rows, nchunk=nchunk, n_assign=dest.shape[0]),
        out_shape=jax.ShapeDtypeStruct((nbt * rows * nchunk, LANES), F32),
        grid_spec=grid_spec,
        compiler_params=_cparams(("arbitrary",)),
        name="moe_experts",
    )(dest, blk_e, n_used, h3, wg, wu, wd)


def _combine_kernel(dest_ref, ys_ref, rt_ref, x1_ref, p_ref, gp_ref, wpg_ref, wpl_ref, gfin_ref, o_ref,
                    gbuf, sem, *, tm, nchunk, final):
    i = pl.program_id(0)
    slot = lax.rem(i, 2)

    def copy(tile, sl, j):
        return pltpu.make_async_copy(ys_ref.at[dest_ref[tile * (2 * tm) + j]],
                                     gbuf.at[sl, pl.ds(j * nchunk, nchunk)], sem.at[sl])

    def start_tile(tile, sl):
        def body(j, carry):
            copy(tile, sl, 2 * j).start(priority=0)
            copy(tile, sl, 2 * j + 1).start(priority=1)
            return carry
        lax.fori_loop(0, tm, body, 0, unroll=4)

    @pl.when(i == 0)
    def _():
        start_tile(0, 0)

    @pl.when(i + 1 < pl.num_programs(0))
    def _():
        start_tile(i + 1, 1 - slot)

    def drain(j, carry):
        copy(i, slot, j).wait()
        return carry

    lax.fori_loop(0, 2 * tm, drain, 0, unroll=8)
    rt = rt_ref[...]
    lane = lax.broadcasted_iota(I32, rt.shape, 1)
    y0 = jnp.concatenate([gbuf[slot, pl.ds(s, tm, stride=2 * nchunk), :] for s in range(nchunk)], axis=1)
    y1 = jnp.concatenate([gbuf[slot, pl.ds(nchunk + s, tm, stride=2 * nchunk), :] for s in range(nchunk)], axis=1)
    x2 = x1_ref[...] + (_lane_pick(rt, lane, 2) * y0 + _lane_pick(rt, lane, 3) * y1)
    gate = jax.nn.sigmoid(_dot(_rms(x2, gp_ref[...]).astype(BF16), wpg_ref[...]))
    x3 = x2 + gate * _dot(p_ref[...].astype(BF16), wpl_ref[...])
    o_ref[...] = _rms(x3, gfin_ref[...]) if final else x3


def _combine(dest, ys3, route, x1, pa, gple, wpg, wpl, gfin, final):
    n, d = x1.shape
    nchunk = d // LANES
    tm = _pick_tile(n, 320, 16)
    dp = pa.shape[1]
    row = lambda i, dst: (i, 0)
    const2 = lambda i, dst: (0, 0)
    grid_spec = pltpu.PrefetchScalarGridSpec(
        num_scalar_prefetch=1, grid=(n // tm,),
        in_specs=[pl.BlockSpec(memory_space=pl.ANY),
                  pl.BlockSpec((tm, LANES), row), pl.BlockSpec((tm, d), row), pl.BlockSpec((tm, dp), row),
                  pl.BlockSpec((1, d), const2), pl.BlockSpec((d, d), const2), pl.BlockSpec((dp, d), const2),
                  pl.BlockSpec((1, d), const2)],
        out_specs=pl.BlockSpec((tm, d), row),
        scratch_shapes=[pltpu.VMEM((2, 2 * tm * nchunk, LANES), F32), pltpu.SemaphoreType.DMA((2,))])
    return pl.pallas_call(
        functools.partial(_combine_kernel, tm=tm, nchunk=nchunk, final=final),
        out_shape=jax.ShapeDtypeStruct((n, d), F32),
        grid_spec=grid_spec,
        compiler_params=_cparams(("arbitrary",)),
        name="moe_combine_tail",
    )(dest, ys3, route, x1, pa, gple, wpg, wpl, gfin)


def _rope_tables(pos):
    inv = ROPE_THETA ** (-jnp.arange(ROPE_HALF, dtype=F32) / ROPE_HALF)
    ang = pos.astype(F32)[:, None] * inv[None, :]
    cos, sin = jnp.cos(ang), jnp.sin(ang)
    n = pos.shape[0]
    rest = DA_DH - 2 * ROPE_HALF
    z = jnp.zeros((n, ROPE_HALF), F32)
    zr = jnp.zeros((n, rest), F32)
    c64 = jnp.concatenate([cos, cos, jnp.ones((n, rest), F32)], axis=1)
    a64 = jnp.concatenate([-sin, z, zr], axis=1)
    b64 = jnp.concatenate([z, sin, zr], axis=1)
    rep = LANES // DA_DH
    return jnp.tile(c64, (1, rep)), jnp.tile(a64, (1, rep)), jnp.tile(b64, (1, rep))


def _column_tables():
    ones = lambda n: jnp.ones((n,), F32)
    zeros = lambda n: jnp.zeros((n,), F32)
    gw = NSA_KV * NSA_DH
    rmask = jnp.concatenate([ones(C_DA_Q), ones(C_DA_K), zeros(C_DA_V), ones(C_NSA_Q)]
                            + [ones(gw), zeros(gw)] * 3)
    cscale = jnp.concatenate([ones(C_DA_Q) * DA_DH ** -0.5, ones(C_DA_K), ones(C_DA_V),
                              ones(C_NSA_Q) * NSA_DH ** -0.5, ones(C_NSA_KV)])
    return rmask[None, :], cscale[None, :]


def _block_expand(width, nkeys):
    blk = jnp.arange(width, dtype=I32)[:, None]
    key = jnp.arange(nkeys, dtype=I32)[None, :]
    return (key // BLOCK == blk).astype(BF16)


def _pad_rows(x, axis, size):
    pad = [(0, 0)] * x.ndim
    pad[axis] = (0, size - x.shape[axis])
    return jnp.pad(x, pad)


def _cmp_layout(kc, b, nc, width):
    x = kc.reshape(b, nc, NSA_KV, NSA_DH).transpose(0, 2, 1, 3)
    return _pad_rows(x, 2, width).astype(BF16)


def kernel(x_prompt, x_sample, cache_da_kv, cache_nsa_kv, cache_win_kv, page_table, p_prompt, p_sample,
           g_mix, w_in, da_lambda, da_subln, nsa_cpos, nsa_cmp, w_da_up, w_nsa_up, w_o, g_ffn,
           w_rg, b_rg, w_re, b_re, w_e_gate, w_e_up, w_e_down, g_ple, w_ple_gate, w_ple, g_final):
    b, s, d = x_prompt.shape
    nb, t4, _ = x_sample.shape
    depth = g_mix.shape[0]
    npr, nsm = b * s, nb * t4
    n = npr + nsm
    page = cache_da_kv.shape[2]
    npages = page_table.shape[1]
    past = npages * page
    wb = cache_win_kv.shape[2]
    nchunk = d // LANES
    rq = t4 * DA_REP
    assert rq % 8 == 0 and page % BLOCK == 0 and s % BLOCK == 0 and t4 <= page and d % LANES == 0

    xa = jnp.concatenate([x_prompt.reshape(npr, d), x_sample.reshape(nsm, d)], axis=0)
    pos = jnp.concatenate([jnp.tile(jnp.arange(s, dtype=I32), b), jnp.tile(past + jnp.arange(t4, dtype=I32), nb)])
    cos, sa, sb = _rope_tables(pos)
    rmask, cscale = _column_tables()

    nc_p, ns_p = s // BLOCK, max(-(-s // BLOCK), N_SELECT)
    tot_s = past + t4
    nc_s, ns_s = tot_s // BLOCK, max(-(-tot_s // BLOCK), N_SELECT)
    wid_p, wid_s = _round_up(ns_p, LANES), _round_up(ns_s, LANES)
    emat_p = _block_expand(wid_p, s)
    pg_s, nsteps_s = _page_steps(npages)
    emat_s = _block_expand(wid_s, (nsteps_s + 1) * pg_s * page)

    col = jnp.arange(LANES, dtype=I32)[:, None]
    lane = jnp.arange(NSA_HEADS * NSA_DH, dtype=I32)[None, :]
    gate_expand = jnp.stack([(col == br * NSA_HEADS + lane // NSA_DH) for br in range(3)]).astype(BF16)

    n_assign = 2 * n
    shift = int(math.log2(EXPERT_ROWS))
    nbt = -(-n_assign // EXPERT_ROWS) + N_EXPERTS
    tm_rank = _pick_tile(n, 640, 16)
    tri = jnp.tril(jnp.ones((tm_rank, tm_rank), BF16), -1)

    outs = {k: [] for k in ("da_p", "da_s", "ns_p", "ns_s", "win_p", "win_s")}
    for i in range(depth):
        lam_init = 0.8 - 0.6 * math.exp(-0.3 * i)
        wi = w_in[i]
        g1 = g_mix[i][None, :]
        zf, zb = _proj_rope(xa, g1, wi[:, :S5].astype(BF16), cos, sa, sb, rmask, cscale)
        wg_cols = wi[:, S5:S6].reshape(d, NSA_HEADS, 3).transpose(0, 2, 1).reshape(d, C_NSA_G)
        gns = _gate_proj(xa, g1, _pad_rows(wg_cols, 1, LANES).astype(BF16))
        gmg = _gate_proj(xa, g1, wi[:, S6:].astype(BF16))
        lam_p = da_lambda[i].astype(F32)
        subln = da_subln[i][None, :].astype(F32)
        wp = jnp.tile(nsa_cpos[i].astype(F32), (1, 1, NSA_KV))
        wcm = nsa_cmp[i].astype(F32)
        zc = jnp.zeros_like(wcm)
        wc = jnp.concatenate([jnp.concatenate([wcm, zc], axis=2), jnp.concatenate([zc, wcm], axis=2)], axis=1)

        zs = zb[npr:].reshape(nb, t4, S5)
        kd0, kd1 = S1 - PROJ_F32_FROM, S3 - PROJ_F32_FROM
        kn0, kn1 = S4 - PROJ_F32_FROM, S5 - PROJ_F32_FROM
        zfs = zf[npr:].reshape(nb, t4, kn1)
        q6 = zs[..., :S1].reshape(nb, t4, DA_KV, DA_REP, 2, DA_DH).transpose(0, 2, 4, 1, 3, 5)
        eye_g = jnp.eye(DA_KV, dtype=BF16)
        eye_c = jnp.eye(2, dtype=BF16)
        qbd = (q6[:, :, :, :, :, None, None, :] * eye_g[None, :, None, None, None, :, None, None]
               * eye_c[None, None, :, None, None, None, :, None]).reshape(nb, DA_KV * 2 * rq, DA_KV * 2 * DA_DH)
        cda = cache_da_kv[i].reshape(cache_da_kv.shape[1], page * 2 * DA_KV, DA_DV)
        new_da = _pad_rows(zfs[..., kd0:kd1], 1, page).reshape(nb, page * 2 * DA_KV, DA_DV)
        oda_s = _da_decode(page_table, qbd, cda, new_da, lam_p, subln, lam_init)
        oda_s = oda_s.reshape(nb, DA_KV, t4, DA_REP, DA_DV).transpose(0, 2, 1, 3, 4).reshape(nsm, DA_HEADS * DA_DV)

        qns = zs[..., S3:S4].reshape(nb, t4, NSA_KV, NSA_REP, NSA_DH).transpose(0, 2, 1, 3, 4)
        eye_n = jnp.eye(NSA_KV, dtype=BF16)
        qnb = (qns[:, :, :, :, None, :] * eye_n[None, :, None, None, :, None]
               ).reshape(nb, NSA_KV * t4 * NSA_REP, LANES)
        cns_t = cache_nsa_kv[i].transpose(0, 2, 3, 4, 1)
        cwin_t = cache_win_kv[i].transpose(0, 2, 3, 4, 1)
        kw = NSA_KV * NSA_DH
        wt = jnp.tile(nsa_cpos[i].astype(F32).transpose(0, 2, 1)[:, None],
                      (1, NSA_KV, 1, page // BLOCK)).reshape(2 * kw, page)
        ocmp_s, sel_s = _cmp_decode(page_table, qns.reshape(nb, NSA_KV, t4 * NSA_REP, NSA_DH), cns_t, wt,
                                    nsa_cmp[i].astype(F32).transpose(0, 2, 1),
                                    past=past, t4=t4, nc=nc_s, ns=ns_s, width=wid_s)
        ocmp_s = ocmp_s.reshape(nb, NSA_KV, t4, NSA_REP, NSA_DH).transpose(0, 2, 1, 3, 4
                                                                             ).reshape(nsm, NSA_HEADS * NSA_DH)
        selrows = jnp.broadcast_to(sel_s[:, :, :, None, :], (nb, NSA_KV, t4, NSA_REP, wid_s)
                                   ).reshape(nb, NSA_KV * t4 * NSA_REP, wid_s)
        new_t = lambda x: _pad_rows(x.reshape(nb, t4, 2, NSA_KV, NSA_DH), 1, page).transpose(0, 2, 3, 4, 1)
        oslc_s = _slc_decode(page_table, qnb, selrows, emat_s, cns_t, new_t(zfs[..., kn0 + 2 * kw:kn0 + 4 * kw]),
                             past=past, t4=t4)
        owin_s = _win_decode(qnb, cwin_t, new_t(zfs[..., kn0 + 4 * kw:]), past=past, t4=t4)

        def halves(o):
            o = o.reshape(nb, NSA_KV, t4, NSA_REP, NSA_KV, NSA_DH)
            o = jnp.stack([o[:, g, :, :, g] for g in range(NSA_KV)], axis=2)
            return o.reshape(nsm, NSA_HEADS * NSA_DH)

        oslc_s, owin_s = halves(oslc_s), halves(owin_s)

        seeded = lambda rows: jnp.concatenate([jnp.zeros((npr, rows.shape[1]), rows.dtype), rows], axis=0)
        oda = _da_flash(zb, seeded(oda_s.astype(BF16)), b, s, lam_p, subln, lam_init)
        kc, vc = _compress_prompt(zf, wp, wc, b, s)
        kc2 = _cmp_layout(kc, b, nc_p, wid_p)
        kc2 = jnp.concatenate([kc2, kc2], axis=-1)
        vcl = _cmp_layout(vc, b, nc_p, wid_p)
        vz = jnp.zeros_like(vcl)
        vc2 = jnp.stack([jnp.concatenate([vcl, vz], axis=-1), jnp.concatenate([vz, vcl], axis=-1)], axis=2)
        ocmp, sel_p = _cmp_select(zb, kc2, vc2, seeded(ocmp_s), b, s, nc=nc_p, ns=ns_p)
        kvp = zb[:npr, S4:S5].reshape(npr, 6, NSA_KV, NSA_DH)
        dup = lambda k: jnp.concatenate([k, k], axis=-1).reshape(npr, NSA_KV * LANES)
        with_ones = lambda v: jnp.concatenate([v, jnp.ones_like(v)], axis=-1).reshape(npr, NSA_KV * LANES)
        oslc = _nsa_flash(zb, dup(kvp[:, 2]), with_ones(kvp[:, 3]), seeded(oslc_s), b, s, sel_p, emat_p)
        owin = _nsa_flash(zb, dup(kvp[:, 4]), with_ones(kvp[:, 5]), seeded(owin_s), b, s)

        mrg =_merge(oda, ocmp, oslc, owin, gns, gmg, gate_expand,
                     w_da_up[i].astype(BF16), w_nsa_up[i].astype(BF16))
        wr = _pad_rows(jnp.concatenate([w_rg[i], w_re[i]], axis=1).astype(F32), 1, LANES)
        br = _pad_rows(jnp.concatenate([b_rg[i], b_re[i]]).astype(F32)[None, :], 1, LANES)
        x1, hrows, route = _wo_route(mrg, xa, w_o[i].astype(BF16), g_ffn[i][None, :], wr, br)
        rank, sizes = _expert_ranks(route, tri)
        dest, be, nu = _destinations(route, rank, sizes, nbt, shift)
        dest_flat = dest[:, :2].reshape(n_assign)
        ys = _experts(dest_flat, be[:nbt, 0], nu[0, :1], hrows.reshape(n, nchunk, LANES),
                      w_e_gate[i], w_e_up[i], w_e_down[i], EXPERT_ROWS)
        pa = jnp.concatenate([p_prompt[i].reshape(npr, -1), p_sample[i].reshape(nsm, -1)], axis=0)
        xa = _combine(dest_flat, ys.reshape(nbt * EXPERT_ROWS, nchunk, LANES), route, x1, pa,
                      g_ple[i][None, :], w_ple_gate[i].astype(BF16), w_ple[i].astype(BF16),
                      g_final[None, :], i == depth - 1)

        zfp = zf[:npr].reshape(b, s, kn1)
        outs["da_p"].append(zfp[..., kd0:kd1].reshape(b, s, 2, DA_KV, DA_DV))
        outs["da_s"].append(zfs[..., kd0:kd1].reshape(nb, t4, 2, DA_KV, DA_DV))
        outs["ns_p"].append(zfp[..., kn0:kn0 + 4 * kw].reshape(b, s, 4, NSA_KV, NSA_DH))
        outs["ns_s"].append(zfs[..., kn0:kn0 + 4 * kw].reshape(nb, t4, 4, NSA_KV, NSA_DH))
        win_p = zfp[..., kn0 + 4 * kw:].reshape(b, s, 2, NSA_KV, NSA_DH)
        outs["win_p"].append(win_p[:, s - min(WINDOW, s):])
        win_s = jnp.concatenate([cache_win_kv[i], zfs[..., kn0 + 4 * kw:].reshape(nb, t4, 2, NSA_KV, NSA_DH)], axis=1)
        outs["win_s"].append(win_s[:, win_s.shape[1] - wb:])

    y_prompt = xa[:npr].reshape(b, s, d)
    y_sample = xa[npr:].reshape(nb, t4, d)
    return (y_prompt, y_sample, jnp.stack(outs["da_p"]), jnp.stack(outs["da_s"]), jnp.stack(outs["ns_p"]),
            jnp.stack(outs["ns_s"]), jnp.stack(outs["win_p"]), jnp.stack(outs["win_s"]))
```

```python
import functools
import math

import jax
import jax.numpy as jnp
from jax import lax
from jax.experimental import pallas as pl
from jax.experimental.pallas import tpu as pltpu

F32 = jnp.float32
BF16 = jnp.bfloat16
I32 = jnp.int32
HIGHEST = lax.Precision.HIGHEST

DA_HEADS, DA_KV, DA_REP, DA_DH = 8, 4, 2, 64
DA_DV = 2 * DA_DH
NSA_HEADS, NSA_KV, NSA_REP, NSA_DH = 16, 2, 8, 64
BLOCK = 64
BLOCK_SHIFT = 6
N_SELECT = 16
WINDOW = 512
FORCE_BONUS = 1000.0
ROPE_THETA = 500000.0
ROPE_HALF = 8
N_GROUPS, EXP_PER_GROUP = 8, 8
N_EXPERTS = N_GROUPS * EXP_PER_GROUP
EPS = 1e-6
NEG = -1e30
LOWEST = -3e38

LANES = 128
VMEM_LIMIT_BYTES = 56 * 1024 * 1024

C_DA_Q = DA_HEADS * 2 * DA_DH
C_DA_K = DA_KV * 2 * DA_DH
C_DA_V = DA_KV * DA_DV
C_NSA_Q = NSA_HEADS * NSA_DH
C_NSA_KV = 6 * NSA_KV * NSA_DH
C_NSA_G = 3 * NSA_HEADS
S1 = C_DA_Q
S2 = S1 + C_DA_K
S3 = S2 + C_DA_V
S4 = S3 + C_NSA_Q
S5 = S4 + C_NSA_KV
S6 = S5 + C_NSA_G

WEIGHT_DMA_SPLIT = 4
EXPERT_ROWS = 128
DECODE_PAGES = 16
PROJ_TILE = 768
PROJ_F32_FROM = 768
FLASH_TILE = 512
FLASH_ROWS = 128
FLASH_KEYS = 256


def _cparams(sem):
    return pltpu.CompilerParams(dimension_semantics=sem, vmem_limit_bytes=VMEM_LIMIT_BYTES)


def _pick_tile(n, target, mult):
    best = None
    for t in range(mult, min(n, target) + 1, mult):
        if n % t == 0:
            best = t
    return n if best is None else best


def _round_up(x, m):
    return (x + m - 1) // m * m


def _nt(a, b):
    return lax.dot_general(a, b, (((1,), (1,)), ((), ())), preferred_element_type=F32)


def _dot(a, b, **kw):
    return jnp.dot(a, b, preferred_element_type=F32, **kw)


def _rms(x, g):
    ms = jnp.mean(x * x, axis=-1, keepdims=True)
    return x * lax.rsqrt(ms + EPS) * g


def _proj_rope_kernel(x_ref, g_ref, w_ref, cos_ref, sa_ref, sb_ref, rm_ref, cs_ref,
                      of_ref, ob_ref, h_scr, *, n_chunks, skip):
    @pl.when(pl.program_id(1) == 0)
    def _():
        h_scr[...] = _rms(x_ref[...], g_ref[...]).astype(BF16)

    j = pl.program_id(1)
    z = _dot(h_scr[...], w_ref[...])
    c, sa, sb = cos_ref[...], sa_ref[...], sb_ref[...]
    outs = []
    for k in range(n_chunks):
        sl = slice(k * LANES, (k + 1) * LANES)
        zc = z[:, sl]
        rot = zc * c + pltpu.roll(zc, LANES - ROPE_HALF, 1) * sa + pltpu.roll(zc, ROPE_HALF, 1) * sb
        out = jnp.where(rm_ref[:, sl] > 0.5, rot, zc) * cs_ref[:, sl]
        ob_ref[:, sl] = out.astype(BF16)
        outs.append(out)

    @pl.when(j >= skip)
    def _():
        for k, out in enumerate(outs):
            of_ref[:, k * LANES:(k + 1) * LANES] = out


def _proj_rope(xa, g, w, cos, sa, sb, rmask, cscale):
    n, d = xa.shape
    ca = w.shape[1]
    tm = _pick_tile(n, 640, 16)
    tn = PROJ_TILE
    skip = PROJ_F32_FROM // tn
    row = lambda i, j: (i, 0)
    col = lambda i, j: (0, j)
    fcol = lambda i, j: (i, jnp.maximum(j - skip, 0))
    return pl.pallas_call(
        functools.partial(_proj_rope_kernel, n_chunks=tn // LANES, skip=skip),
        out_shape=(jax.ShapeDtypeStruct((n, ca - PROJ_F32_FROM), F32), jax.ShapeDtypeStruct((n, ca), BF16)),
        grid=(n // tm, ca // tn),
        in_specs=[pl.BlockSpec((tm, d), row), pl.BlockSpec((1, d), lambda i, j: (0, 0)),
                  pl.BlockSpec((d, tn), col),
                  pl.BlockSpec((tm, LANES), row), pl.BlockSpec((tm, LANES), row),
                  pl.BlockSpec((tm, LANES), row),
                  pl.BlockSpec((1, tn), col), pl.BlockSpec((1, tn), col)],
        out_specs=(pl.BlockSpec((tm, tn), fcol),
                   pl.BlockSpec((tm, tn), lambda i, j: (i, j))),
        scratch_shapes=[pltpu.VMEM((tm, d), BF16)],
        compiler_params=_cparams(("parallel", "arbitrary")),
        name="proj_rope",
    )(xa, g, w, cos, sa, sb, rmask, cscale)


def _gate_kernel(x_ref, g_ref, w_ref, o_ref, h_scr):
    @pl.when(pl.program_id(1) == 0)
    def _():
        h_scr[...] = _rms(x_ref[...], g_ref[...]).astype(BF16)

    o_ref[...] = jax.nn.sigmoid(_dot(h_scr[...], w_ref[...]))


def _gate_proj(xa, g, w):
    n, d = xa.shape
    ca = w.shape[1]
    tm = _pick_tile(n, 640, 16)
    tn = _pick_tile(ca, 1024, LANES)
    return pl.pallas_call(
        _gate_kernel,
        out_shape=jax.ShapeDtypeStruct((n, ca), F32),
        grid=(n // tm, ca // tn),
        in_specs=[pl.BlockSpec((tm, d), lambda i, j: (i, 0)), pl.BlockSpec((1, d), lambda i, j: (0, 0)),
                  pl.BlockSpec((d, tn), lambda i, j: (0, j))],
        out_specs=pl.BlockSpec((tm, tn), lambda i, j: (i, j)),
        scratch_shapes=[pltpu.VMEM((tm, d), BF16)],
        compiler_params=_cparams(("parallel", "arbitrary")),
        name="gate_proj",
    )(xa, g, w)


def _softmax_block(s, mask, v, m_old, l_old, acc_old, v_is_transposed=False):
    if mask is not None:
        s = jnp.where(mask, s, NEG)
    m_new = jnp.maximum(m_old, jnp.max(s, axis=-1, keepdims=True))
    p = jnp.exp(s - m_new)
    if mask is not None:
        p = jnp.where(mask, p, 0.0)
    alpha = jnp.exp(m_old - m_new)
    l_new = alpha * l_old + jnp.sum(p, axis=-1, keepdims=True)
    pv = _nt(p.astype(v.dtype), v) if v_is_transposed else _dot(p.astype(v.dtype), v)
    return m_new, l_new, alpha * acc_old + pv


def _flash_update(s, mask, v, m_ref, l_ref, acc_ref, idx, v_is_transposed=False):
    m, l, acc = _softmax_block(s, mask, v, m_ref[idx], l_ref[idx], acc_ref[idx], v_is_transposed)
    m_ref[idx] = m
    l_ref[idx] = l
    acc_ref[idx] = acc


def _flash_init(m_ref, l_ref, acc_ref):
    m_ref[...] = jnp.full(m_ref.shape, NEG, F32)
    l_ref[...] = jnp.zeros(l_ref.shape, F32)
    acc_ref[...] = jnp.zeros(acc_ref.shape, F32)


def _da_lambda(lam_ref, lam_init):
    a = jnp.sum(lam_ref[0:1, :] * lam_ref[1:2, :], axis=-1, keepdims=True)
    b = jnp.sum(lam_ref[2:3, :] * lam_ref[3:4, :], axis=-1, keepdims=True)
    return jnp.exp(a) - jnp.exp(b) + lam_init


def _da_finish(a0, a1, lam, sub, lam_init):
    o = a0 - lam * a1
    return _rms(o, sub) * (1.0 - lam_init)


def _flash_rows(s, bias, v1, m_ref, acc_ref, idx):
    if bias is not None:
        s = s + bias
    m_old = m_ref[idx]
    m_new = jnp.maximum(m_old, jnp.max(s, axis=-1, keepdims=True))
    p = jnp.exp(s - jnp.tile(m_new, (1, s.shape[1] // LANES)))
    alpha = jnp.exp(m_old - m_new)
    acc_ref[idx] = jnp.tile(alpha, (1, v1.shape[1] // LANES)) * acc_ref[idx] + _dot(p.astype(v1.dtype), v1)
    m_ref[idx] = m_new


def _tri_tables(nq):
    qi = [q for q in range(nq) for _ in range(q + 1)]
    ki = [k for q in range(nq) for k in range(q + 1)]
    return jnp.asarray(qi, I32), jnp.asarray(ki, I32)


def _da_flash_kernel(qi_ref, ki_ref, lam_ref, sub_ref, q_ref, k_ref, v_ref, base_ref, o_ref, m_scr, acc_scr,
                     *, tq, rb, kb, lam_init):
    del base_ref
    t = pl.program_id(2)
    qi = qi_ref[t]
    ki = ki_ref[t]

    @pl.when(ki == 0)
    def _():
        m_scr[...] = jnp.full(m_scr.shape, NEG, F32)
        acc_scr[...] = jnp.zeros(acc_scr.shape, F32)

    upper = lax.broadcasted_iota(I32, (1, LANES), 1) >= DA_DH

    def step(diag):
        k = k_ref[...]
        v1 = jnp.concatenate([v_ref[...], jnp.ones((tq, DA_DV), BF16)], axis=1)
        for r in range(DA_REP):
            for r0 in range(0, tq, rb):
                nk = r0 + rb if diag else tq
                q = q_ref[pl.ds(r0, rb), r * DA_DV:(r + 1) * DA_DV]
                qc = [jnp.where(upper if c == 1 else jnp.logical_not(upper), q, jnp.zeros_like(q))
                      for c in range(2)]
                for c0 in range(0, nk, kb):
                    c1 = min(c0 + kb, nk)
                    bias = None
                    if diag and c1 > r0:
                        row = r0 - c0 + lax.broadcasted_iota(I32, (rb, c1 - c0), 0)
                        bias = jnp.where(lax.broadcasted_iota(I32, (rb, c1 - c0), 1) <= row, 0.0, NEG)
                    for c in range(2):
                        _flash_rows(_nt(qc[c], k[c0:c1]), bias, v1[c0:c1], m_scr, acc_scr,
                                    (c, pl.ds(r * tq + r0, rb)))

    @pl.when(ki < qi)
    def _():
        step(False)

    @pl.when(ki == qi)
    def _():
        step(True)
        lam = _da_lambda(lam_ref, lam_init)
        a0 = acc_scr[0, :, :DA_DV] / acc_scr[0, :, DA_DV:]
        a1 = acc_scr[1, :, :DA_DV] / acc_scr[1, :, DA_DV:]
        o = _da_finish(a0, a1, lam, sub_ref[...], lam_init)
        for r in range(DA_REP):
            o_ref[:, r * DA_DV:(r + 1) * DA_DV] = o[r * tq:(r + 1) * tq].astype(o_ref.dtype)


def _da_flash(zb, base, b, s, lam_p, subln, lam_init):
    tq = _pick_tile(s, FLASH_TILE, 16)
    rb = _pick_tile(tq, FLASH_ROWS, 16)
    kb = _pick_tile(tq, FLASH_KEYS, LANES)
    nq = s // tq
    qtab, ktab = _tri_tables(nq)
    kcol, vcol = S1 // DA_DV, S2 // DA_DV
    grid_spec = pltpu.PrefetchScalarGridSpec(
        num_scalar_prefetch=2, grid=(b, DA_KV, qtab.shape[0]),
        in_specs=[pl.BlockSpec((4, DA_DH), lambda bi, g, t, qt, kt: (0, 0)),
                  pl.BlockSpec((1, DA_DV), lambda bi, g, t, qt, kt: (0, 0)),
                  pl.BlockSpec((tq, DA_REP * DA_DV), lambda bi, g, t, qt, kt: (bi * nq + qt[t], g)),
                  pl.BlockSpec((tq, DA_DV), lambda bi, g, t, qt, kt: (bi * nq + kt[t], kcol + g)),
                  pl.BlockSpec((tq, DA_DV), lambda bi, g, t, qt, kt: (bi * nq + kt[t], vcol + g)),
                  pl.BlockSpec(memory_space=pl.ANY)],
        out_specs=pl.BlockSpec((tq, DA_REP * DA_DV), lambda bi, g, t, qt, kt: (bi * nq + qt[t], g)),
        scratch_shapes=[pltpu.VMEM((2, DA_REP * tq, LANES), F32),
                        pltpu.VMEM((2, DA_REP * tq, 2 * DA_DV), F32)])
    return pl.pallas_call(
        functools.partial(_da_flash_kernel, tq=tq, rb=rb, kb=kb, lam_init=lam_init),
        out_shape=jax.ShapeDtypeStruct(base.shape, base.dtype),
        grid_spec=grid_spec,
        input_output_aliases={7: 0},
        compiler_params=_cparams(("parallel", "parallel", "arbitrary")),
        name="da_flash",
    )(qtab, ktab, lam_p, subln, zb, zb, zb, base)


def _page_steps(npages):
    pg = _pick_tile(npages, DECODE_PAGES, 1)
    return pg, npages // pg


def _page_specs(block, npages, pg, col_block):
    nd = len(block) - 1

    def spec(u):
        def index(b, p, pt):
            idx = [pt[b, jnp.minimum(p * pg + u, npages - 1)]] + [0] * nd
            if col_block is not None:
                idx[1] = col_block
            return tuple(idx)
        return pl.BlockSpec(block, index)

    return [spec(u) for u in range(pg)]


def _da_decode_kernel(pt_ref, lam_ref, sub_ref, q_ref, *refs, nsteps, pg, page, rq, lam_init):
    del pt_ref
    c_refs = refs[:pg]
    n_ref, o_ref, m_scr, l_scr, acc_scr = refs[pg:]
    p = pl.program_id(1)
    per = 2 * DA_KV

    @pl.when(p == 0)
    def _():
        _flash_init(m_scr, l_scr, acc_scr)

    q = q_ref[0]
    rows = q.shape[0]

    def heads(ref, first):
        return jnp.concatenate([ref[0, pl.ds(first + g, page, stride=per), :] for g in range(DA_KV)],
                               axis=1).astype(BF16)

    def step(blocks, mask):
        k = jnp.concatenate([heads(r, 0) for r in blocks], axis=0)
        v = jnp.concatenate([heads(r, DA_KV) for r in blocks], axis=0)
        _flash_update(_nt(q, k), mask, v, m_scr, l_scr, acc_scr, 0)

    @pl.when(p < nsteps)
    def _():
        step(c_refs, None)

    @pl.when(p == nsteps)
    def _():
        row = lax.broadcasted_iota(I32, (rows, page), 0)
        col = lax.broadcasted_iota(I32, (rows, page), 1)
        step([n_ref], col <= (row % rq) // DA_REP)
        lam = _da_lambda(lam_ref, lam_init)
        a = acc_scr[0] / l_scr[0]
        for g in range(DA_KV):
            cols = slice(g * DA_DV, (g + 1) * DA_DV)
            a0 = a[g * 2 * rq:g * 2 * rq + rq, cols]
            a1 = a[g * 2 * rq + rq:(g + 1) * 2 * rq, cols]
            o_ref[0, g] = _da_finish(a0, a1, lam, sub_ref[...], lam_init)


def _da_decode(page_table, qbd, cache, newkv, lam_p, subln, lam_init):
    nb, rows, ck = qbd.shape
    npages = page_table.shape[1]
    prow = cache.shape[1]
    page = prow // (2 * DA_KV)
    rq = rows // (DA_KV * 2)
    pg, nsteps = _page_steps(npages)
    grid_spec = pltpu.PrefetchScalarGridSpec(
        num_scalar_prefetch=1, grid=(nb, nsteps + 1),
        in_specs=[pl.BlockSpec((4, DA_DH), lambda b, p, pt: (0, 0)),
                  pl.BlockSpec((1, DA_DV), lambda b, p, pt: (0, 0)),
                  pl.BlockSpec((1, rows, ck), lambda b, p, pt: (b, 0, 0))]
        + _page_specs((1, prow, DA_DV), npages, pg, None)
        + [pl.BlockSpec((1, prow, DA_DV), lambda b, p, pt: (b, 0, 0))],
        out_specs=pl.BlockSpec((1, DA_KV, rq, DA_DV), lambda b, p, pt: (b, 0, 0, 0)),
        scratch_shapes=[pltpu.VMEM((1, rows, 1), F32), pltpu.VMEM((1, rows, 1), F32),
                        pltpu.VMEM((1, rows, DA_KV * DA_DV), F32)])
    return pl.pallas_call(
        functools.partial(_da_decode_kernel, nsteps=nsteps, pg=pg, page=page, rq=rq, lam_init=lam_init),
        out_shape=jax.ShapeDtypeStruct((nb, DA_KV, rq, DA_DV), F32),
        grid_spec=grid_spec,
        compiler_params=_cparams(("parallel", "arbitrary")),
        name="da_decode",
    )(page_table, lam_p, subln, qbd, *([cache] * pg), newkv)


def _compress_kernel(k_ref, v_ref, wp_ref, wc_ref, ko_ref, vo_ref):
    for c, (src, dst) in enumerate(((k_ref, ko_ref), (v_ref, vo_ref))):
        x = src[...]
        t = x.shape[0]
        y = jnp.sum(x.reshape(t // BLOCK, BLOCK, LANES) * wp_ref[c][None], axis=1)
        dst[...] = _dot(y, wc_ref[c], precision=HIGHEST)


def _compress_prompt(zf, wp, wc, b, s):
    nc = s // BLOCK
    kc0 = (S4 - PROJ_F32_FROM) // LANES
    out = jax.ShapeDtypeStruct((b * nc, LANES), F32)
    return pl.pallas_call(
        _compress_kernel,
        out_shape=(out, out),
        grid=(b,),
        in_specs=[pl.BlockSpec((s, LANES), lambda i: (i, kc0)), pl.BlockSpec((s, LANES), lambda i: (i, kc0 + 1)),
                  pl.BlockSpec((2, BLOCK, LANES), lambda i: (0, 0, 0)),
                  pl.BlockSpec((2, LANES, LANES), lambda i: (0, 0, 0))],
        out_specs=(pl.BlockSpec((nc, LANES), lambda i: (i, 0)), pl.BlockSpec((nc, LANES), lambda i: (i, 0))),
        compiler_params=_cparams(("parallel",)),
        name="nsa_compress",
    )(zf, zf, wp, wc)


def _masked_probs(s, ok):
    s = jnp.where(ok, s, NEG)
    e = jnp.where(ok, jnp.exp(s - jnp.max(s, axis=-1, keepdims=True)), 0.0)
    den = jnp.sum(e, axis=-1, keepdims=True)
    return e / jnp.where(den > 0.0, den, 1.0)


def _select_blocks(imp, pos, ns):
    width = imp.shape[1]
    blk = lax.broadcasted_iota(I32, (1, width), 1)
    cur = lax.shift_right_logical(pos, BLOCK_SHIFT)
    forced = (blk == 0) | (blk == cur) | (blk == cur - 1)
    score = jnp.where(forced, imp + FORCE_BONUS, imp)
    score = jnp.where(blk <= cur, score, -1.0)
    score = jnp.where(blk < ns, score, 0.5 * LOWEST)
    sel = jnp.zeros(imp.shape, F32)
    for _ in range(N_SELECT):
        mx = jnp.max(score, axis=-1, keepdims=True)
        idx = jnp.min(jnp.where(score == mx, blk, width), axis=-1, keepdims=True)
        hit = blk == idx
        sel = jnp.where(hit, 1.0, sel)
        score = jnp.where(hit, LOWEST, score)
    return sel


def _half_mask(half):
    upper = lax.broadcasted_iota(I32, (1, LANES), 1) >= NSA_DH
    return upper if half == 1 else jnp.logical_not(upper)


def _head_q(q_ref, rows, r):
    q = q_ref[rows, (r // 2) * LANES:(r // 2 + 1) * LANES]
    return jnp.where(_half_mask(r % 2), q, jnp.zeros_like(q))


def _cmp_select_kernel(q_ref, kc_ref, vc_ref, base_ref, o_ref, sel_ref, *, tq, nc, ns, width):
    del base_ref
    qi = pl.program_id(1)
    pos = qi * tq + lax.broadcasted_iota(I32, (tq, 1), 0)
    blk = lax.broadcasted_iota(I32, (1, width), 1)
    cmp_ok = (blk < nc) & ((blk + 1) * BLOCK - 1 <= pos)
    hw = NSA_REP * NSA_DH
    for g in range(NSA_KV):
        kc = kc_ref[0, g]
        imp = jnp.zeros((tq, width), F32)
        for j in range(NSA_REP // 2):
            out = None
            for a in range(2):
                q = q_ref[:, g * hw + j * LANES:g * hw + (j + 1) * LANES]
                q = jnp.where(_half_mask(a), q, jnp.zeros_like(q))
                p = _masked_probs(_nt(q, kc), cmp_ok)
                imp = imp + p
                term = _dot(p.astype(BF16), vc_ref[0, g, a])
                out = term if out is None else out + term
            o_ref[:, g * hw + j * LANES:g * hw + (j + 1) * LANES] = out
        sel_ref[0, g] = _select_blocks(imp, pos, ns).astype(sel_ref.dtype)


def _cmp_decode_kernel(pt_ref, q_ref, wt_ref, wct_ref, *refs, nsteps, pg, page, past, t4, nc, width):
    del pt_ref
    c_refs = refs[:pg]
    o_ref, imp_ref, acc = refs[pg:]
    p = pl.program_id(1)
    kw = NSA_KV * NSA_DH
    cw = acc.shape[1]

    @pl.when(p == 0)
    def _():
        acc[...] = jnp.zeros(acc.shape, F32)

    z = jnp.concatenate([c_refs[u][0].reshape(2 * kw, page) * wt_ref[...] for u in range(pg)], axis=1)
    tokblk = lax.shift_right_logical(lax.broadcasted_iota(I32, (pg * page, cw), 0), BLOCK_SHIFT)
    lane = lax.broadcasted_iota(I32, (pg * page, cw), 1)
    place = jnp.where(lane == p * (pg * page // BLOCK) + tokblk, 1.0, 0.0).astype(BF16)
    total = None
    for _ in range(3):
        piece = z.astype(BF16)
        z = z - piece.astype(F32)
        term = _dot(piece, place)
        total = term if total is None else total + term
    acc[...] += total

    @pl.when(p == nsteps - 1)
    def _():
        y = acc[...]
        rows = t4 * NSA_REP
        pos = past + lax.broadcasted_iota(I32, (rows, 1), 0) // NSA_REP
        blk = lax.broadcasted_iota(I32, (1, cw), 1)
        cmp_ok = (blk < nc) & ((blk + 1) * BLOCK - 1 <= pos)
        for g in range(NSA_KV):
            kct = _dot(wct_ref[0], y[g * NSA_DH:(g + 1) * NSA_DH], precision=HIGHEST)
            vct = _dot(wct_ref[1], y[kw + g * NSA_DH:kw + (g + 1) * NSA_DH], precision=HIGHEST)
            pr = _masked_probs(_dot(q_ref[0, g], kct.astype(BF16)), cmp_ok)
            o_ref[0, g] = _nt(pr.astype(BF16), vct.astype(BF16))
            imp = jnp.sum(pr.reshape(t4, NSA_REP, cw), axis=1)
            if width > cw:
                imp = jnp.concatenate([imp, jnp.zeros((t4, width - cw), F32)], axis=1)
            imp_ref[0, g] = imp


def _select_decode_kernel(imp_ref, sel_ref, *, past, t4, ns):
    rows = imp_ref.shape[0]
    pos = past + lax.broadcasted_iota(I32, (rows, 1), 0) % t4
    sel_ref[...] = _select_blocks(imp_ref[...], pos, ns).astype(sel_ref.dtype)


def _cmp_decode(page_table, qst, cache_t, wt, wct, *, past, t4, nc, ns, width):
    nb, npages = page_table.shape
    page = cache_t.shape[4]
    pg, nsteps = _page_steps(npages)
    rows = t4 * NSA_REP
    cw = _round_up(nc, LANES)
    kw = NSA_KV * NSA_DH
    grid_spec = pltpu.PrefetchScalarGridSpec(
        num_scalar_prefetch=1, grid=(nb, nsteps),
        in_specs=[pl.BlockSpec((1, NSA_KV, rows, NSA_DH), lambda b, p, pt: (b, 0, 0, 0)),
                  pl.BlockSpec((2 * kw, page), lambda b, p, pt: (0, 0)),
                  pl.BlockSpec((2, NSA_DH, NSA_DH), lambda b, p, pt: (0, 0, 0))]
        + _page_specs((1, 2, NSA_KV, NSA_DH, page), npages, pg, 0),
        out_specs=(pl.BlockSpec((1, NSA_KV, rows, NSA_DH), lambda b, p, pt: (b, 0, 0, 0)),
                   pl.BlockSpec((1, NSA_KV, t4, width), lambda b, p, pt: (b, 0, 0, 0))),
        scratch_shapes=[pltpu.VMEM((2 * kw, cw), F32)])
    ocmp, imp = pl.pallas_call(
        functools.partial(_cmp_decode_kernel, nsteps=nsteps, pg=pg, page=page, past=past, t4=t4,
                          nc=nc, width=width),
        out_shape=(jax.ShapeDtypeStruct((nb, NSA_KV, rows, NSA_DH), F32),
                   jax.ShapeDtypeStruct((nb, NSA_KV, t4, width), F32)),
        grid_spec=grid_spec,
        compiler_params=_cparams(("parallel", "arbitrary")),
        name="nsa_cmp_decode",
    )(page_table, qst, wt, wct, *([cache_t] * pg))
    nrow = nb * NSA_KV * t4
    sel = pl.pallas_call(
        functools.partial(_select_decode_kernel, past=past, t4=t4, ns=ns),
        out_shape=jax.ShapeDtypeStruct((nrow, width), BF16),
        grid=(1,),
        in_specs=[pl.BlockSpec((nrow, width), lambda i: (0, 0))],
        out_specs=pl.BlockSpec((nrow, width), lambda i: (0, 0)),
        compiler_params=_cparams(("arbitrary",)),
        name="nsa_select_decode",
    )(imp.reshape(nrow, width))
    return ocmp, sel.reshape(nb, NSA_KV, t4, width)


def _cmp_select(zb, kc2, vc2, base, b, s, *, nc, ns):
    width = kc2.shape[2]
    tq = _pick_tile(s, 256, 16)
    nq = s // tq
    qcol = S3 // C_NSA_Q
    return pl.pallas_call(
        functools.partial(_cmp_select_kernel, tq=tq, nc=nc, ns=ns, width=width),
        out_shape=(jax.ShapeDtypeStruct(base.shape, base.dtype),
                   jax.ShapeDtypeStruct((b, NSA_KV, s, width), BF16)),
        grid=(b, nq),
        in_specs=[pl.BlockSpec((tq, C_NSA_Q), lambda bi, qi: (bi * nq + qi, qcol)),
                  pl.BlockSpec((1, NSA_KV, width, LANES), lambda bi, qi: (bi, 0, 0, 0)),
                  pl.BlockSpec((1, NSA_KV, 2, width, LANES), lambda bi, qi: (bi, 0, 0, 0, 0)),
                  pl.BlockSpec(memory_space=pl.ANY)],
        out_specs=(pl.BlockSpec((tq, C_NSA_Q), lambda bi, qi: (bi * nq + qi, 0)),
                   pl.BlockSpec((1, NSA_KV, tq, width), lambda bi, qi: (bi, 0, qi, 0))),
        input_output_aliases={3: 0},
        compiler_params=_cparams(("parallel", "parallel")),
        name="nsa_cmp_select",
    )(zb, kc2, vc2, base)


def _nsa_flash_kernel(qi_ref, ki_ref, *refs, mode, tq, rb, kb, nsteps, band):
    if mode == "slc":
        q_ref, k_ref, v_ref, sel_ref, e_ref, _, o_ref, m_scr, acc_scr = refs
    else:
        q_ref, k_ref, v_ref, _, o_ref, m_scr, acc_scr = refs
    t = pl.program_id(2)
    qi = qi_ref[t]
    ki = ki_ref[t]
    first = (t == 0) | (qi_ref[jnp.maximum(t - 1, 0)] != qi)
    last = (t == nsteps - 1) | (qi_ref[jnp.minimum(t + 1, nsteps - 1)] != qi)

    @pl.when(first)
    def _():
        m_scr[...] = jnp.full(m_scr.shape, NEG, F32)
        acc_scr[...] = jnp.zeros(acc_scr.shape, F32)

    if mode == "slc":
        picked = _dot(sel_ref[0, 0], e_ref[...])
    k = k_ref[...]
    v1 = v_ref[...]
    diag = qi == ki
    for r0 in range(0, tq, rb):
        rows = pl.ds(r0, rb)
        nk = _round_up(r0 + rb, LANES)

        def update(ncols, off_diag=False, r0=r0, rows=rows):
            for c0 in range(0, ncols, kb):
                c1 = min(c0 + kb, ncols)
                if off_diag and band == 1 and tq + r0 - (c1 - 1) >= WINDOW:
                    continue
                d = ((qi - ki) * tq + r0 - c0 + lax.broadcasted_iota(I32, (rb, c1 - c0), 0)
                     - lax.broadcasted_iota(I32, (rb, c1 - c0), 1))
                if mode == "slc":
                    visible = (picked[r0:r0 + rb, c0:c1] > 0.5) & (d >= 0)
                else:
                    visible = (d >= 0) & (d < WINDOW)
                bias = jnp.where(visible, 0.0, NEG)
                for h in range(NSA_REP):
                    s = _nt(_head_q(q_ref, rows, h), k[c0:c1])
                    _flash_rows(s, bias, v1[c0:c1], m_scr, acc_scr, (h, rows))

        if nk < tq or band == 1:
            pl.when(diag)(functools.partial(update, nk))
            pl.when(jnp.logical_not(diag))(functools.partial(update, tq, True))
        else:
            update(tq)

    @pl.when(last)
    def _():
        low = _half_mask(0)
        for j in range(NSA_REP // 2):
            a, bq = acc_scr[2 * j], acc_scr[2 * j + 1]
            num = jnp.where(low, a, pltpu.roll(bq, NSA_DH, 1))
            den = jnp.where(low, pltpu.roll(a, NSA_DH, 1), bq)
            o_ref[:, j * LANES:(j + 1) * LANES] = num / den


def _nsa_flash(zb, k2, v1, base, b, s, sel=None, emat=None):
    tq = _pick_tile(s, FLASH_TILE, 16)
    rb = _pick_tile(tq, FLASH_ROWS, 16)
    kb = _pick_tile(tq, FLASH_KEYS, LANES)
    nq = s // tq
    back = 0
    if sel is not None:
        mode = "slc"
        qtab, ktab = _tri_tables(nq)
    else:
        mode = "win"
        back = -(-WINDOW // tq)
        pairs = [(q, kk) for q in range(nq) for kk in range(max(q - back, 0), q + 1)]
        qtab = jnp.asarray([p[0] for p in pairs], I32)
        ktab = jnp.asarray([p[1] for p in pairs], I32)
    nsteps = qtab.shape[0]
    gw = NSA_REP * NSA_DH
    qcol = S3 // gw
    in_specs = [pl.BlockSpec((tq, gw), lambda bi, g, t, qt, kt: (bi * nq + qt[t], qcol + g)),
                pl.BlockSpec((tq, LANES), lambda bi, g, t, qt, kt: (bi * nq + kt[t], g)),
                pl.BlockSpec((tq, LANES), lambda bi, g, t, qt, kt: (bi * nq + kt[t], g))]
    args = [zb, k2, v1]
    if sel is not None:
        width = sel.shape[3]
        in_specs += [pl.BlockSpec((1, 1, tq, width), lambda bi, g, t, qt, kt: (bi, g, qt[t], 0)),
                     pl.BlockSpec((width, tq), lambda bi, g, t, qt, kt: (0, kt[t]))]
        args += [sel, emat]
    in_specs.append(pl.BlockSpec(memory_space=pl.ANY))
    args.append(base)
    grid_spec = pltpu.PrefetchScalarGridSpec(
        num_scalar_prefetch=2, grid=(b, NSA_KV, nsteps),
        in_specs=in_specs,
        out_specs=pl.BlockSpec((tq, gw), lambda bi, g, t, qt, kt: (bi * nq + qt[t], g)),
        scratch_shapes=[pltpu.VMEM((NSA_REP, tq, LANES), F32), pltpu.VMEM((NSA_REP, tq, LANES), F32)])
    return pl.pallas_call(
        functools.partial(_nsa_flash_kernel, mode=mode, tq=tq, rb=rb, kb=kb, nsteps=nsteps, band=back),
        out_shape=jax.ShapeDtypeStruct(base.shape, base.dtype),
        grid_spec=grid_spec,
        input_output_aliases={1 + len(args): 0},
        compiler_params=_cparams(("parallel", "parallel", "arbitrary")),
        name="nsa_" + mode,
    )(qtab, ktab, *args)


def _row_token(rows, cols, t4):
    row = lax.broadcasted_iota(I32, (rows, cols), 0)
    return (row // NSA_REP) % t4


def _kv_t(blocks):
    kw = NSA_KV * NSA_DH
    kt = [r[0, 0].reshape(kw, r.shape[4]) for r in blocks]
    vt = [r[0, 1].reshape(kw, r.shape[4]) for r in blocks]
    cat = lambda xs: (xs[0] if len(xs) == 1 else jnp.concatenate(xs, axis=1)).astype(BF16)
    return cat(kt), cat(vt)


def _slc_decode_kernel(pt_ref, q_ref, sel_ref, e_ref, *refs, nsteps, pg, page, past, t4):
    del pt_ref
    c_refs = refs[:pg]
    n_ref, o_ref, m_scr, l_scr, acc_scr = refs[pg:]
    p = pl.program_id(1)

    @pl.when(p == 0)
    def _():
        _flash_init(m_scr, l_scr, acc_scr)

    q = q_ref[0]
    rows = q.shape[0]

    def step(blocks):
        kt, vt = _kv_t(blocks)
        nk = kt.shape[1]
        kpos = p * (pg * page) + lax.broadcasted_iota(I32, (rows, nk), 1)
        picked = _dot(sel_ref[0], e_ref[:, :nk]) > 0.5
        mask = picked & (kpos <= past + _row_token(rows, nk, t4))
        _flash_update(_dot(q, kt), mask, vt, m_scr, l_scr, acc_scr, 0, v_is_transposed=True)

    @pl.when(p < nsteps)
    def _():
        step(c_refs)

    @pl.when(p == nsteps)
    def _():
        step([n_ref])
        o_ref[0] = acc_scr[0] / l_scr[0]


def _slc_decode(page_table, qnb, selrows, emat, cache_t, new_t, *, past, t4):
    nb, rows, _ = qnb.shape
    npages = page_table.shape[1]
    page = cache_t.shape[4]
    width = selrows.shape[2]
    pg, nsteps = _page_steps(npages)
    blk = (1, 2, NSA_KV, NSA_DH, page)
    grid_spec = pltpu.PrefetchScalarGridSpec(
        num_scalar_prefetch=1, grid=(nb, nsteps + 1),
        in_specs=[pl.BlockSpec((1, rows, LANES), lambda b, p, pt: (b, 0, 0)),
                  pl.BlockSpec((1, rows, width), lambda b, p, pt: (b, 0, 0)),
                  pl.BlockSpec((width, pg * page), lambda b, p, pt: (0, p))]
        + _page_specs(blk, npages, pg, 1)
        + [pl.BlockSpec(blk, lambda b, p, pt: (b, 0, 0, 0, 0))],
        out_specs=pl.BlockSpec((1, rows, LANES), lambda b, p, pt: (b, 0, 0)),
        scratch_shapes=[pltpu.VMEM((1, rows, 1), F32), pltpu.VMEM((1, rows, 1), F32),
                        pltpu.VMEM((1, rows, LANES), F32)])
    return pl.pallas_call(
        functools.partial(_slc_decode_kernel, nsteps=nsteps, pg=pg, page=page, past=past, t4=t4),
        out_shape=jax.ShapeDtypeStruct((nb, rows, LANES), F32),
        grid_spec=grid_spec,
        compiler_params=_cparams(("parallel", "arbitrary")),
        name="nsa_slc_decode",
    )(page_table, qnb, selrows, emat, *([cache_t] * pg), new_t)


def _win_decode_kernel(q_ref, c_ref, n_ref, o_ref, *, past, t4):
    q = q_ref[0]
    rows = q.shape[0]
    wb = c_ref.shape[4]
    m = jnp.full((rows, 1), NEG, F32)
    l = jnp.zeros((rows, 1), F32)
    acc = jnp.zeros((rows, LANES), F32)
    for ref, kstart in ((c_ref, past - wb), (n_ref, past)):
        kt, vt = _kv_t([ref])
        nk = kt.shape[1]
        d = past + _row_token(rows, nk, t4) - (kstart + lax.broadcasted_iota(I32, (rows, nk), 1))
        mask = (d >= 0) & (d < WINDOW)
        m, l, acc = _softmax_block(_dot(q, kt), mask, vt, m, l, acc, v_is_transposed=True)
    o_ref[0] = acc / l


def _win_decode(qnb, cwin_t, new_t, *, past, t4):
    nb, rows, _ = qnb.shape
    return pl.pallas_call(
        functools.partial(_win_decode_kernel, past=past, t4=t4),
        out_shape=jax.ShapeDtypeStruct((nb, rows, LANES), F32),
        grid=(nb,),
        in_specs=[pl.BlockSpec((1, rows, LANES), lambda b: (b, 0, 0)),
                  pl.BlockSpec((1,) + cwin_t.shape[1:], lambda b: (b, 0, 0, 0, 0)),
                  pl.BlockSpec((1,) + new_t.shape[1:], lambda b: (b, 0, 0, 0, 0))],
        out_specs=pl.BlockSpec((1, rows, LANES), lambda b: (b, 0, 0)),
        compiler_params=_cparams(("parallel",)),
        name="nsa_win_decode",
    )(qnb, cwin_t, new_t)


def _merge_kernel(oda_ref, oc_ref, os_ref, ow_ref, gn_ref, ga_ref, gb_ref, e_ref, wda_ref, wns_ref, o_ref):
    g = gn_ref[...]
    ghi = g.astype(BF16)
    glo = (g - ghi.astype(F32)).astype(BF16)
    ons = None
    for b, src in enumerate((oc_ref, os_ref, ow_ref)):
        ge = _dot(ghi, e_ref[b]) + _dot(glo, e_ref[b])
        term = ge * src[...]
        ons = term if ons is None else ons + term
    yda = _dot(oda_ref[...], wda_ref[...])
    yns = _dot(ons.astype(BF16), wns_ref[...])
    o_ref[...] = (ga_ref[...] * yda + gb_ref[...] * yns).astype(o_ref.dtype)


def _merge(oda, ocmp, oslc, owin, gns, gmg, emat, wda, wns):
    n, hd = oda.shape
    d = wda.shape[1]
    tm = _pick_tile(n, 320, 16)
    row = lambda i: (i, 0)
    const2 = lambda i: (0, 0)
    return pl.pallas_call(
        _merge_kernel,
        out_shape=jax.ShapeDtypeStruct((n, d), BF16),
        grid=(n // tm,),
        in_specs=[pl.BlockSpec((tm, hd), row), pl.BlockSpec((tm, hd), row), pl.BlockSpec((tm, hd), row),
                  pl.BlockSpec((tm, hd), row), pl.BlockSpec((tm, LANES), row),
                  pl.BlockSpec((tm, d), lambda i: (i, 0)), pl.BlockSpec((tm, d), lambda i: (i, 1)),
                  pl.BlockSpec((3, LANES, hd), lambda i: (0, 0, 0)),
                  pl.BlockSpec((hd, d), const2), pl.BlockSpec((hd, d), const2)],
        out_specs=pl.BlockSpec((tm, d), row),
        compiler_params=_cparams(("parallel",)),
        name="mixer_merge",
    )(oda, ocmp, oslc, owin, gns, gmg, gmg, emat, wda, wns)


def _lane_pick(x, lane, k):
    return jnp.sum(jnp.where(lane == k, x, 0.0), axis=-1, keepdims=True)


def _route(logits):
    lane = lax.broadcasted_iota(I32, logits.shape, 1)
    isg = lane < N_GROUPS
    gmax = jnp.max(jnp.where(isg, logits, LOWEST), axis=-1, keepdims=True)
    gsel = jnp.min(jnp.where(isg & (logits == gmax), lane, LANES), axis=-1, keepdims=True)
    pgrp = 1.0 / jnp.sum(jnp.where(isg, jnp.exp(logits - gmax), 0.0), axis=-1, keepdims=True)
    lo = N_GROUPS + gsel * EXP_PER_GROUP
    ing = (lane >= lo) & (lane < lo + EXP_PER_GROUP)
    v1 = jnp.max(jnp.where(ing, logits, LOWEST), axis=-1, keepdims=True)
    i1 = jnp.min(jnp.where(ing & (logits == v1), lane, LANES), axis=-1, keepdims=True)
    ing2 = ing & (lane != i1)
    v2 = jnp.max(jnp.where(ing2, logits, LOWEST), axis=-1, keepdims=True)
    i2 = jnp.min(jnp.where(ing2 & (logits == v2), lane, LANES), axis=-1, keepdims=True)
    t = jnp.exp(v2 - v1)
    w1 = pgrp / (1.0 + t)
    w2 = pgrp * t / (1.0 + t)
    e1 = (i1 - N_GROUPS).astype(F32)
    e2 = (i2 - N_GROUPS).astype(F32)
    return jnp.where(lane == 0, e1, jnp.where(lane == 1, e2, jnp.where(lane == 2, w1, jnp.where(lane == 3, w2, 0.0))))


def _store_rows(ref, y, rows, nchunk):
    for s in range(nchunk):
        ref[pl.ds(s, rows, stride=nchunk), :] = y[:, s * LANES:(s + 1) * LANES]


def _wo_route_kernel(m_ref, x_ref, wo_ref, gf_ref, wr_ref, br_ref, x1_ref, h_ref, rt_ref, *, nchunk):
    x1 = x_ref[...] + _dot(m_ref[...], wo_ref[...])
    x1_ref[...] = x1
    h = _rms(x1, gf_ref[...])
    _store_rows(h_ref, h, h.shape[0], nchunk)
    hi = h.astype(BF16)
    lo = (h - hi.astype(F32)).astype(BF16)
    w = wr_ref[...]
    w_hi = w.astype(BF16)
    w_lo = (w - w_hi.astype(F32)).astype(BF16)
    logits = _dot(hi, w_hi) + (_dot(hi, w_lo) + _dot(lo, w_hi))
    rt_ref[...] = _route(logits + br_ref[...])


def _wo_route(mrg, xa, wo, gffn, wr, br):
    n, d = xa.shape
    tm = _pick_tile(n, 320, 16)
    nchunk = d // LANES
    row = lambda i: (i, 0)
    const2 = lambda i: (0, 0)
    return pl.pallas_call(
        functools.partial(_wo_route_kernel, nchunk=nchunk),
        out_shape=(jax.ShapeDtypeStruct((n, d), F32), jax.ShapeDtypeStruct((n * nchunk, LANES), F32),
                   jax.ShapeDtypeStruct((n, LANES), F32)),
        grid=(n // tm,),
        in_specs=[pl.BlockSpec((tm, d), row), pl.BlockSpec((tm, d), row), pl.BlockSpec((d, d), const2),
                  pl.BlockSpec((1, d), const2), pl.BlockSpec((d, LANES), const2),
                  pl.BlockSpec((1, LANES), const2)],
        out_specs=(pl.BlockSpec((tm, d), row), pl.BlockSpec((tm * nchunk, LANES), row),
                   pl.BlockSpec((tm, LANES), row)),
        compiler_params=_cparams(("parallel",)),
        name="wo_route",
    )(mrg, xa, wo, gffn, wr, br)


def _onehots(rt):
    lane = lax.broadcasted_iota(I32, rt.shape, 1)
    lane_f = lane.astype(F32)
    oh0 = jnp.where(lane_f == _lane_pick(rt, lane, 0), 1.0, 0.0)
    oh1 = jnp.where(lane_f == _lane_pick(rt, lane, 1), 1.0, 0.0)
    return lane, oh0, oh1


def _rank_kernel(rt_ref, tri_ref, rk_ref, sz_ref, carry):
    @pl.when(pl.program_id(0) == 0)
    def _():
        carry[...] = jnp.zeros(carry.shape, F32)

    lane, oh0, oh1 = _onehots(rt_ref[...])
    oh = oh0 + oh1
    before = _dot(tri_ref[...], oh.astype(BF16)) + carry[0:1, :]
    r0 = jnp.sum(before * oh0, axis=-1, keepdims=True)
    r1 = jnp.sum(before * oh1, axis=-1, keepdims=True)
    rk_ref[...] = jnp.where(lane == 0, r0, jnp.where(lane == 1, r1, 0.0))
    carry[...] = carry[...] + jnp.sum(oh, axis=0, keepdims=True)
    sz_ref[...] = carry[...]


def _expert_ranks(route, tri):
    n = route.shape[0]
    tm = tri.shape[0]
    return pl.pallas_call(
        _rank_kernel,
        out_shape=(jax.ShapeDtypeStruct((n, LANES), F32), jax.ShapeDtypeStruct((8, LANES), F32)),
        grid=(n // tm,),
        in_specs=[pl.BlockSpec((tm, LANES), lambda i: (i, 0)), pl.BlockSpec((tm, tm), lambda i: (0, 0))],
        out_specs=(pl.BlockSpec((tm, LANES), lambda i: (i, 0)), pl.BlockSpec((8, LANES), lambda i: (0, 0))),
        scratch_shapes=[pltpu.VMEM((8, LANES), F32)],
        compiler_params=_cparams(("arbitrary",)),
        name="moe_rank",
    )(route, tri)


def _dest_kernel(rt_ref, rk_ref, sz_ref, dest_ref, be_ref, nu_ref, *, shift):
    lane8 = lax.broadcasted_iota(I32, (8, LANES), 1)
    sizes = sz_ref[...].astype(I32)
    blk = 1 << shift
    padded = jnp.where(lane8 < N_EXPERTS, lax.shift_right_logical(sizes + (blk - 1), shift) * blk, 0)
    ends = padded
    d = 1
    while d < LANES:
        ends = ends + jnp.where(lane8 >= d, pltpu.roll(ends, d, 1), 0)
        d *= 2
    pstart = (ends - padded)[0:1, :].astype(F32)
    rt = rt_ref[...]
    lane, oh0, oh1 = _onehots(rt)
    rk = rk_ref[...]
    d0 = jnp.sum(oh0 * pstart, axis=-1, keepdims=True) + _lane_pick(rk, lane, 0)
    d1 = jnp.sum(oh1 * pstart, axis=-1, keepdims=True) + _lane_pick(rk, lane, 1)
    dest_ref[...] = jnp.where(lane == 0, d0, jnp.where(lane == 1, d1, 0.0)).astype(I32)
    nbt = be_ref.shape[0]
    lane_b = lax.broadcasted_iota(I32, (nbt, LANES), 1)
    first_row = lax.broadcasted_iota(I32, (nbt, LANES), 0) * blk
    cnt = jnp.sum(jnp.where((lane_b < N_EXPERTS) & (ends[0:1, :] <= first_row), 1, 0), axis=-1, keepdims=True)
    be_ref[...] = jnp.broadcast_to(jnp.minimum(cnt, N_EXPERTS - 1), (nbt, LANES))
    total = jnp.sum(jnp.where(lane8 == N_EXPERTS - 1, ends, 0), axis=-1, keepdims=True)
    nu_ref[...] = jnp.broadcast_to(lax.shift_right_logical(total, shift), (8, LANES))


def _destinations(route, rank, sizes, nbt, shift):
    n = route.shape[0]
    tm = _pick_tile(n, 640, 8)
    nbt_p = _round_up(nbt, 8)
    return pl.pallas_call(
        functools.partial(_dest_kernel, shift=shift),
        out_shape=(jax.ShapeDtypeStruct((n, LANES), I32), jax.ShapeDtypeStruct((nbt_p, LANES), I32),
                   jax.ShapeDtypeStruct((8, LANES), I32)),
        grid=(n // tm,),
        in_specs=[pl.BlockSpec((tm, LANES), lambda i: (i, 0)), pl.BlockSpec((tm, LANES), lambda i: (i, 0)),
                  pl.BlockSpec((8, LANES), lambda i: (0, 0))],
        out_specs=(pl.BlockSpec((tm, LANES), lambda i: (i, 0)), pl.BlockSpec((nbt_p, LANES), lambda i: (0, 0)),
                   pl.BlockSpec((8, LANES), lambda i: (0, 0))),
        compiler_params=_cparams(("arbitrary",)),
        name="moe_dest",
    )(route, rank, sizes)


def _expert_kernel(dest_ref, be_ref, nu_ref, h_ref, wg_ref, wu_ref, wd_ref, o_ref,
                   tok, ordinal, xbuf, sem, wgf, wuf, wdf, wsem, wgb, wub, wdb, *, rows, nchunk, n_assign):
    j = pl.program_id(0)
    n_used = nu_ref[0]

    def weight_copies(e, ws):
        copies = []
        for m, (src, dst) in enumerate(((wg_ref, wgf), (wu_ref, wuf), (wd_ref, wdf))):
            band = src.shape[1] // WEIGHT_DMA_SPLIT
            for c in range(WEIGHT_DMA_SPLIT):
                rows_c = pl.ds(c * band, band)
                copies.append(pltpu.make_async_copy(src.at[e, rows_c], dst.at[ws, rows_c], wsem.at[ws, m]))
        return copies

    def start_weights(e, ws):
        for n_c, c in enumerate(weight_copies(e, ws)):
            c.start(priority=n_c % 2)

    def row_copy(tile, slot, r):
        return pltpu.make_async_copy(h_ref.at[tok[tile * rows + r]],
                                     xbuf.at[slot, pl.ds(r * nchunk, nchunk)], sem.at[slot])

    def start_tile(tile, slot):
        def body(r, carry):
            row_copy(tile, slot, 2 * r).start(priority=0)
            row_copy(tile, slot, 2 * r + 1).start(priority=1)
            return carry
        lax.fori_loop(0, rows // 2, body, 0, unroll=4)

    def wait_tile(tile, slot):
        def body(r, carry):
            row_copy(tile, slot, r).wait()
            return carry
        lax.fori_loop(0, rows, body, 0, unroll=8)

    @pl.when(j == 0)
    def _():
        start_weights(be_ref[0], 0)
        ordinal[0] = 0

        def clear(r, carry):
            tok[r] = 0
            return carry
        lax.fori_loop(0, tok.shape[0], clear, 0, unroll=8)

        def place(a, carry):
            tok[dest_ref[a]] = lax.shift_right_logical(a, 1)
            return carry
        lax.fori_loop(0, n_assign, place, 0, unroll=8)
        start_tile(0, 0)

    slot = lax.rem(j, 2)
    active = j < n_used
    expert = be_ref[j]
    changed = (j == 0) | (expert != be_ref[jnp.maximum(j - 1, 0)])

    @pl.when(active & changed)
    def _():
        ws = lax.rem(ordinal[0], 2)
        for c in weight_copies(expert, ws):
            c.wait()
        wgb[...] = wgf[ws].astype(BF16)
        wub[...] = wuf[ws].astype(BF16)
        wdb[...] = wdf[ws].astype(BF16)
        nxt = lax.while_loop(lambda t: (t < n_used) & (be_ref[jnp.minimum(t, n_used - 1)] == expert),
                             lambda t: t + 1, j + 1)

        @pl.when(nxt < n_used)
        def _():
            start_weights(be_ref[nxt], 1 - ws)

        ordinal[0] = ordinal[0] + 1

    @pl.when(active)
    def _():
        @pl.when(j + 1 < n_used)
        def _():
            start_tile(j + 1, 1 - slot)

        wait_tile(j, slot)
        x = jnp.concatenate([xbuf[slot, pl.ds(s, rows, stride=nchunk), :] for s in range(nchunk)],
                            axis=1).astype(BF16)
        a = _dot(x, wgb[...])
        u = _dot(x, wub[...])
        y = _dot((jax.nn.silu(a) * u).astype(BF16), wdb[...])
        _store_rows(o_ref, y, rows, nchunk)

    @pl.when(jnp.logical_not(active))
    def _():
        o_ref[...] = jnp.zeros(o_ref.shape, F32)


def _experts(dest, blk_e, n_used, h3, wg, wu, wd, rows):
    d, de = wg.shape[1], wg.shape[2]
    nchunk = d // LANES
    nbt = blk_e.shape[0]
    grid_spec = pltpu.PrefetchScalarGridSpec(
        num_scalar_prefetch=3, grid=(nbt,),
        in_specs=[pl.BlockSpec(memory_space=pl.ANY)] * 4,
        out_specs=pl.BlockSpec((rows * nchunk, LANES), lambda j, dst, be, nu: (j, 0)),
        scratch_shapes=[pltpu.SMEM((nbt * rows,), I32), pltpu.SMEM((1,), I32),
                        pltpu.VMEM((2, rows * nchunk, LANES), F32), pltpu.SemaphoreType.DMA((2,)),
                        pltpu.VMEM((2, d, de), F32), pltpu.VMEM((2, d, de), F32), pltpu.VMEM((2, de, d), F32),
                        pltpu.SemaphoreType.DMA((2, 3)),
                        pltpu.VMEM((d, de), BF16), pltpu.VMEM((d, de), BF16), pltpu.VMEM((de, d), BF16)])
    return pl.pallas_call(
        functools.partial(_expert_kernel, rows=rows, nchunk=nchunk, n_assign=dest.shape[0]),
        out_shape=jax.ShapeDtypeStruct((nbt * rows * nchunk, LANES), F32),
        grid_spec=grid_spec,
        compiler_params=_cparams(("arbitrary",)),
        name="moe_experts",
    )(dest, blk_e, n_used, h3, wg, wu, wd)


def _combine_kernel(dest_ref, ys_ref, rt_ref, x1_ref, p_ref, gp_ref, wpg_ref, wpl_ref, gfin_ref, o_ref,
                    gbuf, sem, *, tm, nchunk, final):
    i = pl.program_id(0)
    slot = lax.rem(i, 2)

    def copy(tile, sl, j):
        return pltpu.make_async_copy(ys_ref.at[dest_ref[tile * (2 * tm) + j]],
                                     gbuf.at[sl, pl.ds(j * nchunk, nchunk)], sem.at[sl])

    def start_tile(tile, sl):
        def body(j, carry):
            copy(tile, sl, 2 * j).start(priority=0)
            copy(tile, sl, 2 * j + 1).start(priority=1)
            return carry
        lax.fori_loop(0, tm, body, 0, unroll=4)

    @pl.when(i == 0)
    def _():
        start_tile(0, 0)

    @pl.when(i + 1 < pl.num_programs(0))
    def _():
        start_tile(i + 1, 1 - slot)

    def drain(j, carry):
        copy(i, slot, j).wait()
        return carry

    lax.fori_loop(0, 2 * tm, drain, 0, unroll=8)
    rt = rt_ref[...]
    lane = lax.broadcasted_iota(I32, rt.shape, 1)
    y0 = jnp.concatenate([gbuf[slot, pl.ds(s, tm, stride=2 * nchunk), :] for s in range(nchunk)], axis=1)
    y1 = jnp.concatenate([gbuf[slot, pl.ds(nchunk + s, tm, stride=2 * nchunk), :] for s in range(nchunk)], axis=1)
    x2 = x1_ref[...] + (_lane_pick(rt, lane, 2) * y0 + _lane_pick(rt, lane, 3) * y1)
    gate = jax.nn.sigmoid(_dot(_rms(x2, gp_ref[...]).astype(BF16), wpg_ref[...]))
    x3 = x2 + gate * _dot(p_ref[...].astype(BF16), wpl_ref[...])
    o_ref[...] = _rms(x3, gfin_ref[...]) if final else x3


def _combine(dest, ys3, route, x1, pa, gple, wpg, wpl, gfin, final):
    n, d = x1.shape
    nchunk = d // LANES
    tm = _pick_tile(n, 320, 16)
    dp = pa.shape[1]
    row = lambda i, dst: (i, 0)
    const2 = lambda i, dst: (0, 0)
    grid_spec = pltpu.PrefetchScalarGridSpec(
        num_scalar_prefetch=1, grid=(n // tm,),
        in_specs=[pl.BlockSpec(memory_space=pl.ANY),
                  pl.BlockSpec((tm, LANES), row), pl.BlockSpec((tm, d), row), pl.BlockSpec((tm, dp), row),
                  pl.BlockSpec((1, d), const2), pl.BlockSpec((d, d), const2), pl.BlockSpec((dp, d), const2),
                  pl.BlockSpec((1, d), const2)],
        out_specs=pl.BlockSpec((tm, d), row),
        scratch_shapes=[pltpu.VMEM((2, 2 * tm * nchunk, LANES), F32), pltpu.SemaphoreType.DMA((2,))])
    return pl.pallas_call(
        functools.partial(_combine_kernel, tm=tm, nchunk=nchunk, final=final),
        out_shape=jax.ShapeDtypeStruct((n, d), F32),
        grid_spec=grid_spec,
        compiler_params=_cparams(("arbitrary",)),
        name="moe_combine_tail",
    )(dest, ys3, route, x1, pa, gple, wpg, wpl, gfin)


def _rope_tables(pos):
    inv = ROPE_THETA ** (-jnp.arange(ROPE_HALF, dtype=F32) / ROPE_HALF)
    ang = pos.astype(F32)[:, None] * inv[None, :]
    cos, sin = jnp.cos(ang), jnp.sin(ang)
    n = pos.shape[0]
    rest = DA_DH - 2 * ROPE_HALF
    z = jnp.zeros((n, ROPE_HALF), F32)
    zr = jnp.zeros((n, rest), F32)
    c64 = jnp.concatenate([cos, cos, jnp.ones((n, rest), F32)], axis=1)
    a64 = jnp.concatenate([-sin, z, zr], axis=1)
    b64 = jnp.concatenate([z, sin, zr], axis=1)
    rep = LANES // DA_DH
    return jnp.tile(c64, (1, rep)), jnp.tile(a64, (1, rep)), jnp.tile(b64, (1, rep))


def _column_tables():
    ones = lambda n: jnp.ones((n,), F32)
    zeros = lambda n: jnp.zeros((n,), F32)
    gw = NSA_KV * NSA_DH
    rmask = jnp.concatenate([ones(C_DA_Q), ones(C_DA_K), zeros(C_DA_V), ones(C_NSA_Q)]
                            + [ones(gw), zeros(gw)] * 3)
    cscale = jnp.concatenate([ones(C_DA_Q) * DA_DH ** -0.5, ones(C_DA_K), ones(C_DA_V),
                              ones(C_NSA_Q) * NSA_DH ** -0.5, ones(C_NSA_KV)])
    return rmask[None, :], cscale[None, :]


def _block_expand(width, nkeys):
    blk = jnp.arange(width, dtype=I32)[:, None]
    key = jnp.arange(nkeys, dtype=I32)[None, :]
    return (key // BLOCK == blk).astype(BF16)


def _pad_rows(x, axis, size):
    pad = [(0, 0)] * x.ndim
    pad[axis] = (0, size - x.shape[axis])
    return jnp.pad(x, pad)


def _cmp_layout(kc, b, nc, width):
    x = kc.reshape(b, nc, NSA_KV, NSA_DH).transpose(0, 2, 1, 3)
    return _pad_rows(x, 2, width).astype(BF16)


def kernel(x_prompt, x_sample, cache_da_kv, cache_nsa_kv, cache_win_kv, page_table, p_prompt, p_sample,
           g_mix, w_in, da_lambda, da_subln, nsa_cpos, nsa_cmp, w_da_up, w_nsa_up, w_o, g_ffn,
           w_rg, b_rg, w_re, b_re, w_e_gate, w_e_up, w_e_down, g_ple, w_ple_gate, w_ple, g_final):
    b, s, d = x_prompt.shape
    nb, t4, _ = x_sample.shape
    depth = g_mix.shape[0]
    npr, nsm = b * s, nb * t4
    n = npr + nsm
    page = cache_da_kv.shape[2]
    npages = page_table.shape[1]
    past = npages * page
    wb = cache_win_kv.shape[2]
    nchunk = d // LANES
    rq = t4 * DA_REP
    assert rq % 8 == 0 and page % BLOCK == 0 and s % BLOCK == 0 and t4 <= page and d % LANES == 0

    xa = jnp.concatenate([x_prompt.reshape(npr, d), x_sample.reshape(nsm, d)], axis=0)
    pos = jnp.concatenate([jnp.tile(jnp.arange(s, dtype=I32), b), jnp.tile(past + jnp.arange(t4, dtype=I32), nb)])
    cos, sa, sb = _rope_tables(pos)
    rmask, cscale = _column_tables()

    nc_p, ns_p = s // BLOCK, max(-(-s // BLOCK), N_SELECT)
    tot_s = past + t4
    nc_s, ns_s = tot_s // BLOCK, max(-(-tot_s // BLOCK), N_SELECT)
    wid_p, wid_s = _round_up(ns_p, LANES), _round_up(ns_s, LANES)
    emat_p = _block_expand(wid_p, s)
    pg_s, nsteps_s = _page_steps(npages)
    emat_s = _block_expand(wid_s, (nsteps_s + 1) * pg_s * page)

    col = jnp.arange(LANES, dtype=I32)[:, None]
    lane = jnp.arange(NSA_HEADS * NSA_DH, dtype=I32)[None, :]
    gate_expand = jnp.stack([(col == br * NSA_HEADS + lane // NSA_DH) for br in range(3)]).astype(BF16)

    n_assign = 2 * n
    shift = int(math.log2(EXPERT_ROWS))
    nbt = -(-n_assign // EXPERT_ROWS) + N_EXPERTS
    tm_rank = _pick_tile(n, 640, 16)
    tri = jnp.tril(jnp.ones((tm_rank, tm_rank), BF16), -1)

    outs = {k: [] for k in ("da_p", "da_s", "ns_p", "ns_s", "win_p", "win_s")}
    for i in range(depth):
        lam_init = 0.8 - 0.6 * math.exp(-0.3 * i)
        wi = w_in[i]
        g1 = g_mix[i][None, :]
        zf, zb = _proj_rope(xa, g1, wi[:, :S5].astype(BF16), cos, sa, sb, rmask, cscale)
        wg_cols = wi[:, S5:S6].reshape(d, NSA_HEADS, 3).transpose(0, 2, 1).reshape(d, C_NSA_G)
        gns = _gate_proj(xa, g1, _pad_rows(wg_cols, 1, LANES).astype(BF16))
        gmg = _gate_proj(xa, g1, wi[:, S6:].astype(BF16))
        lam_p = da_lambda[i].astype(F32)
        subln = da_subln[i][None, :].astype(F32)
        wp = jnp.tile(nsa_cpos[i].astype(F32), (1, 1, NSA_KV))
        wcm = nsa_cmp[i].astype(F32)
        zc = jnp.zeros_like(wcm)
        wc = jnp.concatenate([jnp.concatenate([wcm, zc], axis=2), jnp.concatenate([zc, wcm], axis=2)], axis=1)

        zs = zb[npr:].reshape(nb, t4, S5)
        kd0, kd1 = S1 - PROJ_F32_FROM, S3 - PROJ_F32_FROM
        kn0, kn1 = S4 - PROJ_F32_FROM, S5 - PROJ_F32_FROM
        zfs = zf[npr:].reshape(nb, t4, kn1)
        q6 = zs[..., :S1].reshape(nb, t4, DA_KV, DA_REP, 2, DA_DH).transpose(0, 2, 4, 1, 3, 5)
        eye_g = jnp.eye(DA_KV, dtype=BF16)
        eye_c = jnp.eye(2, dtype=BF16)
        qbd = (q6[:, :, :, :, :, None, None, :] * eye_g[None, :, None, None, None, :, None, None]
               * eye_c[None, None, :, None, None, None, :, None]).reshape(nb, DA_KV * 2 * rq, DA_KV * 2 * DA_DH)
        cda = cache_da_kv[i].reshape(cache_da_kv.shape[1], page * 2 * DA_KV, DA_DV)
        new_da = _pad_rows(zfs[..., kd0:kd1], 1, page).reshape(nb, page * 2 * DA_KV, DA_DV)
        oda_s = _da_decode(page_table, qbd, cda, new_da, lam_p, subln, lam_init)
        oda_s = oda_s.reshape(nb, DA_KV, t4, DA_REP, DA_DV).transpose(0, 2, 1, 3, 4).reshape(nsm, DA_HEADS * DA_DV)

        qns = zs[..., S3:S4].reshape(nb, t4, NSA_KV, NSA_REP, NSA_DH).transpose(0, 2, 1, 3, 4)
        eye_n = jnp.eye(NSA_KV, dtype=BF16)
        qnb = (qns[:, :, :, :, None, :] * eye_n[None, :, None, None, :, None]
               ).reshape(nb, NSA_KV * t4 * NSA_REP, LANES)
        cns_t = cache_nsa_kv[i].transpose(0, 2, 3, 4, 1)
        cwin_t = cache_win_kv[i].transpose(0, 2, 3, 4, 1)
        kw = NSA_KV * NSA_DH
        wt = jnp.tile(nsa_cpos[i].astype(F32).transpose(0, 2, 1)[:, None],
                      (1, NSA_KV, 1, page // BLOCK)).reshape(2 * kw, page)
        ocmp_s, sel_s = _cmp_decode(page_table, qns.reshape(nb, NSA_KV, t4 * NSA_REP, NSA_DH), cns_t, wt,
                                    nsa_cmp[i].astype(F32).transpose(0, 2, 1),
                                    past=past, t4=t4, nc=nc_s, ns=ns_s, width=wid_s)
        ocmp_s = ocmp_s.reshape(nb, NSA_KV, t4, NSA_REP, NSA_DH).transpose(0, 2, 1, 3, 4
                                                                             ).reshape(nsm, NSA_HEADS * NSA_DH)
        selrows = jnp.broadcast_to(sel_s[:, :, :, None, :], (nb, NSA_KV, t4, NSA_REP, wid_s)
                                   ).reshape(nb, NSA_KV * t4 * NSA_REP, wid_s)
        new_t = lambda x: _pad_rows(x.reshape(nb, t4, 2, NSA_KV, NSA_DH), 1, page).transpose(0, 2, 3, 4, 1)
        oslc_s = _slc_decode(page_table, qnb, selrows, emat_s, cns_t, new_t(zfs[..., kn0 + 2 * kw:kn0 + 4 * kw]),
                             past=past, t4=t4)
        owin_s = _win_decode(qnb, cwin_t, new_t(zfs[..., kn0 + 4 * kw:]), past=past, t4=t4)

        def halves(o):
            o = o.reshape(nb, NSA_KV, t4, NSA_REP, NSA_KV, NSA_DH)
            o = jnp.stack([o[:, g, :, :, g] for g in range(NSA_KV)], axis=2)
            return o.reshape(nsm, NSA_HEADS * NSA_DH)

        oslc_s, owin_s = halves(oslc_s), halves(owin_s)

        seeded = lambda rows: jnp.concatenate([jnp.zeros((npr, rows.shape[1]), rows.dtype), rows], axis=0)
        oda = _da_flash(zb, seeded(oda_s.astype(BF16)), b, s, lam_p, subln, lam_init)
        kc, vc = _compress_prompt(zf, wp, wc, b, s)
        kc2 = _cmp_layout(kc, b, nc_p, wid_p)
        kc2 = jnp.concatenate([kc2, kc2], axis=-1)
        vcl = _cmp_layout(vc, b, nc_p, wid_p)
        vz = jnp.zeros_like(vcl)
        vc2 = jnp.stack([jnp.concatenate([vcl, vz], axis=-1), jnp.concatenate([vz, vcl], axis=-1)], axis=2)
        ocmp, sel_p = _cmp_select(zb, kc2, vc2, seeded(ocmp_s), b, s, nc=nc_p, ns=ns_p)
        kvp = zb[:npr, S4:S5].reshape(npr, 6, NSA_KV, NSA_DH)
        dup = lambda k: jnp.concatenate([k, k], axis=-1).reshape(npr, NSA_KV * LANES)
        with_ones = lambda v: jnp.concatenate([v, jnp.ones_like(v)], axis=-1).reshape(npr, NSA_KV * LANES)
        oslc = _nsa_flash(zb, dup(kvp[:, 2]), with_ones(kvp[:, 3]), seeded(oslc_s), b, s, sel_p, emat_p)
        owin = _nsa_flash(zb, dup(kvp[:, 4]), with_ones(kvp[:, 5]), seeded(owin_s), b, s)

        mrg =_merge(oda, ocmp, oslc, owin, gns, gmg, gate_expand,
                     w_da_up[i].astype(BF16), w_nsa_up[i].astype(BF16))
        wr = _pad_rows(jnp.concatenate([w_rg[i], w_re[i]], axis=1).astype(F32), 1, LANES)
        br = _pad_rows(jnp.concatenate([b_rg[i], b_re[i]]).astype(F32)[None, :], 1, LANES)
        x1, hrows, route = _wo_route(mrg, xa, w_o[i].astype(BF16), g_ffn[i][None, :], wr, br)
        rank, sizes = _expert_ranks(route, tri)
        dest, be, nu = _destinations(route, rank, sizes, nbt, shift)
        dest_flat = dest[:, :2].reshape(n_assign)
        ys = _experts(dest_flat, be[:nbt, 0], nu[0, :1], hrows.reshape(n, nchunk, LANES),
                      w_e_gate[i], w_e_up[i], w_e_down[i], EXPERT_ROWS)
        pa = jnp.concatenate([p_prompt[i].reshape(npr, -1), p_sample[i].reshape(nsm, -1)], axis=0)
        xa = _combine(dest_flat, ys.reshape(nbt * EXPERT_ROWS, nchunk, LANES), route, x1, pa,
                      g_ple[i][None, :], w_ple_gate[i].astype(BF16), w_ple[i].astype(BF16),
                      g_final[None, :], i == depth - 1)

        outs["da_p"].append(zf[:npr, kd0:kd1].reshape(b, s, 2, DA_KV, DA_DV))
        outs["da_s"].append(zfs[..., kd0:kd1].reshape(nb, t4, 2, DA_KV, DA_DV))
        outs["ns_p"].append(zf[:npr, kn0:kn0 + 4 * kw].reshape(b, s, 4, NSA_KV, NSA_DH))
        outs["ns_s"].append(zfs[..., kn0:kn0 + 4 * kw].reshape(nb, t4, 4, NSA_KV, NSA_DH))
        wrows = min(WINDOW, s)
        win_p = jnp.stack([zf[bi * s + s - wrows:(bi + 1) * s, kn0 + 4 * kw:] for bi in range(b)])
        outs["win_p"].append(win_p.reshape(b, wrows, 2, NSA_KV, NSA_DH))
        win_s = jnp.concatenate([cache_win_kv[i], zfs[..., kn0 + 4 * kw:].reshape(nb, t4, 2, NSA_KV, NSA_DH)], axis=1)
        outs["win_s"].append(win_s[:, win_s.shape[1] - wb:])

    y_prompt = xa[:npr].reshape(b, s, d)
    y_sample = xa[npr:].reshape(nb, t4, d)
    return (y_prompt, y_sample, jnp.stack(outs["da_p"]), jnp.stack(outs["da_s"]), jnp.stack(outs["ns_p"]),
            jnp.stack(outs["ns_s"]), jnp.stack(outs["win_p"]), jnp.stack(outs["win_s"]))
```

```python
import functools
import math

import jax
import jax.numpy as jnp
from jax import lax
from jax.experimental import pallas as pl
from jax.experimental.pallas import tpu as pltpu

F32 = jnp.float32
BF16 = jnp.bfloat16
I32 = jnp.int32
HIGHEST = lax.Precision.HIGHEST

DA_HEADS, DA_KV, DA_REP, DA_DH = 8, 4, 2, 64
DA_DV = 2 * DA_DH
NSA_HEADS, NSA_KV, NSA_REP, NSA_DH = 16, 2, 8, 64
BLOCK = 64
BLOCK_SHIFT = 6
N_SELECT = 16
WINDOW = 512
FORCE_BONUS = 1000.0
ROPE_THETA = 500000.0
ROPE_HALF = 8
N_GROUPS, EXP_PER_GROUP = 8, 8
N_EXPERTS = N_GROUPS * EXP_PER_GROUP
EPS = 1e-6
NEG = -1e30
LOWEST = -3e38

LANES = 128
VMEM_LIMIT_BYTES = 56 * 1024 * 1024

C_DA_Q = DA_HEADS * 2 * DA_DH
C_DA_K = DA_KV * 2 * DA_DH
C_DA_V = DA_KV * DA_DV
C_NSA_Q = NSA_HEADS * NSA_DH
C_NSA_KV = 6 * NSA_KV * NSA_DH
C_NSA_G = 3 * NSA_HEADS
S1 = C_DA_Q
S2 = S1 + C_DA_K
S3 = S2 + C_DA_V
S4 = S3 + C_NSA_Q
S5 = S4 + C_NSA_KV
S6 = S5 + C_NSA_G

WEIGHT_DMA_SPLIT = 4
EXPERT_ROWS = 128
DECODE_PAGES = 16
PROJ_TILE = 768
PROJ_F32_FROM = 768
FLASH_TILE = 512
FLASH_ROWS = 128
FLASH_KEYS = 256


def _cparams(sem):
    return pltpu.CompilerParams(dimension_semantics=sem, vmem_limit_bytes=VMEM_LIMIT_BYTES)


def _pick_tile(n, target, mult):
    best = None
    for t in range(mult, min(n, target) + 1, mult):
        if n % t == 0:
            best = t
    return n if best is None else best


def _round_up(x, m):
    return (x + m - 1) // m * m


def _nt(a, b):
    return lax.dot_general(a, b, (((1,), (1,)), ((), ())), preferred_element_type=F32)


def _dot(a, b, **kw):
    return jnp.dot(a, b, preferred_element_type=F32, **kw)


def _rms(x, g):
    ms = jnp.mean(x * x, axis=-1, keepdims=True)
    return x * lax.rsqrt(ms + EPS) * g


def _proj_rope_kernel(x_ref, g_ref, w_ref, cos_ref, sa_ref, sb_ref, rm_ref, cs_ref,
                      of_ref, ob_ref, h_scr, *, n_chunks, skip):
    @pl.when(pl.program_id(1) == 0)
    def _():
        h_scr[...] = _rms(x_ref[...], g_ref[...]).astype(BF16)

    j = pl.program_id(1)
    z = _dot(h_scr[...], w_ref[...])
    c, sa, sb = cos_ref[...], sa_ref[...], sb_ref[...]
    outs = []
    for k in range(n_chunks):
        sl = slice(k * LANES, (k + 1) * LANES)
        zc = z[:, sl]
        rot = zc * c + pltpu.roll(zc, LANES - ROPE_HALF, 1) * sa + pltpu.roll(zc, ROPE_HALF, 1) * sb
        out = jnp.where(rm_ref[:, sl] > 0.5, rot, zc) * cs_ref[:, sl]
        ob_ref[:, sl] = out.astype(BF16)
        outs.append(out)

    @pl.when(j >= skip)
    def _():
        for k, out in enumerate(outs):
            of_ref[:, k * LANES:(k + 1) * LANES] = out


def _proj_rope(xa, g, w, cos, sa, sb, rmask, cscale):
    n, d = xa.shape
    ca = w.shape[1]
    tm = _pick_tile(n, 640, 16)
    tn = PROJ_TILE
    skip = PROJ_F32_FROM // tn
    row = lambda i, j: (i, 0)
    col = lambda i, j: (0, j)
    fcol = lambda i, j: (i, jnp.maximum(j - skip, 0))
    return pl.pallas_call(
        functools.partial(_proj_rope_kernel, n_chunks=tn // LANES, skip=skip),
        out_shape=(jax.ShapeDtypeStruct((n, ca - PROJ_F32_FROM), F32), jax.ShapeDtypeStruct((n, ca), BF16)),
        grid=(n // tm, ca // tn),
        in_specs=[pl.BlockSpec((tm, d), row), pl.BlockSpec((1, d), lambda i, j: (0, 0)),
                  pl.BlockSpec((d, tn), col),
                  pl.BlockSpec((tm, LANES), row), pl.BlockSpec((tm, LANES), row),
                  pl.BlockSpec((tm, LANES), row),
                  pl.BlockSpec((1, tn), col), pl.BlockSpec((1, tn), col)],
        out_specs=(pl.BlockSpec((tm, tn), fcol),
                   pl.BlockSpec((tm, tn), lambda i, j: (i, j))),
        scratch_shapes=[pltpu.VMEM((tm, d), BF16)],
        compiler_params=_cparams(("parallel", "arbitrary")),
        name="proj_rope",
    )(xa, g, w, cos, sa, sb, rmask, cscale)


def _gate_kernel(x_ref, g_ref, w_ref, o_ref, h_scr):
    @pl.when(pl.program_id(1) == 0)
    def _():
        h_scr[...] = _rms(x_ref[...], g_ref[...]).astype(BF16)

    o_ref[...] = jax.nn.sigmoid(_dot(h_scr[...], w_ref[...]))


def _gate_proj(xa, g, w):
    n, d = xa.shape
    ca = w.shape[1]
    tm = _pick_tile(n, 640, 16)
    tn = _pick_tile(ca, 1024, LANES)
    return pl.pallas_call(
        _gate_kernel,
        out_shape=jax.ShapeDtypeStruct((n, ca), F32),
        grid=(n // tm, ca // tn),
        in_specs=[pl.BlockSpec((tm, d), lambda i, j: (i, 0)), pl.BlockSpec((1, d), lambda i, j: (0, 0)),
                  pl.BlockSpec((d, tn), lambda i, j: (0, j))],
        out_specs=pl.BlockSpec((tm, tn), lambda i, j: (i, j)),
        scratch_shapes=[pltpu.VMEM((tm, d), BF16)],
        compiler_params=_cparams(("parallel", "arbitrary")),
        name="gate_proj",
    )(xa, g, w)


def _softmax_block(s, mask, v, m_old, l_old, acc_old, v_is_transposed=False):
    if mask is not None:
        s = jnp.where(mask, s, NEG)
    m_new = jnp.maximum(m_old, jnp.max(s, axis=-1, keepdims=True))
    p = jnp.exp(s - m_new)
    if mask is not None:
        p = jnp.where(mask, p, 0.0)
    alpha = jnp.exp(m_old - m_new)
    l_new = alpha * l_old + jnp.sum(p, axis=-1, keepdims=True)
    pv = _nt(p.astype(v.dtype), v) if v_is_transposed else _dot(p.astype(v.dtype), v)
    return m_new, l_new, alpha * acc_old + pv


def _flash_update(s, mask, v, m_ref, l_ref, acc_ref, idx, v_is_transposed=False):
    m, l, acc = _softmax_block(s, mask, v, m_ref[idx], l_ref[idx], acc_ref[idx], v_is_transposed)
    m_ref[idx] = m
    l_ref[idx] = l
    acc_ref[idx] = acc


def _flash_init(m_ref, l_ref, acc_ref):
    m_ref[...] = jnp.full(m_ref.shape, NEG, F32)
    l_ref[...] = jnp.zeros(l_ref.shape, F32)
    acc_ref[...] = jnp.zeros(acc_ref.shape, F32)


def _da_lambda(lam_ref, lam_init):
    a = jnp.sum(lam_ref[0:1, :] * lam_ref[1:2, :], axis=-1, keepdims=True)
    b = jnp.sum(lam_ref[2:3, :] * lam_ref[3:4, :], axis=-1, keepdims=True)
    return jnp.exp(a) - jnp.exp(b) + lam_init


def _da_finish(a0, a1, lam, sub, lam_init):
    o = a0 - lam * a1
    return _rms(o, sub) * (1.0 - lam_init)


def _flash_rows(s, bias, v1, m_ref, acc_ref, idx):
    if bias is not None:
        s = s + bias
    m_old = m_ref[idx]
    m_new = jnp.maximum(m_old, jnp.max(s, axis=-1, keepdims=True))
    p = jnp.exp(s - jnp.tile(m_new, (1, s.shape[1] // LANES)))
    alpha = jnp.exp(m_old - m_new)
    acc_ref[idx] = jnp.tile(alpha, (1, v1.shape[1] // LANES)) * acc_ref[idx] + _dot(p.astype(v1.dtype), v1)
    m_ref[idx] = m_new


def _tri_tables(nq):
    qi = [q for q in range(nq) for _ in range(q + 1)]
    ki = [k for q in range(nq) for k in range(q + 1)]
    return jnp.asarray(qi, I32), jnp.asarray(ki, I32)


def _da_flash_kernel(qi_ref, ki_ref, lam_ref, sub_ref, q_ref, k_ref, v_ref, base_ref, o_ref, m_scr, acc_scr,
                     *, tq, rb, kb, lam_init):
    del base_ref
    t = pl.program_id(2)
    qi = qi_ref[t]
    ki = ki_ref[t]

    @pl.when(ki == 0)
    def _():
        m_scr[...] = jnp.full(m_scr.shape, NEG, F32)
        acc_scr[...] = jnp.zeros(acc_scr.shape, F32)

    upper = lax.broadcasted_iota(I32, (1, LANES), 1) >= DA_DH

    def step(diag):
        k = k_ref[...]
        v1 = jnp.concatenate([v_ref[...], jnp.ones((tq, DA_DV), BF16)], axis=1)
        for r in range(DA_REP):
            for r0 in range(0, tq, rb):
                nk = r0 + rb if diag else tq
                q = q_ref[pl.ds(r0, rb), r * DA_DV:(r + 1) * DA_DV]
                qc = [jnp.where(upper if c == 1 else jnp.logical_not(upper), q, jnp.zeros_like(q))
                      for c in range(2)]
                for c0 in range(0, nk, kb):
                    c1 = min(c0 + kb, nk)
                    bias = None
                    if diag and c1 > r0:
                        row = r0 - c0 + lax.broadcasted_iota(I32, (rb, c1 - c0), 0)
                        bias = jnp.where(lax.broadcasted_iota(I32, (rb, c1 - c0), 1) <= row, 0.0, NEG)
                    for c in range(2):
                        _flash_rows(_nt(qc[c], k[c0:c1]), bias, v1[c0:c1], m_scr, acc_scr,
                                    (c, pl.ds(r * tq + r0, rb)))

    @pl.when(ki < qi)
    def _():
        step(False)

    @pl.when(ki == qi)
    def _():
        step(True)
        lam = _da_lambda(lam_ref, lam_init)
        a0 = acc_scr[0, :, :DA_DV] / acc_scr[0, :, DA_DV:]
        a1 = acc_scr[1, :, :DA_DV] / acc_scr[1, :, DA_DV:]
        o = _da_finish(a0, a1, lam, sub_ref[...], lam_init)
        for r in range(DA_REP):
            o_ref[:, r * DA_DV:(r + 1) * DA_DV] = o[r * tq:(r + 1) * tq].astype(o_ref.dtype)


def _da_flash(zb, base, b, s, lam_p, subln, lam_init):
    tq = _pick_tile(s, FLASH_TILE, 16)
    rb = _pick_tile(tq, FLASH_ROWS, 16)
    kb = _pick_tile(tq, FLASH_KEYS, LANES)
    nq = s // tq
    qtab, ktab = _tri_tables(nq)
    kcol, vcol = S1 // DA_DV, S2 // DA_DV
    grid_spec = pltpu.PrefetchScalarGridSpec(
        num_scalar_prefetch=2, grid=(b, DA_KV, qtab.shape[0]),
        in_specs=[pl.BlockSpec((4, DA_DH), lambda bi, g, t, qt, kt: (0, 0)),
                  pl.BlockSpec((1, DA_DV), lambda bi, g, t, qt, kt: (0, 0)),
                  pl.BlockSpec((tq, DA_REP * DA_DV), lambda bi, g, t, qt, kt: (bi * nq + qt[t], g)),
                  pl.BlockSpec((tq, DA_DV), lambda bi, g, t, qt, kt: (bi * nq + kt[t], kcol + g)),
                  pl.BlockSpec((tq, DA_DV), lambda bi, g, t, qt, kt: (bi * nq + kt[t], vcol + g)),
                  pl.BlockSpec(memory_space=pl.ANY)],
        out_specs=pl.BlockSpec((tq, DA_REP * DA_DV), lambda bi, g, t, qt, kt: (bi * nq + qt[t], g)),
        scratch_shapes=[pltpu.VMEM((2, DA_REP * tq, LANES), F32),
                        pltpu.VMEM((2, DA_REP * tq, 2 * DA_DV), F32)])
    return pl.pallas_call(
        functools.partial(_da_flash_kernel, tq=tq, rb=rb, kb=kb, lam_init=lam_init),
        out_shape=jax.ShapeDtypeStruct(base.shape, base.dtype),
        grid_spec=grid_spec,
        input_output_aliases={7: 0},
        compiler_params=_cparams(("parallel", "parallel", "arbitrary")),
        name="da_flash",
    )(qtab, ktab, lam_p, subln, zb, zb, zb, base)


def _page_steps(npages):
    pg = _pick_tile(npages, DECODE_PAGES, 1)
    return pg, npages // pg


def _page_specs(block, npages, pg, col_block):
    nd = len(block) - 1

    def spec(u):
        def index(b, p, pt):
            idx = [pt[b, jnp.minimum(p * pg + u, npages - 1)]] + [0] * nd
            if col_block is not None:
                idx[1] = col_block
            return tuple(idx)
        return pl.BlockSpec(block, index)

    return [spec(u) for u in range(pg)]


def _da_decode_kernel(pt_ref, lam_ref, sub_ref, q_ref, *refs, nsteps, pg, page, rq, lam_init):
    del pt_ref
    c_refs = refs[:pg]
    n_ref, o_ref, m_scr, l_scr, acc_scr = refs[pg:]
    p = pl.program_id(1)
    per = 2 * DA_KV

    @pl.when(p == 0)
    def _():
        _flash_init(m_scr, l_scr, acc_scr)

    q = q_ref[0]
    rows = q.shape[0]

    def heads(ref, first):
        return jnp.concatenate([ref[0, pl.ds(first + g, page, stride=per), :] for g in range(DA_KV)],
                               axis=1).astype(BF16)

    def step(blocks, mask):
        k = jnp.concatenate([heads(r, 0) for r in blocks], axis=0)
        v = jnp.concatenate([heads(r, DA_KV) for r in blocks], axis=0)
        _flash_update(_nt(q, k), mask, v, m_scr, l_scr, acc_scr, 0)

    @pl.when(p < nsteps)
    def _():
        step(c_refs, None)

    @pl.when(p == nsteps)
    def _():
        row = lax.broadcasted_iota(I32, (rows, page), 0)
        col = lax.broadcasted_iota(I32, (rows, page), 1)
        step([n_ref], col <= (row % rq) // DA_REP)
        lam = _da_lambda(lam_ref, lam_init)
        a = acc_scr[0] / l_scr[0]
        for g in range(DA_KV):
            cols = slice(g * DA_DV, (g + 1) * DA_DV)
            a0 = a[g * 2 * rq:g * 2 * rq + rq, cols]
            a1 = a[g * 2 * rq + rq:(g + 1) * 2 * rq, cols]
            o_ref[0, g] = _da_finish(a0, a1, lam, sub_ref[...], lam_init)


def _da_decode(page_table, qbd, cache, newkv, lam_p, subln, lam_init):
    nb, rows, ck = qbd.shape
    npages = page_table.shape[1]
    prow = cache.shape[1]
    page = prow // (2 * DA_KV)
    rq = rows // (DA_KV * 2)
    pg, nsteps = _page_steps(npages)
    grid_spec = pltpu.PrefetchScalarGridSpec(
        num_scalar_prefetch=1, grid=(nb, nsteps + 1),
        in_specs=[pl.BlockSpec((4, DA_DH), lambda b, p, pt: (0, 0)),
                  pl.BlockSpec((1, DA_DV), lambda b, p, pt: (0, 0)),
                  pl.BlockSpec((1, rows, ck), lambda b, p, pt: (b, 0, 0))]
        + _page_specs((1, prow, DA_DV), npages, pg, None)
        + [pl.BlockSpec((1, prow, DA_DV), lambda b, p, pt: (b, 0, 0))],
        out_specs=pl.BlockSpec((1, DA_KV, rq, DA_DV), lambda b, p, pt: (b, 0, 0, 0)),
        scratch_shapes=[pltpu.VMEM((1, rows, 1), F32), pltpu.VMEM((1, rows, 1), F32),
                        pltpu.VMEM((1, rows, DA_KV * DA_DV), F32)])
    return pl.pallas_call(
        functools.partial(_da_decode_kernel, nsteps=nsteps, pg=pg, page=page, rq=rq, lam_init=lam_init),
        out_shape=jax.ShapeDtypeStruct((nb, DA_KV, rq, DA_DV), F32),
        grid_spec=grid_spec,
        compiler_params=_cparams(("parallel", "arbitrary")),
        name="da_decode",
    )(page_table, lam_p, subln, qbd, *([cache] * pg), newkv)


def _compress_kernel(k_ref, v_ref, wp_ref, wc_ref, ko_ref, vo_ref):
    for c, (src, dst) in enumerate(((k_ref, ko_ref), (v_ref, vo_ref))):
        x = src[...]
        t = x.shape[0]
        y = jnp.sum(x.reshape(t // BLOCK, BLOCK, LANES) * wp_ref[c][None], axis=1)
        dst[...] = _dot(y, wc_ref[c], precision=HIGHEST)


def _compress_prompt(zf, wp, wc, b, s):
    nc = s // BLOCK
    kc0 = (S4 - PROJ_F32_FROM) // LANES
    out = jax.ShapeDtypeStruct((b * nc, LANES), F32)
    return pl.pallas_call(
        _compress_kernel,
        out_shape=(out, out),
        grid=(b,),
        in_specs=[pl.BlockSpec((s, LANES), lambda i: (i, kc0)), pl.BlockSpec((s, LANES), lambda i: (i, kc0 + 1)),
                  pl.BlockSpec((2, BLOCK, LANES), lambda i: (0, 0, 0)),
                  pl.BlockSpec((2, LANES, LANES), lambda i: (0, 0, 0))],
        out_specs=(pl.BlockSpec((nc, LANES), lambda i: (i, 0)), pl.BlockSpec((nc, LANES), lambda i: (i, 0))),
        compiler_params=_cparams(("parallel",)),
        name="nsa_compress",
    )(zf, zf, wp, wc)


def _masked_probs(s, ok):
    s = jnp.where(ok, s, NEG)
    e = jnp.where(ok, jnp.exp(s - jnp.max(s, axis=-1, keepdims=True)), 0.0)
    den = jnp.sum(e, axis=-1, keepdims=True)
    return e / jnp.where(den > 0.0, den, 1.0)


def _select_blocks(imp, pos, ns):
    width = imp.shape[1]
    blk = lax.broadcasted_iota(I32, (1, width), 1)
    cur = lax.shift_right_logical(pos, BLOCK_SHIFT)
    forced = (blk == 0) | (blk == cur) | (blk == cur - 1)
    score = jnp.where(forced, imp + FORCE_BONUS, imp)
    score = jnp.where(blk <= cur, score, -1.0)
    score = jnp.where(blk < ns, score, 0.5 * LOWEST)
    sel = jnp.zeros(imp.shape, F32)
    for _ in range(N_SELECT):
        mx = jnp.max(score, axis=-1, keepdims=True)
        idx = jnp.min(jnp.where(score == mx, blk, width), axis=-1, keepdims=True)
        hit = blk == idx
        sel = jnp.where(hit, 1.0, sel)
        score = jnp.where(hit, LOWEST, score)
    return sel


def _half_mask(half):
    upper = lax.broadcasted_iota(I32, (1, LANES), 1) >= NSA_DH
    return upper if half == 1 else jnp.logical_not(upper)


def _head_q(q_ref, rows, r):
    q = q_ref[rows, (r // 2) * LANES:(r // 2 + 1) * LANES]
    return jnp.where(_half_mask(r % 2), q, jnp.zeros_like(q))


def _cmp_select_kernel(q_ref, kc_ref, vc_ref, base_ref, o_ref, sel_ref, *, tq, nc, ns, width):
    del base_ref
    qi = pl.program_id(1)
    pos = qi * tq + lax.broadcasted_iota(I32, (tq, 1), 0)
    blk = lax.broadcasted_iota(I32, (1, width), 1)
    cmp_ok = (blk < nc) & ((blk + 1) * BLOCK - 1 <= pos)
    hw = NSA_REP * NSA_DH
    for g in range(NSA_KV):
        kc = kc_ref[0, g]
        imp = jnp.zeros((tq, width), F32)
        for j in range(NSA_REP // 2):
            out = None
            for a in range(2):
                q = q_ref[:, g * hw + j * LANES:g * hw + (j + 1) * LANES]
                q = jnp.where(_half_mask(a), q, jnp.zeros_like(q))
                p = _masked_probs(_nt(q, kc), cmp_ok)
                imp = imp + p
                term = _dot(p.astype(BF16), vc_ref[0, g, a])
                out = term if out is None else out + term
            o_ref[:, g * hw + j * LANES:g * hw + (j + 1) * LANES] = out
        sel_ref[0, g] = _select_blocks(imp, pos, ns).astype(sel_ref.dtype)


def _cmp_decode_kernel(pt_ref, q_ref, wt_ref, wct_ref, *refs, nsteps, pg, page, past, t4, nc, width):
    del pt_ref
    c_refs = refs[:pg]
    o_ref, imp_ref, acc = refs[pg:]
    p = pl.program_id(1)
    kw = NSA_KV * NSA_DH
    cw = acc.shape[1]

    @pl.when(p == 0)
    def _():
        acc[...] = jnp.zeros(acc.shape, F32)

    z = jnp.concatenate([c_refs[u][0].reshape(2 * kw, page) * wt_ref[...] for u in range(pg)], axis=1)
    tokblk = lax.shift_right_logical(lax.broadcasted_iota(I32, (pg * page, cw), 0), BLOCK_SHIFT)
    lane = lax.broadcasted_iota(I32, (pg * page, cw), 1)
    place = jnp.where(lane == p * (pg * page // BLOCK) + tokblk, 1.0, 0.0).astype(BF16)
    total = None
    for _ in range(3):
        piece = z.astype(BF16)
        z = z - piece.astype(F32)
        term = _dot(piece, place)
        total = term if total is None else total + term
    acc[...] += total

    @pl.when(p == nsteps - 1)
    def _():
        y = acc[...]
        rows = t4 * NSA_REP
        pos = past + lax.broadcasted_iota(I32, (rows, 1), 0) // NSA_REP
        blk = lax.broadcasted_iota(I32, (1, cw), 1)
        cmp_ok = (blk < nc) & ((blk + 1) * BLOCK - 1 <= pos)
        for g in range(NSA_KV):
            kct = _dot(wct_ref[0], y[g * NSA_DH:(g + 1) * NSA_DH], precision=HIGHEST)
            vct = _dot(wct_ref[1], y[kw + g * NSA_DH:kw + (g + 1) * NSA_DH], precision=HIGHEST)
            pr = _masked_probs(_dot(q_ref[0, g], kct.astype(BF16)), cmp_ok)
            o_ref[0, g] = _nt(pr.astype(BF16), vct.astype(BF16))
            imp = jnp.sum(pr.reshape(t4, NSA_REP, cw), axis=1)
            if width > cw:
                imp = jnp.concatenate([imp, jnp.zeros((t4, width - cw), F32)], axis=1)
            imp_ref[0, g] = imp


def _select_decode_kernel(imp_ref, sel_ref, *, past, t4, ns):
    rows = imp_ref.shape[0]
    pos = past + lax.broadcasted_iota(I32, (rows, 1), 0) % t4
    sel_ref[...] = _select_blocks(imp_ref[...], pos, ns).astype(sel_ref.dtype)


def _cmp_decode(page_table, qst, cache_t, wt, wct, *, past, t4, nc, ns, width):
    nb, npages = page_table.shape
    page = cache_t.shape[4]
    pg, nsteps = _page_steps(npages)
    rows = t4 * NSA_REP
    cw = _round_up(nc, LANES)
    kw = NSA_KV * NSA_DH
    grid_spec = pltpu.PrefetchScalarGridSpec(
        num_scalar_prefetch=1, grid=(nb, nsteps),
        in_specs=[pl.BlockSpec((1, NSA_KV, rows, NSA_DH), lambda b, p, pt: (b, 0, 0, 0)),
                  pl.BlockSpec((2 * kw, page), lambda b, p, pt: (0, 0)),
                  pl.BlockSpec((2, NSA_DH, NSA_DH), lambda b, p, pt: (0, 0, 0))]
        + _page_specs((1, 2, NSA_KV, NSA_DH, page), npages, pg, 0),
        out_specs=(pl.BlockSpec((1, NSA_KV, rows, NSA_DH), lambda b, p, pt: (b, 0, 0, 0)),
                   pl.BlockSpec((1, NSA_KV, t4, width), lambda b, p, pt: (b, 0, 0, 0))),
        scratch_shapes=[pltpu.VMEM((2 * kw, cw), F32)])
    ocmp, imp = pl.pallas_call(
        functools.partial(_cmp_decode_kernel, nsteps=nsteps, pg=pg, page=page, past=past, t4=t4,
                          nc=nc, width=width),
        out_shape=(jax.ShapeDtypeStruct((nb, NSA_KV, rows, NSA_DH), F32),
                   jax.ShapeDtypeStruct((nb, NSA_KV, t4, width), F32)),
        grid_spec=grid_spec,
        compiler_params=_cparams(("parallel", "arbitrary")),
        name="nsa_cmp_decode",
    )(page_table, qst, wt, wct, *([cache_t] * pg))
    nrow = nb * NSA_KV * t4
    sel = pl.pallas_call(
        functools.partial(_select_decode_kernel, past=past, t4=t4, ns=ns),
        out_shape=jax.ShapeDtypeStruct((nrow, width), BF16),
        grid=(1,),
        in_specs=[pl.BlockSpec((nrow, width), lambda i: (0, 0))],
        out_specs=pl.BlockSpec((nrow, width), lambda i: (0, 0)),
        compiler_params=_cparams(("arbitrary",)),
        name="nsa_select_decode",
    )(imp.reshape(nrow, width))
    return ocmp, sel.reshape(nb, NSA_KV, t4, width)


def _cmp_select(zb, kc2, vc2, base, b, s, *, nc, ns):
    width = kc2.shape[2]
    tq = _pick_tile(s, 256, 16)
    nq = s // tq
    qcol = S3 // C_NSA_Q
    return pl.pallas_call(
        functools.partial(_cmp_select_kernel, tq=tq, nc=nc, ns=ns, width=width),
        out_shape=(jax.ShapeDtypeStruct(base.shape, base.dtype),
                   jax.ShapeDtypeStruct((b, NSA_KV, s, width), BF16)),
        grid=(b, nq),
        in_specs=[pl.BlockSpec((tq, C_NSA_Q), lambda bi, qi: (bi * nq + qi, qcol)),
                  pl.BlockSpec((1, NSA_KV, width, LANES), lambda bi, qi: (bi, 0, 0, 0)),
                  pl.BlockSpec((1, NSA_KV, 2, width, LANES), lambda bi, qi: (bi, 0, 0, 0, 0)),
                  pl.BlockSpec(memory_space=pl.ANY)],
        out_specs=(pl.BlockSpec((tq, C_NSA_Q), lambda bi, qi: (bi * nq + qi, 0)),
                   pl.BlockSpec((1, NSA_KV, tq, width), lambda bi, qi: (bi, 0, qi, 0))),
        input_output_aliases={3: 0},
        compiler_params=_cparams(("parallel", "parallel")),
        name="nsa_cmp_select",
    )(zb, kc2, vc2, base)


def _nsa_flash_kernel(qi_ref, ki_ref, *refs, mode, tq, rb, kb, nsteps, band):
    if mode == "slc":
        q_ref, k_ref, v_ref, sel_ref, e_ref, _, o_ref, m_scr, acc_scr = refs
    else:
        q_ref, k_ref, v_ref, _, o_ref, m_scr, acc_scr = refs
    t = pl.program_id(2)
    qi = qi_ref[t]
    ki = ki_ref[t]
    first = (t == 0) | (qi_ref[jnp.maximum(t - 1, 0)] != qi)
    last = (t == nsteps - 1) | (qi_ref[jnp.minimum(t + 1, nsteps - 1)] != qi)

    @pl.when(first)
    def _():
        m_scr[...] = jnp.full(m_scr.shape, NEG, F32)
        acc_scr[...] = jnp.zeros(acc_scr.shape, F32)

    if mode == "slc":
        picked = _dot(sel_ref[0, 0], e_ref[...])
    k = k_ref[...]
    v1 = v_ref[...]
    diag = qi == ki
    for r0 in range(0, tq, rb):
        rows = pl.ds(r0, rb)
        nk = _round_up(r0 + rb, LANES)

        def update(ncols, off_diag=False, r0=r0, rows=rows):
            for c0 in range(0, ncols, kb):
                c1 = min(c0 + kb, ncols)
                if off_diag and band == 1 and tq + r0 - (c1 - 1) >= WINDOW:
                    continue
                d = ((qi - ki) * tq + r0 - c0 + lax.broadcasted_iota(I32, (rb, c1 - c0), 0)
                     - lax.broadcasted_iota(I32, (rb, c1 - c0), 1))
                if mode == "slc":
                    visible = (picked[r0:r0 + rb, c0:c1] > 0.5) & (d >= 0)
                else:
                    visible = (d >= 0) & (d < WINDOW)
                bias = jnp.where(visible, 0.0, NEG)
                for h in range(NSA_REP):
                    s = _nt(_head_q(q_ref, rows, h), k[c0:c1])
                    _flash_rows(s, bias, v1[c0:c1], m_scr, acc_scr, (h, rows))

        if nk < tq or band == 1:
            pl.when(diag)(functools.partial(update, nk))
            pl.when(jnp.logical_not(diag))(functools.partial(update, tq, True))
        else:
            update(tq)

    @pl.when(last)
    def _():
        low = _half_mask(0)
        for j in range(NSA_REP // 2):
            a, bq = acc_scr[2 * j], acc_scr[2 * j + 1]
            num = jnp.where(low, a, pltpu.roll(bq, NSA_DH, 1))
            den = jnp.where(low, pltpu.roll(a, NSA_DH, 1), bq)
            o_ref[:, j * LANES:(j + 1) * LANES] = num / den


def _nsa_flash(zb, k2, v1, base, b, s, sel=None, emat=None):
    tq = _pick_tile(s, FLASH_TILE, 16)
    rb = _pick_tile(tq, FLASH_ROWS, 16)
    kb = _pick_tile(tq, FLASH_KEYS, LANES)
    nq = s // tq
    back = 0
    if sel is not None:
        mode = "slc"
        qtab, ktab = _tri_tables(nq)
    else:
        mode = "win"
        back = -(-WINDOW // tq)
        pairs = [(q, kk) for q in range(nq) for kk in range(max(q - back, 0), q + 1)]
        qtab = jnp.asarray([p[0] for p in pairs], I32)
        ktab = jnp.asarray([p[1] for p in pairs], I32)
    nsteps = qtab.shape[0]
    gw = NSA_REP * NSA_DH
    qcol = S3 // gw
    in_specs = [pl.BlockSpec((tq, gw), lambda bi, g, t, qt, kt: (bi * nq + qt[t], qcol + g)),
                pl.BlockSpec((tq, LANES), lambda bi, g, t, qt, kt: (bi * nq + kt[t], g)),
                pl.BlockSpec((tq, LANES), lambda bi, g, t, qt, kt: (bi * nq + kt[t], g))]
    args = [zb, k2, v1]
    if sel is not None:
        width = sel.shape[3]
        in_specs += [pl.BlockSpec((1, 1, tq, width), lambda bi, g, t, qt, kt: (bi, g, qt[t], 0)),
                     pl.BlockSpec((width, tq), lambda bi, g, t, qt, kt: (0, kt[t]))]
        args += [sel, emat]
    in_specs.append(pl.BlockSpec(memory_space=pl.ANY))
    args.append(base)
    grid_spec = pltpu.PrefetchScalarGridSpec(
        num_scalar_prefetch=2, grid=(b, NSA_KV, nsteps),
        in_specs=in_specs,
        out_specs=pl.BlockSpec((tq, gw), lambda bi, g, t, qt, kt: (bi * nq + qt[t], g)),
        scratch_shapes=[pltpu.VMEM((NSA_REP, tq, LANES), F32), pltpu.VMEM((NSA_REP, tq, LANES), F32)])
    return pl.pallas_call(
        functools.partial(_nsa_flash_kernel, mode=mode, tq=tq, rb=rb, kb=kb, nsteps=nsteps, band=back),
        out_shape=jax.ShapeDtypeStruct(base.shape, base.dtype),
        grid_spec=grid_spec,
        input_output_aliases={1 + len(args): 0},
        compiler_params=_cparams(("parallel", "parallel", "arbitrary")),
        name="nsa_" + mode,
    )(qtab, ktab, *args)


def _row_token(rows, cols, t4):
    row = lax.broadcasted_iota(I32, (rows, cols), 0)
    return (row // NSA_REP) % t4


def _kv_t(blocks):
    kw = NSA_KV * NSA_DH
    kt = [r[0, 0].reshape(kw, r.shape[4]) for r in blocks]
    vt = [r[0, 1].reshape(kw, r.shape[4]) for r in blocks]
    cat = lambda xs: (xs[0] if len(xs) == 1 else jnp.concatenate(xs, axis=1)).astype(BF16)
    return cat(kt), cat(vt)


def _slc_decode_kernel(pt_ref, q_ref, sel_ref, e_ref, *refs, nsteps, pg, page, past, t4):
    del pt_ref
    c_refs = refs[:pg]
    n_ref, o_ref, m_scr, l_scr, acc_scr = refs[pg:]
    p = pl.program_id(1)

    @pl.when(p == 0)
    def _():
        _flash_init(m_scr, l_scr, acc_scr)

    q = q_ref[0]
    rows = q.shape[0]

    def step(blocks):
        kt, vt = _kv_t(blocks)
        nk = kt.shape[1]
        kpos = p * (pg * page) + lax.broadcasted_iota(I32, (rows, nk), 1)
        picked = _dot(sel_ref[0], e_ref[:, :nk]) > 0.5
        mask = picked & (kpos <= past + _row_token(rows, nk, t4))
        _flash_update(_dot(q, kt), mask, vt, m_scr, l_scr, acc_scr, 0, v_is_transposed=True)

    @pl.when(p < nsteps)
    def _():
        step(c_refs)

    @pl.when(p == nsteps)
    def _():
        step([n_ref])
        o_ref[0] = acc_scr[0] / l_scr[0]


def _slc_decode(page_table, qnb, selrows, emat, cache_t, new_t, *, past, t4):
    nb, rows, _ = qnb.shape
    npages = page_table.shape[1]
    page = cache_t.shape[4]
    width = selrows.shape[2]
    pg, nsteps = _page_steps(npages)
    blk = (1, 2, NSA_KV, NSA_DH, page)
    grid_spec = pltpu.PrefetchScalarGridSpec(
        num_scalar_prefetch=1, grid=(nb, nsteps + 1),
        in_specs=[pl.BlockSpec((1, rows, LANES), lambda b, p, pt: (b, 0, 0)),
                  pl.BlockSpec((1, rows, width), lambda b, p, pt: (b, 0, 0)),
                  pl.BlockSpec((width, pg * page), lambda b, p, pt: (0, p))]
        + _page_specs(blk, npages, pg, 1)
        + [pl.BlockSpec(blk, lambda b, p, pt: (b, 0, 0, 0, 0))],
        out_specs=pl.BlockSpec((1, rows, LANES), lambda b, p, pt: (b, 0, 0)),
        scratch_shapes=[pltpu.VMEM((1, rows, 1), F32), pltpu.VMEM((1, rows, 1), F32),
                        pltpu.VMEM((1, rows, LANES), F32)])
    return pl.pallas_call(
        functools.partial(_slc_decode_kernel, nsteps=nsteps, pg=pg, page=page, past=past, t4=t4),
        out_shape=jax.ShapeDtypeStruct((nb, rows, LANES), F32),
        grid_spec=grid_spec,
        compiler_params=_cparams(("parallel", "arbitrary")),
        name="nsa_slc_decode",
    )(page_table, qnb, selrows, emat, *([cache_t] * pg), new_t)


def _win_decode_kernel(q_ref, c_ref, n_ref, o_ref, *, past, t4):
    q = q_ref[0]
    rows = q.shape[0]
    wb = c_ref.shape[4]
    m = jnp.full((rows, 1), NEG, F32)
    l = jnp.zeros((rows, 1), F32)
    acc = jnp.zeros((rows, LANES), F32)
    for ref, kstart in ((c_ref, past - wb), (n_ref, past)):
        kt, vt = _kv_t([ref])
        nk = kt.shape[1]
        d = past + _row_token(rows, nk, t4) - (kstart + lax.broadcasted_iota(I32, (rows, nk), 1))
        mask = (d >= 0) & (d < WINDOW)
        m, l, acc = _softmax_block(_dot(q, kt), mask, vt, m, l, acc, v_is_transposed=True)
    o_ref[0] = acc / l


def _win_decode(qnb, cwin_t, new_t, *, past, t4):
    nb, rows, _ = qnb.shape
    return pl.pallas_call(
        functools.partial(_win_decode_kernel, past=past, t4=t4),
        out_shape=jax.ShapeDtypeStruct((nb, rows, LANES), F32),
        grid=(nb,),
        in_specs=[pl.BlockSpec((1, rows, LANES), lambda b: (b, 0, 0)),
                  pl.BlockSpec((1,) + cwin_t.shape[1:], lambda b: (b, 0, 0, 0, 0)),
                  pl.BlockSpec((1,) + new_t.shape[1:], lambda b: (b, 0, 0, 0, 0))],
        out_specs=pl.BlockSpec((1, rows, LANES), lambda b: (b, 0, 0)),
        compiler_params=_cparams(("parallel",)),
        name="nsa_win_decode",
    )(qnb, cwin_t, new_t)


def _merge_kernel(oda_ref, oc_ref, os_ref, ow_ref, gn_ref, ga_ref, gb_ref, e_ref, wda_ref, wns_ref, o_ref):
    g = gn_ref[...]
    ghi = g.astype(BF16)
    glo = (g - ghi.astype(F32)).astype(BF16)
    ons = None
    for b, src in enumerate((oc_ref, os_ref, ow_ref)):
        ge = _dot(ghi, e_ref[b]) + _dot(glo, e_ref[b])
        term = ge * src[...]
        ons = term if ons is None else ons + term
    yda = _dot(oda_ref[...], wda_ref[...])
    yns = _dot(ons.astype(BF16), wns_ref[...])
    o_ref[...] = (ga_ref[...] * yda + gb_ref[...] * yns).astype(o_ref.dtype)


def _merge(oda, ocmp, oslc, owin, gns, gmg, emat, wda, wns):
    n, hd = oda.shape
    d = wda.shape[1]
    tm = _pick_tile(n, 320, 16)
    row = lambda i: (i, 0)
    const2 = lambda i: (0, 0)
    return pl.pallas_call(
        _merge_kernel,
        out_shape=jax.ShapeDtypeStruct((n, d), BF16),
        grid=(n // tm,),
        in_specs=[pl.BlockSpec((tm, hd), row), pl.BlockSpec((tm, hd), row), pl.BlockSpec((tm, hd), row),
                  pl.BlockSpec((tm, hd), row), pl.BlockSpec((tm, LANES), row),
                  pl.BlockSpec((tm, d), lambda i: (i, 0)), pl.BlockSpec((tm, d), lambda i: (i, 1)),
                  pl.BlockSpec((3, LANES, hd), lambda i: (0, 0, 0)),
                  pl.BlockSpec((hd, d), const2), pl.BlockSpec((hd, d), const2)],
        out_specs=pl.BlockSpec((tm, d), row),
        compiler_params=_cparams(("parallel",)),
        name="mixer_merge",
    )(oda, ocmp, oslc, owin, gns, gmg, gmg, emat, wda, wns)


def _lane_pick(x, lane, k):
    return jnp.sum(jnp.where(lane == k, x, 0.0), axis=-1, keepdims=True)


def _route(logits):
    lane = lax.broadcasted_iota(I32, logits.shape, 1)
    isg = lane < N_GROUPS
    gmax = jnp.max(jnp.where(isg, logits, LOWEST), axis=-1, keepdims=True)
    gsel = jnp.min(jnp.where(isg & (logits == gmax), lane, LANES), axis=-1, keepdims=True)
    pgrp = 1.0 / jnp.sum(jnp.where(isg, jnp.exp(logits - gmax), 0.0), axis=-1, keepdims=True)
    lo = N_GROUPS + gsel * EXP_PER_GROUP
    ing = (lane >= lo) & (lane < lo + EXP_PER_GROUP)
    v1 = jnp.max(jnp.where(ing, logits, LOWEST), axis=-1, keepdims=True)
    i1 = jnp.min(jnp.where(ing & (logits == v1), lane, LANES), axis=-1, keepdims=True)
    ing2 = ing & (lane != i1)
    v2 = jnp.max(jnp.where(ing2, logits, LOWEST), axis=-1, keepdims=True)
    i2 = jnp.min(jnp.where(ing2 & (logits == v2), lane, LANES), axis=-1, keepdims=True)
    t = jnp.exp(v2 - v1)
    w1 = pgrp / (1.0 + t)
    w2 = pgrp * t / (1.0 + t)
    e1 = (i1 - N_GROUPS).astype(F32)
    e2 = (i2 - N_GROUPS).astype(F32)
    return jnp.where(lane == 0, e1, jnp.where(lane == 1, e2, jnp.where(lane == 2, w1, jnp.where(lane == 3, w2, 0.0))))


def _store_rows(ref, y, rows, nchunk):
    for s in range(nchunk):
        ref[pl.ds(s, rows, stride=nchunk), :] = y[:, s * LANES:(s + 1) * LANES]


def _wo_route_kernel(m_ref, x_ref, wo_ref, gf_ref, wr_ref, br_ref, x1_ref, h_ref, rt_ref, *, nchunk):
    x1 = x_ref[...] + _dot(m_ref[...], wo_ref[...])
    x1_ref[...] = x1
    h = _rms(x1, gf_ref[...])
    _store_rows(h_ref, h, h.shape[0], nchunk)
    hi = h.astype(BF16)
    lo = (h - hi.astype(F32)).astype(BF16)
    w = wr_ref[...]
    w_hi = w.astype(BF16)
    w_lo = (w - w_hi.astype(F32)).astype(BF16)
    logits = _dot(hi, w_hi) + (_dot(hi, w_lo) + _dot(lo, w_hi))
    rt_ref[...] = _route(logits + br_ref[...])


def _wo_route(mrg, xa, wo, gffn, wr, br):
    n, d = xa.shape
    tm = _pick_tile(n, 320, 16)
    nchunk = d // LANES
    row = lambda i: (i, 0)
    const2 = lambda i: (0, 0)
    return pl.pallas_call(
        functools.partial(_wo_route_kernel, nchunk=nchunk),
        out_shape=(jax.ShapeDtypeStruct((n, d), F32), jax.ShapeDtypeStruct((n * nchunk, LANES), F32),
                   jax.ShapeDtypeStruct((n, LANES), F32)),
        grid=(n // tm,),
        in_specs=[pl.BlockSpec((tm, d), row), pl.BlockSpec((tm, d), row), pl.BlockSpec((d, d), const2),
                  pl.BlockSpec((1, d), const2), pl.BlockSpec((d, LANES), const2),
                  pl.BlockSpec((1, LANES), const2)],
        out_specs=(pl.BlockSpec((tm, d), row), pl.BlockSpec((tm * nchunk, LANES), row),
                   pl.BlockSpec((tm, LANES), row)),
        compiler_params=_cparams(("parallel",)),
        name="wo_route",
    )(mrg, xa, wo, gffn, wr, br)


def _onehots(rt):
    lane = lax.broadcasted_iota(I32, rt.shape, 1)
    lane_f = lane.astype(F32)
    oh0 = jnp.where(lane_f == _lane_pick(rt, lane, 0), 1.0, 0.0)
    oh1 = jnp.where(lane_f == _lane_pick(rt, lane, 1), 1.0, 0.0)
    return lane, oh0, oh1


def _rank_kernel(rt_ref, tri_ref, rk_ref, sz_ref, carry):
    @pl.when(pl.program_id(0) == 0)
    def _():
        carry[...] = jnp.zeros(carry.shape, F32)

    lane, oh0, oh1 = _onehots(rt_ref[...])
    oh = oh0 + oh1
    before = _dot(tri_ref[...], oh.astype(BF16)) + carry[0:1, :]
    r0 = jnp.sum(before * oh0, axis=-1, keepdims=True)
    r1 = jnp.sum(before * oh1, axis=-1, keepdims=True)
    rk_ref[...] = jnp.where(lane == 0, r0, jnp.where(lane == 1, r1, 0.0))
    carry[...] = carry[...] + jnp.sum(oh, axis=0, keepdims=True)
    sz_ref[...] = carry[...]


def _expert_ranks(route, tri):
    n = route.shape[0]
    tm = tri.shape[0]
    return pl.pallas_call(
        _rank_kernel,
        out_shape=(jax.ShapeDtypeStruct((n, LANES), F32), jax.ShapeDtypeStruct((8, LANES), F32)),
        grid=(n // tm,),
        in_specs=[pl.BlockSpec((tm, LANES), lambda i: (i, 0)), pl.BlockSpec((tm, tm), lambda i: (0, 0))],
        out_specs=(pl.BlockSpec((tm, LANES), lambda i: (i, 0)), pl.BlockSpec((8, LANES), lambda i: (0, 0))),
        scratch_shapes=[pltpu.VMEM((8, LANES), F32)],
        compiler_params=_cparams(("arbitrary",)),
        name="moe_rank",
    )(route, tri)


def _dest_kernel(rt_ref, rk_ref, sz_ref, dest_ref, be_ref, nu_ref, *, shift):
    lane8 = lax.broadcasted_iota(I32, (8, LANES), 1)
    sizes = sz_ref[...].astype(I32)
    blk = 1 << shift
    padded = jnp.where(lane8 < N_EXPERTS, lax.shift_right_logical(sizes + (blk - 1), shift) * blk, 0)
    ends = padded
    d = 1
    while d < LANES:
        ends = ends + jnp.where(lane8 >= d, pltpu.roll(ends, d, 1), 0)
        d *= 2
    pstart = (ends - padded)[0:1, :].astype(F32)
    rt = rt_ref[...]
    lane, oh0, oh1 = _onehots(rt)
    rk = rk_ref[...]
    d0 = jnp.sum(oh0 * pstart, axis=-1, keepdims=True) + _lane_pick(rk, lane, 0)
    d1 = jnp.sum(oh1 * pstart, axis=-1, keepdims=True) + _lane_pick(rk, lane, 1)
    dest_ref[...] = jnp.where(lane == 0, d0, jnp.where(lane == 1, d1, 0.0)).astype(I32)
    nbt = be_ref.shape[0]
    lane_b = lax.broadcasted_iota(I32, (nbt, LANES), 1)
    first_row = lax.broadcasted_iota(I32, (nbt, LANES), 0) * blk
    cnt = jnp.sum(jnp.where((lane_b < N_EXPERTS) & (ends[0:1, :] <= first_row), 1, 0), axis=-1, keepdims=True)
    be_ref[...] = jnp.broadcast_to(jnp.minimum(cnt, N_EXPERTS - 1), (nbt, LANES))
    total = jnp.sum(jnp.where(lane8 == N_EXPERTS - 1, ends, 0), axis=-1, keepdims=True)
    nu_ref[...] = jnp.broadcast_to(lax.shift_right_logical(total, shift), (8, LANES))


def _destinations(route, rank, sizes, nbt, shift):
    n = route.shape[0]
    tm = _pick_tile(n, 640, 8)
    nbt_p = _round_up(nbt, 8)
    return pl.pallas_call(
        functools.partial(_dest_kernel, shift=shift),
        out_shape=(jax.ShapeDtypeStruct((n, LANES), I32), jax.ShapeDtypeStruct((nbt_p, LANES), I32),
                   jax.ShapeDtypeStruct((8, LANES), I32)),
        grid=(n // tm,),
        in_specs=[pl.BlockSpec((tm, LANES), lambda i: (i, 0)), pl.BlockSpec((tm, LANES), lambda i: (i, 0)),
                  pl.BlockSpec((8, LANES), lambda i: (0, 0))],
        out_specs=(pl.BlockSpec((tm, LANES), lambda i: (i, 0)), pl.BlockSpec((nbt_p, LANES), lambda i: (0, 0)),
                   pl.BlockSpec((8, LANES), lambda i: (0, 0))),
        compiler_params=_cparams(("arbitrary",)),
        name="moe_dest",
    )(route, rank, sizes)


def _expert_kernel(dest_ref, be_ref, nu_ref, h_ref, wg_ref, wu_ref, wd_ref, o_ref,
                   tok, ordinal, xbuf, sem, wgf, wuf, wdf, wsem, wgb, wub, wdb, *, rows, nchunk, n_assign):
    j = pl.program_id(0)
    n_used = nu_ref[0]

    def weight_copies(e, ws):
        copies = []
        for m, (src, dst) in enumerate(((wg_ref, wgf), (wu_ref, wuf), (wd_ref, wdf))):
            band = src.shape[1] // WEIGHT_DMA_SPLIT
            for c in range(WEIGHT_DMA_SPLIT):
                rows_c = pl.ds(c * band, band)
                copies.append(pltpu.make_async_copy(src.at[e, rows_c], dst.at[ws, rows_c], wsem.at[ws, m]))
        return copies

    def start_weights(e, ws):
        for c in weight_copies(e, ws):
            c.start(priority=1)

    def row_copy(tile, slot, r):
        return pltpu.make_async_copy(h_ref.at[tok[tile * rows + r]],
                                     xbuf.at[slot, pl.ds(r * nchunk, nchunk)], sem.at[slot])

    def start_tile(tile, slot):
        def body(r, carry):
            row_copy(tile, slot, r).start(priority=0)
            return carry
        lax.fori_loop(0, rows, body, 0, unroll=8)

    def wait_tile(tile, slot):
        def body(r, carry):
            row_copy(tile, slot, r).wait()
            return carry
        lax.fori_loop(0, rows, body, 0, unroll=8)

    @pl.when(j == 0)
    def _():
        start_weights(be_ref[0], 0)
        ordinal[0] = 0

        def clear(r, carry):
            tok[r] = 0
            return carry
        lax.fori_loop(0, tok.shape[0], clear, 0, unroll=8)

        def place(a, carry):
            tok[dest_ref[a]] = lax.shift_right_logical(a, 1)
            return carry
        lax.fori_loop(0, n_assign, place, 0, unroll=8)
        start_tile(0, 0)

    slot = lax.rem(j, 2)
    active = j < n_used
    expert = be_ref[j]
    changed = (j == 0) | (expert != be_ref[jnp.maximum(j - 1, 0)])

    @pl.when(active & changed)
    def _():
        ws = lax.rem(ordinal[0], 2)
        for c in weight_copies(expert, ws):
            c.wait()
        wgb[...] = wgf[ws].astype(BF16)
        wub[...] = wuf[ws].astype(BF16)
        wdb[...] = wdf[ws].astype(BF16)
        nxt = lax.while_loop(lambda t: (t < n_used) & (be_ref[jnp.minimum(t, n_used - 1)] == expert),
                             lambda t: t + 1, j + 1)

        @pl.when(nxt < n_used)
        def _():
            start_weights(be_ref[nxt], 1 - ws)

        ordinal[0] = ordinal[0] + 1

    @pl.when(active)
    def _():
        @pl.when(j + 1 < n_used)
        def _():
            start_tile(j + 1, 1 - slot)

        wait_tile(j, slot)
        x = jnp.concatenate([xbuf[slot, pl.ds(s, rows, stride=nchunk), :] for s in range(nchunk)],
                            axis=1).astype(BF16)
        a = _dot(x, wgb[...])
        u = _dot(x, wub[...])
        y = _dot((jax.nn.silu(a) * u).astype(BF16), wdb[...])
        _store_rows(o_ref, y, rows, nchunk)

    @pl.when(jnp.logical_not(active))
    def _():
        o_ref[...] = jnp.zeros(o_ref.shape, F32)


def _experts(dest, blk_e, n_used, h3, wg, wu, wd, rows):
    d, de = wg.shape[1], wg.shape[2]
    nchunk = d // LANES
    nbt = blk_e.shape[0]
    grid_spec = pltpu.PrefetchScalarGridSpec(
        num_scalar_prefetch=3, grid=(nbt,),
        in_specs=[pl.BlockSpec(memory_space=pl.ANY)] * 4,
        out_specs=pl.BlockSpec((rows * nchunk, LANES), lambda j, dst, be, nu: (j, 0)),
        scratch_shapes=[pltpu.SMEM((nbt * rows,), I32), pltpu.SMEM((1,), I32),
                        pltpu.VMEM((2, rows * nchunk, LANES), F32), pltpu.SemaphoreType.DMA((2,)),
                        pltpu.VMEM((2, d, de), F32), pltpu.VMEM((2, d, de), F32), pltpu.VMEM((2, de, d), F32),
                        pltpu.SemaphoreType.DMA((2, 3)),
                        pltpu.VMEM((d, de), BF16), pltpu.VMEM((d, de), BF16), pltpu.VMEM((de, d), BF16)])
    return pl.pallas_call(
        functools.partial(_expert_kernel, rows=rows, nchunk=nchunk, n_assign=dest.shape[0]),
        out_shape=jax.ShapeDtypeStruct((nbt * rows * nchunk, LANES), F32),
        grid_spec=grid_spec,
        compiler_params=_cparams(("arbitrary",)),
        name="moe_experts",
    )(dest, blk_e, n_used, h3, wg, wu, wd)


def _combine_kernel(dest_ref, ys_ref, rt_ref, x1_ref, p_ref, gp_ref, wpg_ref, wpl_ref, gfin_ref, o_ref,
                    gbuf, sem, *, tm, nchunk, final):
    i = pl.program_id(0)
    slot = lax.rem(i, 2)

    def copy(tile, sl, j):
        return pltpu.make_async_copy(ys_ref.at[dest_ref[tile * (2 * tm) + j]],
                                     gbuf.at[sl, pl.ds(j * nchunk, nchunk)], sem.at[sl])

    def start_tile(tile, sl):
        def body(j, carry):
            copy(tile, sl, 2 * j).start(priority=0)
            copy(tile, sl, 2 * j + 1).start(priority=1)
            return carry
        lax.fori_loop(0, tm, body, 0, unroll=4)

    @pl.when(i == 0)
    def _():
        start_tile(0, 0)

    @pl.when(i + 1 < pl.num_programs(0))
    def _():
        start_tile(i + 1, 1 - slot)

    def drain(j, carry):
        copy(i, slot, j).wait()
        return carry

    lax.fori_loop(0, 2 * tm, drain, 0, unroll=8)
    rt = rt_ref[...]
    lane = lax.broadcasted_iota(I32, rt.shape, 1)
    y0 = jnp.concatenate([gbuf[slot, pl.ds(s, tm, stride=2 * nchunk), :] for s in range(nchunk)], axis=1)
    y1 = jnp.concatenate([gbuf[slot, pl.ds(nchunk + s, tm, stride=2 * nchunk), :] for s in range(nchunk)], axis=1)
    x2 = x1_ref[...] + (_lane_pick(rt, lane, 2) * y0 + _lane_pick(rt, lane, 3) * y1)
    gate = jax.nn.sigmoid(_dot(_rms(x2, gp_ref[...]).astype(BF16), wpg_ref[...]))
    x3 = x2 + gate * _dot(p_ref[...].astype(BF16), wpl_ref[...])
    o_ref[...] = _rms(x3, gfin_ref[...]) if final else x3


def _combine(dest, ys3, route, x1, pa, gple, wpg, wpl, gfin, final):
    n, d = x1.shape
    nchunk = d // LANES
    tm = _pick_tile(n, 320, 16)
    dp = pa.shape[1]
    row = lambda i, dst: (i, 0)
    const2 = lambda i, dst: (0, 0)
    grid_spec = pltpu.PrefetchScalarGridSpec(
        num_scalar_prefetch=1, grid=(n // tm,),
        in_specs=[pl.BlockSpec(memory_space=pl.ANY),
                  pl.BlockSpec((tm, LANES), row), pl.BlockSpec((tm, d), row), pl.BlockSpec((tm, dp), row),
                  pl.BlockSpec((1, d), const2), pl.BlockSpec((d, d), const2), pl.BlockSpec((dp, d), const2),
                  pl.BlockSpec((1, d), const2)],
        out_specs=pl.BlockSpec((tm, d), row),
        scratch_shapes=[pltpu.VMEM((2, 2 * tm * nchunk, LANES), F32), pltpu.SemaphoreType.DMA((2,))])
    return pl.pallas_call(
        functools.partial(_combine_kernel, tm=tm, nchunk=nchunk, final=final),
        out_shape=jax.ShapeDtypeStruct((n, d), F32),
        grid_spec=grid_spec,
        compiler_params=_cparams(("arbitrary",)),
        name="moe_combine_tail",
    )(dest, ys3, route, x1, pa, gple, wpg, wpl, gfin)


def _rope_tables(pos):
    inv = ROPE_THETA ** (-jnp.arange(ROPE_HALF, dtype=F32) / ROPE_HALF)
    ang = pos.astype(F32)[:, None] * inv[None, :]
    cos, sin = jnp.cos(ang), jnp.sin(ang)
    n = pos.shape[0]
    rest = DA_DH - 2 * ROPE_HALF
    z = jnp.zeros((n, ROPE_HALF), F32)
    zr = jnp.zeros((n, rest), F32)
    c64 = jnp.concatenate([cos, cos, jnp.ones((n, rest), F32)], axis=1)
    a64 = jnp.concatenate([-sin, z, zr], axis=1)
    b64 = jnp.concatenate([z, sin, zr], axis=1)
    rep = LANES // DA_DH
    return jnp.tile(c64, (1, rep)), jnp.tile(a64, (1, rep)), jnp.tile(b64, (1, rep))


def _column_tables():
    ones = lambda n: jnp.ones((n,), F32)
    zeros = lambda n: jnp.zeros((n,), F32)
    gw = NSA_KV * NSA_DH
    rmask = jnp.concatenate([ones(C_DA_Q), ones(C_DA_K), zeros(C_DA_V), ones(C_NSA_Q)]
                            + [ones(gw), zeros(gw)] * 3)
    cscale = jnp.concatenate([ones(C_DA_Q) * DA_DH ** -0.5, ones(C_DA_K), ones(C_DA_V),
                              ones(C_NSA_Q) * NSA_DH ** -0.5, ones(C_NSA_KV)])
    return rmask[None, :], cscale[None, :]


def _block_expand(width, nkeys):
    blk = jnp.arange(width, dtype=I32)[:, None]
    key = jnp.arange(nkeys, dtype=I32)[None, :]
    return (key // BLOCK == blk).astype(BF16)


def _pad_rows(x, axis, size):
    pad = [(0, 0)] * x.ndim
    pad[axis] = (0, size - x.shape[axis])
    return jnp.pad(x, pad)


def _cmp_layout(kc, b, nc, width):
    x = kc.reshape(b, nc, NSA_KV, NSA_DH).transpose(0, 2, 1, 3)
    return _pad_rows(x, 2, width).astype(BF16)


def kernel(x_prompt, x_sample, cache_da_kv, cache_nsa_kv, cache_win_kv, page_table, p_prompt, p_sample,
           g_mix, w_in, da_lambda, da_subln, nsa_cpos, nsa_cmp, w_da_up, w_nsa_up, w_o, g_ffn,
           w_rg, b_rg, w_re, b_re, w_e_gate, w_e_up, w_e_down, g_ple, w_ple_gate, w_ple, g_final):
    b, s, d = x_prompt.shape
    nb, t4, _ = x_sample.shape
    depth = g_mix.shape[0]
    npr, nsm = b * s, nb * t4
    n = npr + nsm
    page = cache_da_kv.shape[2]
    npages = page_table.shape[1]
    past = npages * page
    wb = cache_win_kv.shape[2]
    nchunk = d // LANES
    rq = t4 * DA_REP
    assert rq % 8 == 0 and page % BLOCK == 0 and s % BLOCK == 0 and t4 <= page and d % LANES == 0

    xa = jnp.concatenate([x_prompt.reshape(npr, d), x_sample.reshape(nsm, d)], axis=0)
    pos = jnp.concatenate([jnp.tile(jnp.arange(s, dtype=I32), b), jnp.tile(past + jnp.arange(t4, dtype=I32), nb)])
    cos, sa, sb = _rope_tables(pos)
    rmask, cscale = _column_tables()

    nc_p, ns_p = s // BLOCK, max(-(-s // BLOCK), N_SELECT)
    tot_s = past + t4
    nc_s, ns_s = tot_s // BLOCK, max(-(-tot_s // BLOCK), N_SELECT)
    wid_p, wid_s = _round_up(ns_p, LANES), _round_up(ns_s, LANES)
    emat_p = _block_expand(wid_p, s)
    pg_s, nsteps_s = _page_steps(npages)
    emat_s = _block_expand(wid_s, (nsteps_s + 1) * pg_s * page)

    col = jnp.arange(LANES, dtype=I32)[:, None]
    lane = jnp.arange(NSA_HEADS * NSA_DH, dtype=I32)[None, :]
    gate_expand = jnp.stack([(col == br * NSA_HEADS + lane // NSA_DH) for br in range(3)]).astype(BF16)

    n_assign = 2 * n
    shift = int(math.log2(EXPERT_ROWS))
    nbt = -(-n_assign // EXPERT_ROWS) + N_EXPERTS
    tm_rank = _pick_tile(n, 640, 16)
    tri = jnp.tril(jnp.ones((tm_rank, tm_rank), BF16), -1)

    outs = {k: [] for k in ("da_p", "da_s", "ns_p", "ns_s", "win_p", "win_s")}
    for i in range(depth):
        lam_init = 0.8 - 0.6 * math.exp(-0.3 * i)
        wi = w_in[i]
        g1 = g_mix[i][None, :]
        zf, zb = _proj_rope(xa, g1, wi[:, :S5].astype(BF16), cos, sa, sb, rmask, cscale)
        wg_cols = wi[:, S5:S6].reshape(d, NSA_HEADS, 3).transpose(0, 2, 1).reshape(d, C_NSA_G)
        gns = _gate_proj(xa, g1, _pad_rows(wg_cols, 1, LANES).astype(BF16))
        gmg = _gate_proj(xa, g1, wi[:, S6:].astype(BF16))
        lam_p = da_lambda[i].astype(F32)
        subln = da_subln[i][None, :].astype(F32)
        wp = jnp.tile(nsa_cpos[i].astype(F32), (1, 1, NSA_KV))
        wcm = nsa_cmp[i].astype(F32)
        zc = jnp.zeros_like(wcm)
        wc = jnp.concatenate([jnp.concatenate([wcm, zc], axis=2), jnp.concatenate([zc, wcm], axis=2)], axis=1)

        zs = zb[npr:].reshape(nb, t4, S5)
        kd0, kd1 = S1 - PROJ_F32_FROM, S3 - PROJ_F32_FROM
        kn0, kn1 = S4 - PROJ_F32_FROM, S5 - PROJ_F32_FROM
        zfs = zf[npr:].reshape(nb, t4, kn1)
        q6 = zs[..., :S1].reshape(nb, t4, DA_KV, DA_REP, 2, DA_DH).transpose(0, 2, 4, 1, 3, 5)
        eye_g = jnp.eye(DA_KV, dtype=BF16)
        eye_c = jnp.eye(2, dtype=BF16)
        qbd = (q6[:, :, :, :, :, None, None, :] * eye_g[None, :, None, None, None, :, None, None]
               * eye_c[None, None, :, None, None, None, :, None]).reshape(nb, DA_KV * 2 * rq, DA_KV * 2 * DA_DH)
        cda = cache_da_kv[i].reshape(cache_da_kv.shape[1], page * 2 * DA_KV, DA_DV)
        new_da = _pad_rows(zfs[..., kd0:kd1], 1, page).reshape(nb, page * 2 * DA_KV, DA_DV)
        oda_s = _da_decode(page_table, qbd, cda, new_da, lam_p, subln, lam_init)
        oda_s = oda_s.reshape(nb, DA_KV, t4, DA_REP, DA_DV).transpose(0, 2, 1, 3, 4).reshape(nsm, DA_HEADS * DA_DV)

        qns = zs[..., S3:S4].reshape(nb, t4, NSA_KV, NSA_REP, NSA_DH).transpose(0, 2, 1, 3, 4)
        eye_n = jnp.eye(NSA_KV, dtype=BF16)
        qnb = (qns[:, :, :, :, None, :] * eye_n[None, :, None, None, :, None]
               ).reshape(nb, NSA_KV * t4 * NSA_REP, LANES)
        cns_t = cache_nsa_kv[i].transpose(0, 2, 3, 4, 1)
        cwin_t = cache_win_kv[i].transpose(0, 2, 3, 4, 1)
        kw = NSA_KV * NSA_DH
        wt = jnp.tile(nsa_cpos[i].astype(F32).transpose(0, 2, 1)[:, None],
                      (1, NSA_KV, 1, page // BLOCK)).reshape(2 * kw, page)
        ocmp_s, sel_s = _cmp_decode(page_table, qns.reshape(nb, NSA_KV, t4 * NSA_REP, NSA_DH), cns_t, wt,
                                    nsa_cmp[i].astype(F32).transpose(0, 2, 1),
                                    past=past, t4=t4, nc=nc_s, ns=ns_s, width=wid_s)
        ocmp_s = ocmp_s.reshape(nb, NSA_KV, t4, NSA_REP, NSA_DH).transpose(0, 2, 1, 3, 4
                                                                             ).reshape(nsm, NSA_HEADS * NSA_DH)
        selrows = jnp.broadcast_to(sel_s[:, :, :, None, :], (nb, NSA_KV, t4, NSA_REP, wid_s)
                                   ).reshape(nb, NSA_KV * t4 * NSA_REP, wid_s)
        new_t = lambda x: _pad_rows(x.reshape(nb, t4, 2, NSA_KV, NSA_DH), 1, page).transpose(0, 2, 3, 4, 1)
        oslc_s = _slc_decode(page_table, qnb, selrows, emat_s, cns_t, new_t(zfs[..., kn0 + 2 * kw:kn0 + 4 * kw]),
                             past=past, t4=t4)
        owin_s = _win_decode(qnb, cwin_t, new_t(zfs[..., kn0 + 4 * kw:]), past=past, t4=t4)

        def halves(o):
            o = o.reshape(nb, NSA_KV, t4, NSA_REP, NSA_KV, NSA_DH)
            o = jnp.stack([o[:, g, :, :, g] for g in range(NSA_KV)], axis=2)
            return o.reshape(nsm, NSA_HEADS * NSA_DH)

        oslc_s, owin_s = halves(oslc_s), halves(owin_s)

        seeded = lambda rows: jnp.concatenate([jnp.zeros((npr, rows.shape[1]), rows.dtype), rows], axis=0)
        oda = _da_flash(zb, seeded(oda_s.astype(BF16)), b, s, lam_p, subln, lam_init)
        kc, vc = _compress_prompt(zf, wp, wc, b, s)
        kc2 = _cmp_layout(kc, b, nc_p, wid_p)
        kc2 = jnp.concatenate([kc2, kc2], axis=-1)
        vcl = _cmp_layout(vc, b, nc_p, wid_p)
        vz = jnp.zeros_like(vcl)
        vc2 = jnp.stack([jnp.concatenate([vcl, vz], axis=-1), jnp.concatenate([vz, vcl], axis=-1)], axis=2)
        ocmp, sel_p = _cmp_select(zb, kc2, vc2, seeded(ocmp_s), b, s, nc=nc_p, ns=ns_p)
        kvp = zb[:npr, S4:S5].reshape(npr, 6, NSA_KV, NSA_DH)
        dup = lambda k: jnp.concatenate([k, k], axis=-1).reshape(npr, NSA_KV * LANES)
        with_ones = lambda v: jnp.concatenate([v, jnp.ones_like(v)], axis=-1).reshape(npr, NSA_KV * LANES)
        oslc = _nsa_flash(zb, dup(kvp[:, 2]), with_ones(kvp[:, 3]), seeded(oslc_s), b, s, sel_p, emat_p)
        owin = _nsa_flash(zb, dup(kvp[:, 4]), with_ones(kvp[:, 5]), seeded(owin_s), b, s)

        mrg =_merge(oda, ocmp, oslc, owin, gns, gmg, gate_expand,
                     w_da_up[i].astype(BF16), w_nsa_up[i].astype(BF16))
        wr = _pad_rows(jnp.concatenate([w_rg[i], w_re[i]], axis=1).astype(F32), 1, LANES)
        br = _pad_rows(jnp.concatenate([b_rg[i], b_re[i]]).astype(F32)[None, :], 1, LANES)
        x1, hrows, route = _wo_route(mrg, xa, w_o[i].astype(BF16), g_ffn[i][None, :], wr, br)
        rank, sizes = _expert_ranks(route, tri)
        dest, be, nu = _destinations(route, rank, sizes, nbt, shift)
        dest_flat = dest[:, :2].reshape(n_assign)
        ys = _experts(dest_flat, be[:nbt, 0], nu[0, :1], hrows.reshape(n, nchunk, LANES),
                      w_e_gate[i], w_e_up[i], w_e_down[i], EXPERT_ROWS)
        pa = jnp.concatenate([p_prompt[i].reshape(npr, -1), p_sample[i].reshape(nsm, -1)], axis=0)
        xa = _combine(dest_flat, ys.reshape(nbt * EXPERT_ROWS, nchunk, LANES), route, x1, pa,
                      g_ple[i][None, :], w_ple_gate[i].astype(BF16), w_ple[i].astype(BF16),
                      g_final[None, :], i == depth - 1)

        outs["da_p"].append(zf[:npr, kd0:kd1].reshape(b, s, 2, DA_KV, DA_DV))
        outs["da_s"].append(zfs[..., kd0:kd1].reshape(nb, t4, 2, DA_KV, DA_DV))
        outs["ns_p"].append(zf[:npr, kn0:kn0 + 4 * kw].reshape(b, s, 4, NSA_KV, NSA_DH))
        outs["ns_s"].append(zfs[..., kn0:kn0 + 4 * kw].reshape(nb, t4, 4, NSA_KV, NSA_DH))
        wrows = min(WINDOW, s)
        win_p = jnp.stack([zf[bi * s + s - wrows:(bi + 1) * s, kn0 + 4 * kw:] for bi in range(b)])
        outs["win_p"].append(win_p.reshape(b, wrows, 2, NSA_KV, NSA_DH))
        win_s = jnp.concatenate([cache_win_kv[i], zfs[..., kn0 + 4 * kw:].reshape(nb, t4, 2, NSA_KV, NSA_DH)], axis=1)
        outs["win_s"].append(win_s[:, win_s.shape[1] - wb:])

    y_prompt = xa[:npr].reshape(b, s, d)
    y_sample = xa[npr:].reshape(nb, t4, d)
    return (y_prompt, y_sample, jnp.stack(outs["da_p"]), jnp.stack(outs["da_s"]), jnp.stack(outs["ns_p"]),
            jnp.stack(outs["ns_s"]), jnp.stack(outs["win_p"]), jnp.stack(outs["win_s"]))
```
